```python
import math
import jax, jax.numpy as jnp
from jax import lax
import numpy as np

D_MODEL = 1024
BATCH = 4
SEQ = 4096
DEPTH = 2
DEC_BATCH = 4
DEC_SEQ = 8192
PAST_LEN = 128

N_MEM = 256
BRANCH_W = 512
N_BRANCH = 5
Q_BLOCK = 128
NEG_INF = -1e30
EPS = 1e-6
A_HEADS = 4
A_DK = 64
A_DV = 128
B_HEADS = 4
B_Q_LORA = 256
B_KV_LORA = 128
B_NOPE = 64
B_ROPE = 32
B_DV = 128
ROPE_THETA = 10000.0
C_QH = 8
C_KVH = 2
C_HD = 64
C_WINDOW = 128
C_BLOCK = 128
D_HEADS = 8
D_HD = 64
DIL_PAIRS = ((128, 1), (512, 4), (2048, 16))
N_DIL = 3
M_HEADS = 4
M_HD = 128

IN_LAYOUT = (
    ("a_q", 2 * A_HEADS * A_DK), ("a_k", 2 * A_HEADS * A_DK), ("a_v", A_HEADS * A_DV),
    ("b_cq", B_Q_LORA), ("b_ckv", B_KV_LORA), ("b_kr", B_ROPE),
    ("c_q", C_QH * C_HD), ("c_k", C_KVH * C_HD), ("c_v", C_KVH * C_HD),
    ("d_q", N_DIL * D_HEADS * D_HD), ("d_k", N_DIL * D_HEADS * D_HD), ("d_v", N_DIL * D_HEADS * D_HD),
    ("m_q", M_HEADS * M_HD),
    ("z", N_BRANCH * BRANCH_W),
    ("g", N_BRANCH * D_MODEL),
)
N_IN = sum(size for _, size in IN_LAYOUT)

kernel_name = "hybrid_gated_bidir_encoder"


def rms_norm(x, g):
    xf = x.astype(jnp.float32)
    y = xf * lax.rsqrt(jnp.mean(xf * xf, axis=-1, keepdims=True) + EPS)
    return (y * g.astype(jnp.float32)).astype(x.dtype)


def alibi_slopes(n):
    return 2.0 ** (-8.0 * jnp.arange(1, n + 1, dtype=jnp.float32) / n)


def rope_tables(S):
    half = B_ROPE // 2
    inv = ROPE_THETA ** (-jnp.arange(half, dtype=jnp.float32) / half)
    ang = jnp.arange(S, dtype=jnp.float32)[:, None] * inv[None, :]
    return jnp.cos(ang), jnp.sin(ang)


def apply_rope(x, cos, sin):
    half = x.shape[-1] // 2
    xf = x.astype(jnp.float32)
    x1, x2 = xf[..., :half], xf[..., half:]
    return jnp.concatenate([x1 * cos - x2 * sin, x2 * cos + x1 * sin], axis=-1).astype(x.dtype)


def to_blocks(t):
    B, S = t.shape[:2]
    return jnp.moveaxis(t.reshape(B, S // Q_BLOCK, Q_BLOCK, *t.shape[2:]), 1, 0)


def from_blocks(t):
    nb, B, qb = t.shape[:3]
    return jnp.moveaxis(t, 0, 1).reshape(B, nb * qb, *t.shape[3:])


def split_columns(proj):
    parts, start = {}, 0
    for name, size in IN_LAYOUT:
        parts[name] = proj[..., start:start + size]
        start += size
    return parts


def diff_attention(q, k, v, lam):
    S = q.shape[1]
    pos = jnp.arange(S)
    slopes = alibi_slopes(q.shape[3])
    scale = q.shape[-1] ** -0.5

    def block(args):
        qb, qp = args
        s = jnp.einsum("bqmhd,bkmhd->bmhqk", qb, k).astype(jnp.float32) * scale
        bias = -slopes[:, None, None] * jnp.abs(qp[:, None] - pos[None, :])
        p = jax.nn.softmax(s + bias, axis=-1)
        a = p[:, 0] - lam * p[:, 1]
        return jnp.einsum("bhqk,bkhd->bqhd", a.astype(v.dtype), v)

    return from_blocks(lax.map(block, (to_blocks(q), pos.reshape(-1, Q_BLOCK))))


def mla_attention(qn, qr, kn, kr, v):
    scale = (qn.shape[-1] + qr.shape[-1]) ** -0.5

    def block(args):
        qnb, qrb = args
        s = (jnp.einsum("bqhd,bkhd->bhqk", qnb, kn)
             + jnp.einsum("bqhr,bkr->bhqk", qrb, kr)).astype(jnp.float32) * scale
        p = jax.nn.softmax(s, axis=-1)
        return jnp.einsum("bhqk,bkhd->bqhd", p.astype(v.dtype), v)

    return from_blocks(lax.map(block, (to_blocks(qn), to_blocks(qr))))


def window_gqa(q, k, v, sink):
    B, S, Hq, hd = q.shape
    Hkv = k.shape[2]
    G = Hq // Hkv
    nb = S // C_BLOCK
    scale = hd ** -0.5
    qb = q.reshape(B, nb, C_BLOCK, Hkv, G, hd)

    def band(t):
        tp = jnp.pad(t, ((0, 0), (C_BLOCK, C_BLOCK), (0, 0), (0, 0))).reshape(B, nb + 2, C_BLOCK, Hkv, hd)
        return jnp.concatenate([tp[:, :-2], tp[:, 1:-1], tp[:, 2:]], axis=2)

    kb, vb = band(k), band(v)
    s = jnp.einsum("bnqhgd,bnkhd->bnhgqk", qb, kb).astype(jnp.float32) * scale
    koff = jnp.arange(3 * C_BLOCK) - C_BLOCK
    rel = koff[None, :] - jnp.arange(C_BLOCK)[:, None]
    kabs = jnp.arange(nb)[:, None] * C_BLOCK + koff[None, :]
    valid = (jnp.abs(rel) <= C_WINDOW)[None] & ((kabs >= 0) & (kabs < S))[:, None, :]
    slopes = alibi_slopes(Hq).reshape(Hkv, G)
    logits = s - slopes[:, :, None, None] * jnp.abs(rel)
    logits = jnp.where(valid[None, :, None, None], logits, NEG_INF)
    sink_l = jnp.broadcast_to(sink.astype(jnp.float32).reshape(Hkv, G)[:, :, None, None],
                              logits.shape[:-1] + (1,))
    p = jax.nn.softmax(jnp.concatenate([logits, sink_l], axis=-1), axis=-1)[..., :-1]
    o = jnp.einsum("bnhgqk,bnkhd->bnqhgd", p.astype(v.dtype), vb)
    return o.reshape(B, S, Hq, hd)


def dilated_attention(q, k, v):
    S, H = q.shape[1], q.shape[3]
    scale = q.shape[-1] ** -0.5
    slopes = alibi_slopes(H)
    pos = jnp.arange(S)
    ks = [k[:, :, g] for g in range(N_DIL)]
    vs = [v[:, :, g] for g in range(N_DIL)]
    offsets = [dil * jnp.arange(-(w // (2 * dil)), w // (2 * dil) + 1) for w, dil in DIL_PAIRS]

    def block(args):
        qb, qp = args
        outs, lses = [], []
        for g in range(N_DIL):
            off = offsets[g]
            idx = qp[:, None] + off[None, :]
            valid = (idx >= 0) & (idx < S)
            idx = jnp.clip(idx, 0, S - 1)
            kg = jnp.take(ks[g], idx, axis=1)
            vg = jnp.take(vs[g], idx, axis=1)
            s = jnp.einsum("bqhd,bqjhd->bhqj", qb[:, :, g], kg).astype(jnp.float32) * scale
            s = s - slopes[:, None, None] * jnp.abs(off)[None, None, :]
            s = jnp.where(valid[None, None], s, NEG_INF)
            m = jnp.max(s, axis=-1, keepdims=True)
            e = jnp.exp(s - m)
            den = jnp.sum(e, axis=-1, keepdims=True)
            outs.append(jnp.einsum("bhqj,bqjhd->bqhd", (e / den).astype(v.dtype), vg))
            lses.append((m + jnp.log(den))[..., 0])
        w = jax.nn.softmax(jnp.stack(lses, axis=0), axis=0)
        w = jnp.swapaxes(w, 2, 3)[..., None]
        return jnp.sum(w * jnp.stack(outs, axis=0).astype(jnp.float32), axis=0).astype(v.dtype)

    return from_blocks(lax.map(block, (to_blocks(q), pos.reshape(-1, Q_BLOCK))))


def memory_attention(q, k, v):
    scale = q.shape[-1] ** -0.5
    s = jnp.einsum("bqhd,bkhd->bhqk", q, k).astype(jnp.float32) * scale
    p = jax.nn.softmax(s, axis=-1)
    return jnp.einsum("bhqk,bkhd->bqhd", p.astype(v.dtype), v)


def encoder_layer(x, mem, layer_idx, p):
    B, S, _ = x.shape
    h = rms_norm(x, p["norm_g"])
    parts = split_columns(h @ p["w_in"])

    aq = rms_norm(parts["a_q"].reshape(B, S, 2, A_HEADS, A_DK), p["a_qn"])
    ak = rms_norm(parts["a_k"].reshape(B, S, 2, A_HEADS, A_DK), p["a_kn"])
    av = parts["a_v"].reshape(B, S, A_HEADS, A_DV)
    lam_init = 0.8 - 0.6 * math.exp(-0.3 * layer_idx)
    lf = p["a_lam"].astype(jnp.float32)
    lam = jnp.exp(jnp.sum(lf[0] * lf[1])) - jnp.exp(jnp.sum(lf[2] * lf[3])) + lam_init
    oa = diff_attention(aq, ak, av, lam)
    oa = (rms_norm(oa, p["a_hn"]) * (1.0 - lam_init)).reshape(B, S, BRANCH_W)

    cq = rms_norm(parts["b_cq"], p["b_cqn"])
    qb = (cq @ p["w_qb"]).reshape(B, S, B_HEADS, B_NOPE + B_ROPE)
    ckv = rms_norm(parts["b_ckv"], p["b_ckvn"])
    kvb = (ckv @ p["w_kvb"]).reshape(B, S, B_HEADS, B_NOPE + B_DV)
    cos, sin = rope_tables(S)
    qn = rms_norm(qb[..., :B_NOPE], p["b_qn"][:B_NOPE])
    qr = apply_rope(rms_norm(qb[..., B_NOPE:], p["b_qn"][B_NOPE:]), cos[:, None], sin[:, None])
    kn = rms_norm(kvb[..., :B_NOPE], p["b_kn"][:B_NOPE])
    kr = apply_rope(rms_norm(parts["b_kr"], p["b_kn"][B_NOPE:]), cos, sin)
    ob = mla_attention(qn, qr, kn, kr, kvb[..., B_NOPE:]).reshape(B, S, BRANCH_W)

    cq_ = rms_norm(parts["c_q"].reshape(B, S, C_QH, C_HD), p["c_qn"])
    ck = rms_norm(parts["c_k"].reshape(B, S, C_KVH, C_HD), p["c_kn"])
    cv = parts["c_v"].reshape(B, S, C_KVH, C_HD)
    oc = window_gqa(cq_, ck, cv, p["c_sink"]).reshape(B, S, BRANCH_W)

    dq = rms_norm(parts["d_q"].reshape(B, S, N_DIL, D_HEADS, D_HD), p["d_qn"])
    dk = rms_norm(parts["d_k"].reshape(B, S, N_DIL, D_HEADS, D_HD), p["d_kn"])
    dv = parts["d_v"].reshape(B, S, N_DIL, D_HEADS, D_HD)
    od = dilated_attention(dq, dk, dv).reshape(B, S, BRANCH_W)

    mn = rms_norm(mem, p["m_norm"])
    mkv = (mn @ p["w_mem_kv"]).reshape(B, mem.shape[1], 2, M_HEADS, M_HD)
    mq = rms_norm(parts["m_q"].reshape(B, S, M_HEADS, M_HD), p["m_qn"])
    mk = rms_norm(mkv[:, :, 0], p["m_kn"])
    om = memory_attention(mq, mk, mkv[:, :, 1]).reshape(B, S, BRANCH_W)

    z, g = parts["z"], parts["g"]
    acc = jnp.zeros_like(x)
    for i, o in enumerate((oa, ob, oc, od, om)):
        zi = jax.nn.silu(z[..., i * BRANCH_W:(i + 1) * BRANCH_W])
        gi = jax.nn.sigmoid(g[..., i * D_MODEL:(i + 1) * D_MODEL] + p["b_gate"][i])
        acc = acc + gi * ((o * zi) @ p["w_br"][i])
    return x + acc @ p["w_out"]


def setup_inputs(seed: int = 0) -> dict:
    key = jax.random.key(seed)
    ks = jax.random.split(key, 32)
    L, D = DEPTH, D_MODEL
    f32 = jnp.float32

    def nrm(k, shape, scale):
        return jax.random.normal(k, shape, f32) * scale

    def gain(k, shape):
        return 1.0 + 0.01 * jax.random.normal(k, shape, f32)

    return {
        "x_prompt": nrm(ks[0], (BATCH, SEQ, D), 1.0),
        "x_sample": nrm(ks[1], (DEC_BATCH, DEC_SEQ, D), 1.0),
        "mem_prompt": nrm(ks[2], (BATCH, N_MEM, D), 1.0),
        "mem_sample": nrm(ks[3], (DEC_BATCH, N_MEM, D), 1.0),
        "norm_g": gain(ks[4], (L, D)),
        "w_in": nrm(ks[5], (L, D, N_IN), D ** -0.5),
        "a_qn": gain(ks[6], (L, A_DK)),
        "a_kn": gain(ks[7], (L, A_DK)),
        "a_lam": nrm(ks[8], (L, 4, A_DK), 0.1),
        "a_hn": gain(ks[9], (L, A_DV)),
        "b_cqn": gain(ks[10], (L, B_Q_LORA)),
        "b_ckvn": gain(ks[11], (L, B_KV_LORA)),
        "w_qb": nrm(ks[12], (L, B_Q_LORA, B_HEADS * (B_NOPE + B_ROPE)), B_Q_LORA ** -0.5),
        "w_kvb": nrm(ks[13], (L, B_KV_LORA, B_HEADS * (B_NOPE + B_DV)), B_KV_LORA ** -0.5),
        "b_qn": gain(ks[14], (L, B_NOPE + B_ROPE)),
        "b_kn": gain(ks[15], (L, B_NOPE + B_ROPE)),
        "c_qn": gain(ks[16], (L, C_HD)),
        "c_kn": gain(ks[17], (L, C_HD)),
        "c_sink": nrm(ks[18], (L, C_QH), 0.5),
        "d_qn": gain(ks[19], (L, D_HD)),
        "d_kn": gain(ks[20], (L, D_HD)),
        "m_norm": gain(ks[21], (L, D)),
        "w_mem_kv": nrm(ks[22], (L, D, 2 * M_HEADS * M_HD), D ** -0.5),
        "m_qn": gain(ks[23], (L, M_HD)),
        "m_kn": gain(ks[24], (L, M_HD)),
        "b_gate": nrm(ks[25], (L, N_BRANCH, D), 0.01),
        "w_br": nrm(ks[26], (L, N_BRANCH, BRANCH_W, D), BRANCH_W ** -0.5),
        "w_out": nrm(ks[27], (L, D, D), D ** -0.5),
    }


def reference(x_prompt, x_sample, mem_prompt, mem_sample, norm_g, w_in, a_qn, a_kn, a_lam, a_hn,
              b_cqn, b_ckvn, w_qb, w_kvb, b_qn, b_kn, c_qn, c_kn, c_sink, d_qn, d_kn,
              m_norm, w_mem_kv, m_qn, m_kn, b_gate, w_br, w_out):
    y_prompt, y_sample = x_prompt, x_sample
    for l in range(DEPTH):
        p = {
            "norm_g": norm_g[l], "w_in": w_in[l],
            "a_qn": a_qn[l], "a_kn": a_kn[l], "a_lam": a_lam[l], "a_hn": a_hn[l],
            "b_cqn": b_cqn[l], "b_ckvn": b_ckvn[l], "w_qb": w_qb[l], "w_kvb": w_kvb[l],
            "b_qn": b_qn[l], "b_kn": b_kn[l],
            "c_qn": c_qn[l], "c_kn": c_kn[l], "c_sink": c_sink[l],
            "d_qn": d_qn[l], "d_kn": d_kn[l],
            "m_norm": m_norm[l], "w_mem_kv": w_mem_kv[l], "m_qn": m_qn[l], "m_kn": m_kn[l],
            "b_gate": b_gate[l], "w_br": w_br[l], "w_out": w_out[l],
        }
        y_prompt = encoder_layer(y_prompt, mem_prompt, l, p)
        y_sample = encoder_layer(y_sample, mem_sample, l, p)
    return (y_prompt, y_sample)
```

```python
import functools
import math

import numpy as np
import jax
import jax.numpy as jnp
from jax import lax
from jax.experimental import pallas as pl
from jax.experimental.pallas import tpu as pltpu

F32 = jnp.float32
BF16 = jnp.bfloat16

D_MODEL = 1024
N_MEM = 256
BRANCH_W = 512
N_BRANCH = 5
NEG_INF = -1e30
EPS = 1e-6
A_HEADS, A_DK, A_DV = 4, 64, 128
B_HEADS, B_Q_LORA, B_KV_LORA, B_NOPE, B_ROPE, B_DV = 4, 256, 128, 64, 32, 128
ROPE_THETA = 10000.0
C_QH, C_KVH, C_HD, C_WINDOW = 8, 2, 64, 128
D_HEADS, D_HD = 8, 64
DIL_PAIRS = ((128, 1), (512, 4), (2048, 16))
N_DIL = 3
M_HEADS, M_HD = 4, 128

OFF_A_Q = 0
OFF_A_K = OFF_A_Q + 2 * A_HEADS * A_DK
OFF_A_V = OFF_A_K + 2 * A_HEADS * A_DK
OFF_B_CQ = OFF_A_V + A_HEADS * A_DV
OFF_B_CKV = OFF_B_CQ + B_Q_LORA
OFF_B_KR = OFF_B_CKV + B_KV_LORA
OFF_C_Q = OFF_B_KR + B_ROPE
OFF_C_K = OFF_C_Q + C_QH * C_HD
OFF_C_V = OFF_C_K + C_KVH * C_HD
OFF_D_Q = OFF_C_V + C_KVH * C_HD
OFF_D_K = OFF_D_Q + N_DIL * D_HEADS * D_HD
OFF_D_V = OFF_D_K + N_DIL * D_HEADS * D_HD
OFF_M_Q = OFF_D_V + N_DIL * D_HEADS * D_HD
OFF_Z = OFF_M_Q + M_HEADS * M_HD
OFF_G = OFF_Z + N_BRANCH * BRANCH_W
N_IN = OFF_G + N_BRANCH * D_MODEL

LANES = 128
COL_TILE = 256
VMEM_LIMIT = 56 * 1024 * 1024

SEG_NONE, SEG64, SEG128 = -1, 0, 1


def _cparams(sem):
    return pltpu.CompilerParams(dimension_semantics=sem, vmem_limit_bytes=VMEM_LIMIT)


def _resident(shape, index_map):
    return pl.BlockSpec(shape, index_map, pipeline_mode=pl.Buffered(1))


def _seg_matrix(width, segs):
    m = np.zeros((width, width), np.float32)
    for s, n in segs:
        m[s:s + n, s:s + n] = 1.0 / n
    return m


def _proj_kernel(x_ref, ng_ref, w_ref, gain_ref, mseg_ref, *out_refs, tiles):
    x = x_ref[...]
    ms = jnp.mean(x * x, axis=-1, keepdims=True)
    h = ((x * lax.rsqrt(ms + EPS)) * ng_ref[...]).astype(BF16)
    for c0, seg, oi, oc in tiles:
        y = jnp.dot(h, w_ref[:, c0:c0 + COL_TILE], preferred_element_type=F32)
        if seg != SEG_NONE:
            sq = (y * y).astype(BF16)
            segms = jnp.dot(sq, mseg_ref[seg], preferred_element_type=F32)
            y = (y * lax.rsqrt(segms + EPS)) * gain_ref[:, c0:c0 + COL_TILE]
        out_refs[oi][:, oc:oc + COL_TILE] = y.astype(BF16)


def _proj(x2d, norm_gain, w, gain, tiles, out_widths, tm):
    t, d = x2d.shape
    n = w.shape[1]
    tm = min(tm, t)
    mseg = jnp.asarray(
        np.stack([_seg_matrix(COL_TILE, [(s, 64) for s in range(0, COL_TILE, 64)]),
                  _seg_matrix(COL_TILE, [(s, 128) for s in range(0, COL_TILE, 128)])]), BF16)
    return pl.pallas_call(
        functools.partial(_proj_kernel, tiles=tuple(tiles)),
        grid=(t // tm,),
        in_specs=[
            pl.BlockSpec((tm, d), lambda i: (i, 0)),
            _resident((1, d), lambda i: (0, 0)),
            _resident((d, n), lambda i: (0, 0)),
            _resident((1, n), lambda i: (0, 0)),
            _resident((2, COL_TILE, COL_TILE), lambda i: (0, 0, 0)),
        ],
        out_specs=[pl.BlockSpec((tm, wd), lambda i: (i, 0)) for wd in out_widths],
        out_shape=[jax.ShapeDtypeStruct((t, wd), BF16) for wd in out_widths],
        compiler_params=_cparams(("parallel",)),
        name="proj",
    )(x2d, norm_gain.reshape(1, d), w, gain, mseg)


def _main_plan():
    zero = N_IN
    cols, gains, tiles = [], [], []
    widths = [3 * 512, 512, 1024, N_DIL * 3 * 512, 512]

    def add(out_idx, out_col, src_cols, seg, gain_key):
        assert len(src_cols) % COL_TILE == 0
        c0 = len(cols)
        cols.extend(src_cols)
        gains.extend([gain_key] * len(src_cols))
        for k in range(len(src_cols) // COL_TILE):
            tiles.append((c0 + k * COL_TILE, seg, out_idx, out_col + k * COL_TILE))

    hm = [m * A_HEADS * A_DK + h * A_DK + d for h in range(A_HEADS) for m in range(2) for d in range(A_DK)]
    add(0, 0, [OFF_A_Q + c for c in hm], SEG64, "a_q")
    add(0, 512, [OFF_A_K + c for c in hm], SEG64, "a_k")
    add(0, 1024, [OFF_A_V + c for c in range(A_HEADS * A_DV)], SEG_NONE, None)
    braw = ([OFF_B_CQ + c for c in range(B_Q_LORA)] + [OFF_B_CKV + c for c in range(B_KV_LORA)]
            + [OFF_B_KR + c for c in range(B_ROPE)] + [zero] * (512 - B_Q_LORA - B_KV_LORA - B_ROPE))
    add(1, 0, braw, SEG_NONE, None)
    add(2, 0, [OFF_C_Q + c for c in range(C_QH * C_HD)], SEG64, "c_q")
    dup = [kv * C_HD + d for kv in range(C_KVH) for _ in range(2) for d in range(C_HD)]
    add(2, 512, [OFF_C_K + c for c in dup], SEG64, "c_k")
    add(2, 768, [OFF_C_V + c for c in dup], SEG_NONE, None)
    for g in range(N_DIL):
        gsl = [g * D_HEADS * D_HD + c for c in range(D_HEADS * D_HD)]
        add(3, g * 1536, [OFF_D_Q + c for c in gsl], SEG64, "d_q")
        add(3, g * 1536 + 512, [OFF_D_K + c for c in gsl], SEG64, "d_k")
        add(3, g * 1536 + 1024, [OFF_D_V + c for c in gsl], SEG_NONE, None)
    add(4, 0, [OFF_M_Q + c for c in range(M_HEADS * M_HD)], SEG128, "m_q")
    return np.asarray(cols, np.int32), gains, tiles, widths


_MAIN_COLS, _MAIN_GAINS, _MAIN_TILES, _MAIN_WIDTHS = _main_plan()


def _main_gain_vector(p):
    per_key = {
        "a_q": p["a_qn"] * (A_DK ** -0.5), "a_k": p["a_kn"],
        "c_q": p["c_qn"] * (C_HD ** -0.5), "c_k": p["c_kn"],
        "d_q": p["d_qn"] * (D_HD ** -0.5), "d_k": p["d_kn"],
        "m_q": p["m_qn"] * (M_HD ** -0.5),
    }
    n = len(_MAIN_GAINS)
    pieces, k = [], 0
    while k < n:
        key = _MAIN_GAINS[k]
        k2 = k
        while k2 < n and _MAIN_GAINS[k2] == key:
            k2 += 1
        if key is None:
            pieces.append(jnp.ones((k2 - k,), F32))
        else:
            gvec = per_key[key].astype(F32)
            pieces.append(jnp.tile(gvec, (k2 - k) // gvec.shape[0]))
        k = k2
    return jnp.concatenate(pieces).reshape(1, n)


def _mla_prep_kernel(b_ref, wq_ref, wk_ref, wv_ref, place_ref, mseg_ref, cqg_ref, ckvg_ref,
                     qg_ref, kg_ref, ct_ref, s1_ref, s2_ref, q_out, k_out, v_out):
    braw = b_ref[...]
    cq = braw[:, :B_Q_LORA].astype(F32)
    cqn = ((cq * lax.rsqrt(jnp.mean(cq * cq, axis=-1, keepdims=True) + EPS)) * cqg_ref[...]).astype(BF16)
    ckv = braw[:, B_Q_LORA:B_Q_LORA + B_KV_LORA].astype(F32)
    ckvn = ((ckv * lax.rsqrt(jnp.mean(ckv * ckv, axis=-1, keepdims=True) + EPS)) * ckvg_ref[...]).astype(BF16)
    qb = jnp.dot(cqn, wq_ref[...], preferred_element_type=F32)
    kb = jnp.dot(ckvn, wk_ref[...], preferred_element_type=F32)
    kb = kb + jnp.dot(braw[:, B_Q_LORA + B_KV_LORA:], place_ref[...], preferred_element_type=F32)
    v_out[...] = jnp.dot(ckvn, wv_ref[...], preferred_element_type=F32).astype(BF16)
    ct, s1, s2 = ct_ref[...], s1_ref[...], s2_ref[...]

    def finish(t, gain_ref, out):
        for h in range(B_HEADS):
            y = t[:, h * LANES:(h + 1) * LANES]
            segms = jnp.dot((y * y).astype(BF16), mseg_ref[...], preferred_element_type=F32)
            yn = (y * lax.rsqrt(segms + EPS)) * gain_ref[...]
            half = B_ROPE // 2
            r = yn * ct + pltpu.roll(yn, half, 1) * s1 + pltpu.roll(yn, LANES - half, 1) * s2
            out[:, h * LANES:(h + 1) * LANES] = r.astype(BF16)

    finish(qb, qg_ref, q_out)
    finish(kb, kg_ref, k_out)


def _rope_tables(s):
    half = B_ROPE // 2
    inv = ROPE_THETA ** (-jnp.arange(half, dtype=F32) / half)
    ang = jnp.arange(s, dtype=F32)[:, None] * inv[None, :]
    cos, sin = jnp.cos(ang), jnp.sin(ang)
    one = jnp.ones((s, B_NOPE), F32)
    zpad = jnp.zeros((s, LANES - B_NOPE - B_ROPE), F32)
    z64 = jnp.zeros((s, B_NOPE), F32)
    zh = jnp.zeros((s, half), F32)
    ct = jnp.concatenate([one, cos, cos, zpad], axis=1)
    s1 = jnp.concatenate([z64, zh, sin, zpad], axis=1)
    s2 = jnp.concatenate([z64, -sin, zh, zpad], axis=1)
    return ct, s1, s2


def _mla_prep(braw, p, bsz, s, tm):
    t = braw.shape[0]
    tm = min(tm, s)
    pad = LANES - B_NOPE - B_ROPE
    wq = p["w_qb"].reshape(B_Q_LORA, B_HEADS, B_NOPE + B_ROPE)
    wq = jnp.pad(wq, ((0, 0), (0, 0), (0, pad))).reshape(B_Q_LORA, B_HEADS * LANES).astype(BF16)
    wkv = p["w_kvb"].reshape(B_KV_LORA, B_HEADS, B_NOPE + B_DV)
    wk = jnp.pad(wkv[:, :, :B_NOPE], ((0, 0), (0, 0), (0, LANES - B_NOPE))).reshape(B_KV_LORA, B_HEADS * LANES).astype(BF16)
    wv = wkv[:, :, B_NOPE:].reshape(B_KV_LORA, B_HEADS * B_DV).astype(BF16)
    place = np.zeros((LANES, B_HEADS * LANES), np.float32)
    for h in range(B_HEADS):
        for d in range(B_ROPE):
            place[d, h * LANES + B_NOPE + d] = 1.0
    mseg = _seg_matrix(LANES, [(0, B_NOPE), (B_NOPE, B_ROPE)])
    scale = (B_NOPE + B_ROPE) ** -0.5
    zp = jnp.zeros((pad,), F32)
    qg = jnp.concatenate([p["b_qn"].astype(F32) * scale, zp]).reshape(1, LANES)
    kg = jnp.concatenate([p["b_kn"].astype(F32), zp]).reshape(1, LANES)
    ct, s1, s2 = _rope_tables(s)
    nst = s // tm
    full = lambda shape: _resident(shape, lambda b, i: tuple(0 for _ in shape))
    row = lambda wd: pl.BlockSpec((tm, wd), lambda b, i: (b * nst + i, 0))
    tab = pl.BlockSpec((tm, LANES), lambda b, i: (i, 0))
    return pl.pallas_call(
        _mla_prep_kernel,
        grid=(bsz, nst),
        in_specs=[row(512), full((B_Q_LORA, 512)), full((B_KV_LORA, 512)), full((B_KV_LORA, 512)),
                  full((LANES, 512)), full((LANES, LANES)), full((1, B_Q_LORA)), full((1, B_KV_LORA)),
                  full((1, LANES)), full((1, LANES)), tab, tab, tab],
        out_specs=[row(512), row(512), row(512)],
        out_shape=[jax.ShapeDtypeStruct((t, 512), BF16)] * 3,
        compiler_params=_cparams(("parallel", "parallel")),
        name="mla_prep",
    )(braw, wq, wk, wv, jnp.asarray(place, BF16), jnp.asarray(mseg, BF16),
      p["b_cqn"].astype(F32).reshape(1, -1), p["b_ckvn"].astype(F32).reshape(1, -1), qg, kg, ct, s1, s2)


def _flash_kernel(slopes_ref, lam_ref, hn_ref, q_ref, k_ref, v_ref, o_ref, m_sc, l_sc, acc_sc,
                  *, nmaps, alibi, tq, tk, seq, lam_init):
    h = pl.program_id(1)
    q0 = pl.program_id(2) * tq
    q = q_ref[...]
    if nmaps == 2:
        lane = lax.broadcasted_iota(jnp.int32, q.shape, 1)
        qs = [jnp.where(lane < A_DK, q, jnp.zeros_like(q)), jnp.where(lane >= A_DK, q, jnp.zeros_like(q))]
    else:
        qs = [q]
    m_sc[...] = jnp.full(m_sc.shape, NEG_INF, F32)
    l_sc[...] = jnp.zeros(l_sc.shape, F32)
    acc_sc[...] = jnp.zeros(acc_sc.shape, F32)
    if alibi:
        slope = slopes_ref[h]
        dmat = (lax.broadcasted_iota(jnp.int32, (tq, tk), 0)
                - lax.broadcasted_iota(jnp.int32, (tq, tk), 1)).astype(F32)

    def body(j, carry):
        k0 = pl.multiple_of(j * tk, tk)
        k = k_ref[pl.ds(k0, tk), :]
        v = v_ref[pl.ds(k0, tk), :]
        if alibi:
            bias = -slope * jnp.abs(dmat + (q0 - k0).astype(F32))
        for m in range(nmaps):
            s = lax.dot_general(qs[m], k, (((1,), (1,)), ((), ())), preferred_element_type=F32)
            if alibi:
                s = s + bias
            m_old = m_sc[m]
            m_new = jnp.maximum(m_old, jnp.max(s, axis=1, keepdims=True))
            alpha = jnp.exp(m_old - m_new)
            pr = jnp.exp(s - m_new)
            l_sc[m] = alpha * l_sc[m] + jnp.sum(pr, axis=1, keepdims=True)
            acc_sc[m] = alpha * acc_sc[m] + jnp.dot(pr.astype(BF16), v, preferred_element_type=F32)
            m_sc[m] = m_new
        return carry

    lax.fori_loop(0, seq // tk, body, 0)
    if nmaps == 2:
        lf = lam_ref[...]
        e1 = jnp.exp(jnp.sum(lf[0:1] * lf[1:2], axis=1, keepdims=True))
        e2 = jnp.exp(jnp.sum(lf[2:3] * lf[3:4], axis=1, keepdims=True))
        lam = e1 - e2 + lam_init
        o = acc_sc[0] / l_sc[0] - lam * (acc_sc[1] / l_sc[1])
        o = (o * lax.rsqrt(jnp.mean(o * o, axis=-1, keepdims=True) + EPS)) * hn_ref[...]
        o = o * (1.0 - lam_init)
    else:
        o = acc_sc[0] / l_sc[0]
    o_ref[...] = o.astype(BF16)


def _flash(qa, ka, va, qcb, kcb, vcb, bsz, s, heads, nmaps, alibi, slopes, lam, hn, lam_init, tq, tk):
    t = qa.shape[0]
    tq, tk = min(tq, s), min(tk, s)
    nq = s // tq
    return pl.pallas_call(
        functools.partial(_flash_kernel, nmaps=nmaps, alibi=alibi, tq=tq, tk=tk, seq=s, lam_init=lam_init),
        grid=(bsz, heads, nq),
        in_specs=[
            pl.BlockSpec(memory_space=pltpu.SMEM),
            _resident(lam.shape, lambda b, h, i: (0, 0)),
            _resident(hn.shape, lambda b, h, i: (0, 0)),
            pl.BlockSpec((tq, LANES), lambda b, h, i: (b * nq + i, qcb + h)),
            pl.BlockSpec((s, LANES), lambda b, h, i: (b, kcb + h)),
            pl.BlockSpec((s, LANES), lambda b, h, i: (b, vcb + h)),
        ],
        out_specs=pl.BlockSpec((tq, LANES), lambda b, h, i: (b * nq + i, h)),
        out_shape=jax.ShapeDtypeStruct((t, heads * LANES), BF16),
        scratch_shapes=[pltpu.VMEM((nmaps, tq, 1), F32), pltpu.VMEM((nmaps, tq, 1), F32),
                        pltpu.VMEM((nmaps, tq, LANES), F32)],
        compiler_params=_cparams(("parallel", "parallel", "arbitrary")),
        name="flash_diff" if nmaps == 2 else "flash_mla",
    )(slopes, lam, hn, qa, ka, va)


def _band_kernel(slopes_ref, sink_ref, q_ref, k_ref, v_ref, *outs, window, qb, kw, ut, u_len, dil, use_sink, emit_lse):
    o_ref = outs[0]
    slab = pl.program_id(2)
    ubase = pl.program_id(3) * ut
    lane = lax.broadcasted_iota(jnp.int32, (kw, LANES), 1)
    qlane = lax.broadcasted_iota(jnp.int32, (qb, LANES), 1)
    dmat = (lax.broadcasted_iota(jnp.int32, (qb, kw), 0) - lax.broadcasted_iota(jnp.int32, (qb, kw), 1))
    for sb in range(ut // qb):
        u0 = ubase + sb * qb
        start = pl.multiple_of(jnp.clip(u0 - window, 0, u_len - kw), 16)
        q = q_ref[sb * qb:(sb + 1) * qb, :]
        k = k_ref[pl.ds(start, kw), :]
        v = v_ref[pl.ds(start, kw), :]
        rel = dmat + (u0 - start)
        valid = jnp.abs(rel) <= window
        dist = jnp.abs(rel).astype(F32) * float(dil)
        o_acc = jnp.zeros((qb, LANES), F32)
        lse_acc = jnp.zeros((qb, LANES), F32)
        for hh in range(2):
            head = slab * 2 + hh
            sel_q = (qlane < 64) if hh == 0 else (qlane >= 64)
            sel_v = (lane < 64) if hh == 0 else (lane >= 64)
            qh = jnp.where(sel_q, q, jnp.zeros_like(q))
            vh = jnp.where(sel_v, v, jnp.zeros_like(v))
            s = lax.dot_general(qh, k, (((1,), (1,)), ((), ())), preferred_element_type=F32)
            logits = jnp.where(valid, s - slopes_ref[head] * dist, NEG_INF)
            mx = jnp.max(logits, axis=1, keepdims=True)
            if use_sink:
                sk = sink_ref[head]
                mx = jnp.maximum(mx, sk)
            e = jnp.exp(logits - mx)
            den = jnp.sum(e, axis=1, keepdims=True)
            if use_sink:
                den = den + jnp.exp(sk - mx)
            pv = jnp.dot(e.astype(BF16), vh, preferred_element_type=F32)
            o_acc = o_acc + pv / den
            if emit_lse:
                lse_acc = lse_acc + jnp.where(sel_q, mx + jnp.log(den), 0.0)
        o_ref[sb * qb:(sb + 1) * qb, :] = o_acc.astype(BF16)
        if emit_lse:
            outs[1][sb * qb:(sb + 1) * qb, :] = lse_acc


def _band(src, qcb, kcb, vcb, bsz, s, dil, ncb_src, window, slopes, sink, use_sink, emit_lse, ut):
    t, c = src.shape
    u_len = s // dil
    qb = min(LANES, u_len)
    kw = min(qb + 2 * window, u_len)
    ut = min(ut, u_len)
    nu = u_len // ut
    srcv = src.reshape(bsz * u_len, dil * c)
    nslab = 4
    out_shapes = [jax.ShapeDtypeStruct((bsz * u_len, dil * 512), BF16)]
    out_specs = [pl.BlockSpec((ut, LANES), lambda b, r, sl, i: (b * nu + i, r * nslab + sl))]
    if emit_lse:
        out_shapes.append(jax.ShapeDtypeStruct((bsz * u_len, dil * 512), F32))
        out_specs.append(pl.BlockSpec((ut, LANES), lambda b, r, sl, i: (b * nu + i, r * nslab + sl)))
    res = pl.pallas_call(
        functools.partial(_band_kernel, window=window, qb=qb, kw=kw, ut=ut, u_len=u_len, dil=dil,
                          use_sink=use_sink, emit_lse=emit_lse),
        grid=(bsz, dil, nslab, nu),
        in_specs=[
            pl.BlockSpec(memory_space=pltpu.SMEM),
            pl.BlockSpec(memory_space=pltpu.SMEM),
            pl.BlockSpec((ut, LANES), lambda b, r, sl, i: (b * nu + i, r * ncb_src + qcb(sl))),
            pl.BlockSpec((u_len, LANES), lambda b, r, sl, i: (b, r * ncb_src + kcb(sl))),
            pl.BlockSpec((u_len, LANES), lambda b, r, sl, i: (b, r * ncb_src + vcb(sl))),
        ],
        out_specs=out_specs,
        out_shape=out_shapes,
        compiler_params=_cparams(("parallel", "parallel", "parallel", "arbitrary")),
        name="band_w%d_d%d" % (window, dil),
    )(slopes, sink, srcv, srcv, srcv)
    return [r.reshape(t, 512) for r in res]


def _memattn_kernel(q_ref, k_ref, v_ref, o_ref):
    for h in range(M_HEADS):
        sl = slice(h * M_HD, (h + 1) * M_HD)
        s = lax.dot_general(q_ref[:, sl], k_ref[:, sl], (((1,), (1,)), ((), ())), preferred_element_type=F32)
        mx = jnp.max(s, axis=1, keepdims=True)
        e = jnp.exp(s - mx)
        den = jnp.sum(e, axis=1, keepdims=True)
        pv = jnp.dot(e.astype(BF16), v_ref[:, sl], preferred_element_type=F32)
        o_ref[:, sl] = (pv / den).astype(BF16)


def _memattn(qm, mk, mv, bsz, s, tq):
    t = qm.shape[0]
    tq = min(tq, s)
    nq = s // tq
    wd = M_HEADS * M_HD
    return pl.pallas_call(
        _memattn_kernel,
        grid=(bsz, nq),
        in_specs=[pl.BlockSpec((tq, wd), lambda b, i: (b * nq + i, 0)),
                  pl.BlockSpec((N_MEM, wd), lambda b, i: (b, 0)),
                  pl.BlockSpec((N_MEM, wd), lambda b, i: (b, 0))],
        out_specs=pl.BlockSpec((tq, wd), lambda b, i: (b * nq + i, 0)),
        out_shape=jax.ShapeDtypeStruct((t, wd), BF16),
        compiler_params=_cparams(("parallel", "parallel")),
        name="memattn",
    )(qm, mk, mv)


def _merge_kernel(x_ref, ng_ref, wz_ref, wg_ref, bg_ref, wbr_ref, wout_ref,
                  oa_ref, ob_ref, oc_ref, om_ref, od0_ref, od1_ref, od2_ref, l0_ref, l1_ref, l2_ref,
                  y_ref, h_sc, acc_sc, o_sc):
    j = pl.program_id(1)

    @pl.when(j == 0)
    def _():
        x = x_ref[...]
        ms = jnp.mean(x * x, axis=-1, keepdims=True)
        h_sc[...] = ((x * lax.rsqrt(ms + EPS)) * ng_ref[...]).astype(BF16)
        acc_sc[...] = jnp.zeros(acc_sc.shape, F32)
        o_sc[...] = oa_ref[...].astype(F32)

    @pl.when(j == 1)
    def _():
        o_sc[...] = ob_ref[...].astype(F32)

    @pl.when(j == 2)
    def _():
        o_sc[...] = oc_ref[...].astype(F32)

    @pl.when(j == 3)
    def _():
        l0, l1, l2 = l0_ref[...], l1_ref[...], l2_ref[...]
        mx = jnp.maximum(jnp.maximum(l0, l1), l2)
        e0, e1, e2 = jnp.exp(l0 - mx), jnp.exp(l1 - mx), jnp.exp(l2 - mx)
        den = e0 + e1 + e2
        o_sc[...] = ((e0 / den) * od0_ref[...].astype(F32) + (e1 / den) * od1_ref[...].astype(F32)
                     + (e2 / den) * od2_ref[...].astype(F32))

    @pl.when(j == 4)
    def _():
        o_sc[...] = om_ref[...].astype(F32)

    h = h_sc[...]
    z = jnp.dot(h, wz_ref[...], preferred_element_type=F32)
    g = jnp.dot(h, wg_ref[...], preferred_element_type=F32) + bg_ref[0]
    u = (o_sc[...] * (z / (1.0 + jnp.exp(-z)))).astype(BF16)
    tbr = jnp.dot(u, wbr_ref[0], preferred_element_type=F32)
    acc_sc[...] += tbr / (1.0 + jnp.exp(-g))

    @pl.when(j == N_BRANCH - 1)
    def _():
        y_ref[...] = x_ref[...] + jnp.dot(acc_sc[...].astype(BF16), wout_ref[...], preferred_element_type=F32)


def _merge(x2d, p, wz, wg, oa, ob, oc, om, ods, lses, tm):
    t, d = x2d.shape
    tm = min(tm, t)
    rowf = lambda wd: pl.BlockSpec((tm, wd), lambda i, j: (i, 0))
    return pl.pallas_call(
        _merge_kernel,
        grid=(t // tm, N_BRANCH),
        in_specs=[
            rowf(d),
            _resident((1, d), lambda i, j: (0, 0)),
            pl.BlockSpec((d, BRANCH_W), lambda i, j: (0, j)),
            pl.BlockSpec((d, d), lambda i, j: (0, j)),
            pl.BlockSpec((1, 1, d), lambda i, j: (j, 0, 0)),
            pl.BlockSpec((1, BRANCH_W, d), lambda i, j: (j, 0, 0)),
            _resident((d, d), lambda i, j: (0, 0)),
        ] + [rowf(BRANCH_W)] * 10,
        out_specs=rowf(d),
        out_shape=jax.ShapeDtypeStruct((t, d), F32),
        scratch_shapes=[pltpu.VMEM((tm, d), BF16), pltpu.VMEM((tm, d), F32), pltpu.VMEM((tm, BRANCH_W), F32)],
        compiler_params=_cparams(("parallel", "arbitrary")),
        name="merge",
    )(x2d, p["norm_g"].astype(F32).reshape(1, d), wz, wg, p["b_gate"].astype(F32).reshape(N_BRANCH, 1, d),
      p["w_br"].astype(BF16), p["w_out"].astype(BF16), oa, ob, oc, om, *ods, *lses)


def _alibi_slopes(n):
    return jnp.asarray(2.0 ** (-8.0 * np.arange(1, n + 1, dtype=np.float64) / n), F32)


def _encoder_layer(x, mem, layer_idx, p):
    bsz, s, d = x.shape
    t = bsz * s
    x2d = x.reshape(t, d)

    w_in = p["w_in"]
    w_pad = jnp.concatenate([w_in, jnp.zeros((d, 1), w_in.dtype)], axis=1)
    w_main = jnp.take(w_pad, jnp.asarray(_MAIN_COLS), axis=1).astype(BF16)
    pa, pb, pc, pd, pm = _proj(x2d, p["norm_g"].astype(F32), w_main, _main_gain_vector(p),
                               _MAIN_TILES, _MAIN_WIDTHS, tm=512)

    lam_init = 0.8 - 0.6 * math.exp(-0.3 * layer_idx)
    oa = _flash(pa, pa, pa, 0, 4, 8, bsz, s, A_HEADS, 2, True, _alibi_slopes(A_HEADS),
                p["a_lam"].astype(F32), p["a_hn"].astype(F32).reshape(1, A_DV), lam_init, tq=256, tk=512)

    qb, kb, vb = _mla_prep(pb, p, bsz, s, tm=512)
    dummy_lam = jnp.zeros((4, A_DK), F32)
    dummy_hn = jnp.ones((1, A_DV), F32)
    ob = _flash(qb, kb, vb, 0, 0, 0, bsz, s, B_HEADS, 1, False, _alibi_slopes(B_HEADS),
                dummy_lam, dummy_hn, 0.0, tq=256, tk=512)

    oc, = _band(pc, lambda sl: sl, lambda sl: 4 + sl // 2, lambda sl: 6 + sl // 2, bsz, s, 1, 8,
                C_WINDOW, _alibi_slopes(C_QH), p["c_sink"].astype(F32), True, False, ut=512)

    ods, lses = [], []
    zero_sink = jnp.zeros((D_HEADS,), F32)
    for g, (win, dil) in enumerate(DIL_PAIRS):
        og, lg = _band(pd, lambda sl, g=g: g * 12 + sl, lambda sl, g=g: g * 12 + 4 + sl,
                       lambda sl, g=g: g * 12 + 8 + sl, bsz, s, dil, 36, win // (2 * dil),
                       _alibi_slopes(D_HEADS), zero_sink, False, True, ut=512)
        ods.append(og)
        lses.append(lg)

    wkv = p["w_mem_kv"].astype(BF16)
    mgain = jnp.concatenate([jnp.tile(p["m_kn"].astype(F32), M_HEADS), jnp.ones((M_HEADS * M_HD,), F32)]).reshape(1, -1)
    mtiles = [(0, SEG128, 0, 0), (256, SEG128, 0, 256), (512, SEG_NONE, 1, 0), (768, SEG_NONE, 1, 256)]
    mk, mv = _proj(mem.reshape(bsz * N_MEM, d), p["m_norm"].astype(F32), wkv, mgain, mtiles, [512, 512], tm=256)
    om = _memattn(pm, mk, mv, bsz, s, tq=512)

    wz = w_in[:, OFF_Z:OFF_G].astype(BF16)
    wg = w_in[:, OFF_G:].astype(BF16)
    y = _merge(x2d, p, wz, wg, oa, ob, oc, om, ods, lses, tm=512)
    return y.reshape(bsz, s, d)


def kernel(x_prompt, x_sample, mem_prompt, mem_sample, norm_g, w_in, a_qn, a_kn, a_lam, a_hn, b_cqn, b_ckvn, w_qb, w_kvb, b_qn, b_kn, c_qn, c_kn, c_sink, d_qn, d_kn, m_norm, w_mem_kv, m_qn, m_kn, b_gate, w_br, w_out):
    depth = norm_g.shape[0]
    y_prompt, y_sample = x_prompt, x_sample
    for l in range(depth):
        p = {
            "norm_g": norm_g[l], "w_in": w_in[l],
            "a_qn": a_qn[l], "a_kn": a_kn[l], "a_lam": a_lam[l], "a_hn": a_hn[l],
            "b_cqn": b_cqn[l], "b_ckvn": b_ckvn[l], "w_qb": w_qb[l], "w_kvb": w_kvb[l],
            "b_qn": b_qn[l], "b_kn": b_kn[l],
            "c_qn": c_qn[l], "c_kn": c_kn[l], "c_sink": c_sink[l],
            "d_qn": d_qn[l], "d_kn": d_kn[l],
            "m_norm": m_norm[l], "w_mem_kv": w_mem_kv[l], "m_qn": m_qn[l], "m_kn": m_kn[l],
            "b_gate": b_gate[l], "w_br": w_br[l], "w_out": w_out[l],
        }
        y_prompt = _encoder_layer(y_prompt, mem_prompt, l, p)
        y_sample = _encoder_layer(y_sample, mem_sample, l, p)
    return (y_prompt, y_sample)
```

```python
import functools
import math

import numpy as np
import jax
import jax.numpy as jnp
from jax import lax
from jax.experimental import pallas as pl
from jax.experimental.pallas import tpu as pltpu

F32 = jnp.float32
BF16 = jnp.bfloat16

D_MODEL = 1024
N_MEM = 256
BRANCH_W = 512
N_BRANCH = 5
NEG_INF = -1e30
EPS = 1e-6
A_HEADS, A_DK, A_DV = 4, 64, 128
B_HEADS, B_Q_LORA, B_KV_LORA, B_NOPE, B_ROPE, B_DV = 4, 256, 128, 64, 32, 128
ROPE_THETA = 10000.0
C_QH, C_KVH, C_HD, C_WINDOW = 8, 2, 64, 128
D_HEADS, D_HD = 8, 64
DIL_PAIRS = ((128, 1), (512, 4), (2048, 16))
N_DIL = 3
M_HEADS, M_HD = 4, 128

OFF_A_Q = 0
OFF_A_K = OFF_A_Q + 2 * A_HEADS * A_DK
OFF_A_V = OFF_A_K + 2 * A_HEADS * A_DK
OFF_B_CQ = OFF_A_V + A_HEADS * A_DV
OFF_B_CKV = OFF_B_CQ + B_Q_LORA
OFF_B_KR = OFF_B_CKV + B_KV_LORA
OFF_C_Q = OFF_B_KR + B_ROPE
OFF_C_K = OFF_C_Q + C_QH * C_HD
OFF_C_V = OFF_C_K + C_KVH * C_HD
OFF_D_Q = OFF_C_V + C_KVH * C_HD
OFF_D_K = OFF_D_Q + N_DIL * D_HEADS * D_HD
OFF_D_V = OFF_D_K + N_DIL * D_HEADS * D_HD
OFF_M_Q = OFF_D_V + N_DIL * D_HEADS * D_HD
OFF_Z = OFF_M_Q + M_HEADS * M_HD
OFF_G = OFF_Z + N_BRANCH * BRANCH_W
N_IN = OFF_G + N_BRANCH * D_MODEL

LANES = 128
COL_TILE = 256
VMEM_LIMIT = 56 * 1024 * 1024

SEG_NONE, SEG64, SEG128 = -1, 0, 1

LOG2E = 1.4426950408889634
MAX_FIXED_SHIFT = 30.0 * LOG2E
BOUND_MARGIN = 1.02


def _cparams(sem):
    return pltpu.CompilerParams(dimension_semantics=sem, vmem_limit_bytes=VMEM_LIMIT)


def _resident(shape, index_map):
    return pl.BlockSpec(shape, index_map, pipeline_mode=pl.Buffered(1))


def _seg_matrix(width, segs):
    m = np.zeros((width, width), np.float32)
    for s, n in segs:
        m[s:s + n, s:s + n] = 1.0 / n
    return m


def _proj_kernel(x_ref, ng_ref, w_ref, gain_ref, mseg_ref, *out_refs, tiles):
    x = x_ref[...]
    ms = jnp.mean(x * x, axis=-1, keepdims=True)
    h = ((x * lax.rsqrt(ms + EPS)) * ng_ref[...]).astype(BF16)
    for c0, seg, oi, oc in tiles:
        y = jnp.dot(h, w_ref[:, c0:c0 + COL_TILE], preferred_element_type=F32)
        if seg != SEG_NONE:
            sq = (y * y).astype(BF16)
            segms = jnp.dot(sq, mseg_ref[seg], preferred_element_type=F32)
            y = (y * lax.rsqrt(segms + EPS)) * gain_ref[:, c0:c0 + COL_TILE]
        out_refs[oi][:, oc:oc + COL_TILE] = y.astype(BF16)


def _proj(x2d, norm_gain, w, gain, tiles, out_widths, tm):
    t, d = x2d.shape
    n = w.shape[1]
    tm = min(tm, t)
    mseg = jnp.asarray(
        np.stack([_seg_matrix(COL_TILE, [(s, 64) for s in range(0, COL_TILE, 64)]),
                  _seg_matrix(COL_TILE, [(s, 128) for s in range(0, COL_TILE, 128)])]), BF16)
    return pl.pallas_call(
        functools.partial(_proj_kernel, tiles=tuple(tiles)),
        grid=(t // tm,),
        in_specs=[
            pl.BlockSpec((tm, d), lambda i: (i, 0)),
            _resident((1, d), lambda i: (0, 0)),
            _resident((d, n), lambda i: (0, 0)),
            _resident((1, n), lambda i: (0, 0)),
            _resident((2, COL_TILE, COL_TILE), lambda i: (0, 0, 0)),
        ],
        out_specs=[pl.BlockSpec((tm, wd), lambda i: (i, 0)) for wd in out_widths],
        out_shape=[jax.ShapeDtypeStruct((t, wd), BF16) for wd in out_widths],
        compiler_params=_cparams(("parallel",)),
        name="proj",
    )(x2d, norm_gain.reshape(1, d), w, gain, mseg)


def _main_plan():
    zero = N_IN
    cols, gains, tiles = [], [], []
    widths = [3 * 512, 512, 1024] + [3 * 512] * N_DIL + [512]

    def add(out_idx, out_col, src_cols, seg, gain_key):
        assert len(src_cols) % COL_TILE == 0
        c0 = len(cols)
        cols.extend(src_cols)
        gains.extend([gain_key] * len(src_cols))
        for k in range(len(src_cols) // COL_TILE):
            tiles.append((c0 + k * COL_TILE, seg, out_idx, out_col + k * COL_TILE))

    hm = [m * A_HEADS * A_DK + h * A_DK + d for h in range(A_HEADS) for m in range(2) for d in range(A_DK)]
    add(0, 0, [OFF_A_Q + c for c in hm], SEG64, "a_q")
    add(0, 512, [OFF_A_K + c for c in hm], SEG64, "a_k")
    add(0, 1024, [OFF_A_V + c for c in range(A_HEADS * A_DV)], SEG_NONE, None)
    braw = ([OFF_B_CQ + c for c in range(B_Q_LORA)] + [OFF_B_CKV + c for c in range(B_KV_LORA)]
            + [OFF_B_KR + c for c in range(B_ROPE)] + [zero] * (512 - B_Q_LORA - B_KV_LORA - B_ROPE))
    add(1, 0, braw, SEG_NONE, None)
    add(2, 0, [OFF_C_Q + c for c in range(C_QH * C_HD)], SEG64, "c_q")
    dup = [kv * C_HD + d for kv in range(C_KVH) for _ in range(2) for d in range(C_HD)]
    add(2, 512, [OFF_C_K + c for c in dup], SEG64, "c_k")
    add(2, 768, [OFF_C_V + c for c in dup], SEG_NONE, None)
    for g in range(N_DIL):
        gsl = [g * D_HEADS * D_HD + c for c in range(D_HEADS * D_HD)]
        add(3 + g, 0, [OFF_D_Q + c for c in gsl], SEG64, "d_q")
        add(3 + g, 512, [OFF_D_K + c for c in gsl], SEG64, "d_k")
        add(3 + g, 1024, [OFF_D_V + c for c in gsl], SEG_NONE, None)
    add(3 + N_DIL, 0, [OFF_M_Q + c for c in range(M_HEADS * M_HD)], SEG128, "m_q")
    return np.asarray(cols, np.int32), gains, tiles, widths


_MAIN_COLS, _MAIN_GAINS, _MAIN_TILES, _MAIN_WIDTHS = _main_plan()


def _main_gain_vector(p):
    per_key = {
        "a_q": p["a_qn"] * (A_DK ** -0.5 * LOG2E), "a_k": p["a_kn"],
        "c_q": p["c_qn"] * (C_HD ** -0.5 * LOG2E), "c_k": p["c_kn"],
        "d_q": p["d_qn"] * (D_HD ** -0.5 * LOG2E), "d_k": p["d_kn"],
        "m_q": p["m_qn"] * (M_HD ** -0.5),
    }
    n = len(_MAIN_GAINS)
    pieces, k = [], 0
    while k < n:
        key = _MAIN_GAINS[k]
        k2 = k
        while k2 < n and _MAIN_GAINS[k2] == key:
            k2 += 1
        if key is None:
            pieces.append(jnp.ones((k2 - k,), F32))
        else:
            gvec = per_key[key].astype(F32)
            pieces.append(jnp.tile(gvec, (k2 - k) // gvec.shape[0]))
        k = k2
    return jnp.concatenate(pieces).reshape(1, n)


def _mla_prep_kernel(b_ref, wq_ref, wk_ref, wv_ref, place_ref, mseg_ref, cqg_ref, ckvg_ref,
                     qg_ref, kg_ref, ct_ref, s1_ref, s2_ref, q_out, k_out, v_out):
    braw = b_ref[...]
    cq = braw[:, :B_Q_LORA].astype(F32)
    cqn = ((cq * lax.rsqrt(jnp.mean(cq * cq, axis=-1, keepdims=True) + EPS)) * cqg_ref[...]).astype(BF16)
    ckv = braw[:, B_Q_LORA:B_Q_LORA + B_KV_LORA].astype(F32)
    ckvn = ((ckv * lax.rsqrt(jnp.mean(ckv * ckv, axis=-1, keepdims=True) + EPS)) * ckvg_ref[...]).astype(BF16)
    qb = jnp.dot(cqn, wq_ref[...], preferred_element_type=F32)
    kb = jnp.dot(ckvn, wk_ref[...], preferred_element_type=F32)
    kb = kb + jnp.dot(braw[:, B_Q_LORA + B_KV_LORA:], place_ref[...], preferred_element_type=F32)
    v_out[...] = jnp.dot(ckvn, wv_ref[...], preferred_element_type=F32).astype(BF16)
    ct, s1, s2 = ct_ref[...], s1_ref[...], s2_ref[...]

    def finish(t, gain_ref, out):
        for h in range(B_HEADS):
            y = t[:, h * LANES:(h + 1) * LANES]
            segms = jnp.dot((y * y).astype(BF16), mseg_ref[...], preferred_element_type=F32)
            yn = (y * lax.rsqrt(segms + EPS)) * gain_ref[...]
            half = B_ROPE // 2
            r = yn * ct + pltpu.roll(yn, half, 1) * s1 + pltpu.roll(yn, LANES - half, 1) * s2
            out[:, h * LANES:(h + 1) * LANES] = r.astype(BF16)

    finish(qb, qg_ref, q_out)
    finish(kb, kg_ref, k_out)


def _rope_tables(s):
    half = B_ROPE // 2
    inv = ROPE_THETA ** (-jnp.arange(half, dtype=F32) / half)
    ang = jnp.arange(s, dtype=F32)[:, None] * inv[None, :]
    cos, sin = jnp.cos(ang), jnp.sin(ang)
    one = jnp.ones((s, B_NOPE), F32)
    zpad = jnp.zeros((s, LANES - B_NOPE - B_ROPE), F32)
    z64 = jnp.zeros((s, B_NOPE), F32)
    zh = jnp.zeros((s, half), F32)
    ct = jnp.concatenate([one, cos, cos, zpad], axis=1)
    s1 = jnp.concatenate([z64, zh, sin, zpad], axis=1)
    s2 = jnp.concatenate([z64, -sin, zh, zpad], axis=1)
    return ct, s1, s2


def _mla_prep(braw, p, bsz, s, tm):
    t = braw.shape[0]
    tm = min(tm, s)
    pad = LANES - B_NOPE - B_ROPE
    wq = p["w_qb"].reshape(B_Q_LORA, B_HEADS, B_NOPE + B_ROPE)
    wq = jnp.pad(wq, ((0, 0), (0, 0), (0, pad))).reshape(B_Q_LORA, B_HEADS * LANES).astype(BF16)
    wkv = p["w_kvb"].reshape(B_KV_LORA, B_HEADS, B_NOPE + B_DV)
    wk = jnp.pad(wkv[:, :, :B_NOPE], ((0, 0), (0, 0), (0, LANES - B_NOPE))).reshape(B_KV_LORA, B_HEADS * LANES).astype(BF16)
    wv = wkv[:, :, B_NOPE:].reshape(B_KV_LORA, B_HEADS * B_DV).astype(BF16)
    place = np.zeros((LANES, B_HEADS * LANES), np.float32)
    for h in range(B_HEADS):
        for d in range(B_ROPE):
            place[d, h * LANES + B_NOPE + d] = 1.0
    mseg = _seg_matrix(LANES, [(0, B_NOPE), (B_NOPE, B_ROPE)])
    scale = (B_NOPE + B_ROPE) ** -0.5 * LOG2E
    zp = jnp.zeros((pad,), F32)
    qg = jnp.concatenate([p["b_qn"].astype(F32) * scale, zp]).reshape(1, LANES)
    kg = jnp.concatenate([p["b_kn"].astype(F32), zp]).reshape(1, LANES)
    ct, s1, s2 = _rope_tables(s)
    nst = s // tm
    full = lambda shape: _resident(shape, lambda b, i: tuple(0 for _ in shape))
    row = lambda wd: pl.BlockSpec((tm, wd), lambda b, i: (b * nst + i, 0))
    tab = pl.BlockSpec((tm, LANES), lambda b, i: (i, 0))
    return pl.pallas_call(
        _mla_prep_kernel,
        grid=(bsz, nst),
        in_specs=[row(512), full((B_Q_LORA, 512)), full((B_KV_LORA, 512)), full((B_KV_LORA, 512)),
                  full((LANES, 512)), full((LANES, LANES)), full((1, B_Q_LORA)), full((1, B_KV_LORA)),
                  full((1, LANES)), full((1, LANES)), tab, tab, tab],
        out_specs=[row(512), row(512), row(512)],
        out_shape=[jax.ShapeDtypeStruct((t, 512), BF16)] * 3,
        compiler_params=_cparams(("parallel", "parallel")),
        name="mla_prep",
    )(braw, wq, wk, wv, jnp.asarray(place, BF16), jnp.asarray(mseg, BF16),
      p["b_cqn"].astype(F32).reshape(1, -1), p["b_ckvn"].astype(F32).reshape(1, -1), qg, kg, ct, s1, s2)


def _flash_kernel(slopes_ref, lam_ref, hn_ref, q_ref, k_ref, v_ref, o_ref, m_sc, l_sc, acc_sc,
                  *, nmaps, alibi, tq, tk, seq, lam_init):
    h = pl.program_id(1)
    q0 = pl.program_id(2) * tq
    q = q_ref[...]
    if nmaps == 2:
        lane = lax.broadcasted_iota(jnp.int32, q.shape, 1)
        qs = [jnp.where(lane < A_DK, q, jnp.zeros_like(q)), jnp.where(lane >= A_DK, q, jnp.zeros_like(q))]
    else:
        qs = [q]
    m_sc[...] = jnp.full(m_sc.shape, NEG_INF, F32)
    l_sc[...] = jnp.zeros(l_sc.shape, F32)
    acc_sc[...] = jnp.zeros(acc_sc.shape, F32)
    if alibi:
        slope2 = slopes_ref[h] * LOG2E
        dmat = (lax.broadcasted_iota(jnp.int32, (tq, tk), 0)
                - lax.broadcasted_iota(jnp.int32, (tq, tk), 1)).astype(F32)

    def body(j, carry):
        k0 = pl.multiple_of(j * tk, tk)
        k = k_ref[pl.ds(k0, tk), :]
        v = v_ref[pl.ds(k0, tk), :]
        if alibi:
            bias = -slope2 * jnp.abs(dmat + (q0 - k0).astype(F32))
        for m in range(nmaps):
            s = lax.dot_general(qs[m], k, (((1,), (1,)), ((), ())), preferred_element_type=F32)
            if alibi:
                s = s + bias
            m_old = m_sc[m]
            m_new = jnp.maximum(m_old, jnp.max(s, axis=1, keepdims=True))
            alpha = jnp.exp2(m_old - m_new)
            pr = jnp.exp2(s - m_new)
            l_sc[m] = alpha * l_sc[m] + jnp.sum(pr, axis=1, keepdims=True)
            acc_sc[m] = alpha * acc_sc[m] + jnp.dot(pr.astype(BF16), v, preferred_element_type=F32)
            m_sc[m] = m_new
        return carry

    lax.fori_loop(0, seq // tk, body, 0)
    outs = [acc_sc[m] / l_sc[m] for m in range(nmaps)]
    o_ref[...] = _flash_epilogue(outs, lam_ref, hn_ref, lam_init).astype(BF16)


def _flash_epilogue(outs, lam_ref, hn_ref, lam_init):
    if len(outs) == 1:
        return outs[0]
    lf = lam_ref[...]
    e1 = jnp.exp(jnp.sum(lf[0:1] * lf[1:2], axis=1, keepdims=True))
    e2 = jnp.exp(jnp.sum(lf[2:3] * lf[3:4], axis=1, keepdims=True))
    lam = e1 - e2 + lam_init
    o = outs[0] - lam * outs[1]
    o = (o * lax.rsqrt(jnp.mean(o * o, axis=-1, keepdims=True) + EPS)) * hn_ref[...]
    return o * (1.0 - lam_init)


def _flash_fixed_kernel(shift_ref, slopes_ref, lam_ref, hn_ref, q_ref, k_ref, v_ref, o_ref, v1_sc, acc_sc,
                        *, nmaps, alibi, tq, tk, seq, lam_init, unroll):
    h = pl.program_id(1)
    i = pl.program_id(2)
    q0 = i * tq

    @pl.when(i == 0)
    def _():
        v1_sc[:, :LANES] = v_ref[...]
        v1_sc[:, LANES:] = jnp.ones((seq, LANES), BF16)

    q = q_ref[...]
    if nmaps == 2:
        lane = lax.broadcasted_iota(jnp.int32, q.shape, 1)
        qs = [jnp.where(lane < A_DK, q, jnp.zeros_like(q)), jnp.where(lane >= A_DK, q, jnp.zeros_like(q))]
    else:
        qs = [q]
    acc_sc[...] = jnp.zeros(acc_sc.shape, F32)
    shift = shift_ref[0]
    if alibi:
        slope2 = slopes_ref[h] * LOG2E
        t0 = (lax.broadcasted_iota(jnp.int32, (tq, tk), 0)
              - lax.broadcasted_iota(jnp.int32, (tq, tk), 1)).astype(F32) * slope2

    def body(j, carry):
        k0 = pl.multiple_of(j * tk, tk)
        k = k_ref[pl.ds(k0, tk), :]
        v1 = v1_sc[pl.ds(k0, tk), :]
        if alibi:
            sub = jnp.abs(t0 + slope2 * (q0 - k0).astype(F32)) + shift
        ps = []
        for m in range(nmaps):
            s = lax.dot_general(qs[m], k, (((1,), (1,)), ((), ())), preferred_element_type=F32)
            ps.append(jnp.exp2(s - sub if alibi else s - shift).astype(BF16))
        pr = ps[0] if nmaps == 1 else jnp.concatenate(ps, axis=0)
        acc_sc[...] += jnp.dot(pr, v1, preferred_element_type=F32)
        return carry

    lax.fori_loop(0, seq // tk, body, 0, unroll=unroll)
    outs = [acc_sc[m * tq:(m + 1) * tq, :LANES] / acc_sc[m * tq:(m + 1) * tq, LANES:] for m in range(nmaps)]
    o_ref[...] = _flash_epilogue(outs, lam_ref, hn_ref, lam_init).astype(BF16)


def _flash(qa, ka, va, qcb, kcb, vcb, bsz, s, heads, nmaps, alibi, slopes, lam, hn, lam_init, bound2):
    t = qa.shape[0]
    name = "flash_diff" if nmaps == 2 else "flash_mla"
    shift = bound2.reshape(1).astype(F32)

    def call(kern, tq, tk, scratch, extra_in, extra_args, suffix):
        tq_, tk_ = min(tq, s), min(tk, s)
        nq = s // tq_
        return pl.pallas_call(
            functools.partial(kern, nmaps=nmaps, alibi=alibi, tq=tq_, tk=tk_, seq=s, lam_init=lam_init),
            grid=(bsz, heads, nq),
            in_specs=extra_in + [
                pl.BlockSpec(memory_space=pltpu.SMEM),
                _resident(lam.shape, lambda b, h, i: (0, 0)),
                _resident(hn.shape, lambda b, h, i: (0, 0)),
                pl.BlockSpec((tq_, LANES), lambda b, h, i: (b * nq + i, qcb + h)),
                pl.BlockSpec((s, LANES), lambda b, h, i: (b, kcb + h)),
                pl.BlockSpec((s, LANES), lambda b, h, i: (b, vcb + h)),
            ],
            out_specs=pl.BlockSpec((tq_, LANES), lambda b, h, i: (b * nq + i, h)),
            out_shape=jax.ShapeDtypeStruct((t, heads * LANES), BF16),
            scratch_shapes=scratch(tq_),
            compiler_params=_cparams(("parallel", "parallel", "arbitrary")),
            name=name + suffix,
        )(*extra_args, slopes, lam, hn, qa, ka, va)

    def fixed():
        return call(functools.partial(_flash_fixed_kernel, unroll=True), 512, 512,
                    lambda tq_: [pltpu.VMEM((s, 2 * LANES), BF16), pltpu.VMEM((nmaps * tq_, 2 * LANES), F32)],
                    [pl.BlockSpec(memory_space=pltpu.SMEM)], [shift], "_fixed")

    def online():
        return call(_flash_kernel, 256, 512,
                    lambda tq_: [pltpu.VMEM((nmaps, tq_, 1), F32), pltpu.VMEM((nmaps, tq_, 1), F32),
                                 pltpu.VMEM((nmaps, tq_, LANES), F32)],
                    [], [], "_online")

    return lax.cond(bound2 <= MAX_FIXED_SHIFT, fixed, online)


def _band_kernel(slopes_ref, sink_ref, q_ref, k_ref, v_ref, *outs, window, qb, kw, ut, u_len, dil, use_sink, emit_lse):
    o_ref = outs[0]
    slab = pl.program_id(2)
    ubase = pl.program_id(3) * ut
    lane = lax.broadcasted_iota(jnp.int32, (kw, LANES), 1)
    qlane = lax.broadcasted_iota(jnp.int32, (qb, LANES), 1)
    dmat = (lax.broadcasted_iota(jnp.int32, (qb, kw), 0) - lax.broadcasted_iota(jnp.int32, (qb, kw), 1))
    for sb in range(ut // qb):
        u0 = ubase + sb * qb
        start = pl.multiple_of(jnp.clip(u0 - window, 0, u_len - kw), 16)
        q = q_ref[sb * qb:(sb + 1) * qb, :]
        k = k_ref[pl.ds(start, kw), :]
        v = v_ref[pl.ds(start, kw), :]
        rel = dmat + (u0 - start)
        valid = jnp.abs(rel) <= window
        dist = jnp.abs(rel).astype(F32) * float(dil)
        o_acc = jnp.zeros((qb, LANES), F32)
        lse_acc = jnp.zeros((qb, LANES), F32)
        for hh in range(2):
            head = slab * 2 + hh
            sel_q = (qlane < 64) if hh == 0 else (qlane >= 64)
            sel_v = (lane < 64) if hh == 0 else (lane >= 64)
            qh = jnp.where(sel_q, q, jnp.zeros_like(q))
            vh = jnp.where(sel_v, v, jnp.zeros_like(v))
            s = lax.dot_general(qh, k, (((1,), (1,)), ((), ())), preferred_element_type=F32)
            logits = jnp.where(valid, s - (slopes_ref[head] * LOG2E) * dist, NEG_INF)
            mx = jnp.max(logits, axis=1, keepdims=True)
            if use_sink:
                sk = sink_ref[head]
                mx = jnp.maximum(mx, sk)
            e = jnp.exp2(logits - mx)
            den = jnp.sum(e, axis=1, keepdims=True)
            if use_sink:
                den = den + jnp.exp2(sk - mx)
            pv = jnp.dot(e.astype(BF16), vh, preferred_element_type=F32)
            o_acc = o_acc + pv / den
            if emit_lse:
                lse_acc = lse_acc + jnp.where(sel_q, mx + jnp.log2(den), 0.0)
        o_ref[sb * qb:(sb + 1) * qb, :] = o_acc.astype(BF16)
        if emit_lse:
            outs[1][sb * qb:(sb + 1) * qb, :] = lse_acc


def _band_fixed_kernel(shift_ref, slopes_ref, sink_ref, q_ref, k_ref, v_ref, *outs,
                       window, qb, kw, ut, u_len, dil, use_sink, emit_lse):
    o_ref = outs[0]
    slab = pl.program_id(2)
    ubase = pl.program_id(3) * ut
    shift = shift_ref[0]
    qlane = lax.broadcasted_iota(jnp.int32, (qb, LANES), 1)
    first = qlane < 64
    dmat = (lax.broadcasted_iota(jnp.int32, (qb, kw), 0) - lax.broadcasted_iota(jnp.int32, (qb, kw), 1))
    ones = jnp.ones((kw, LANES), BF16)
    for sb in range(ut // qb):
        u0 = ubase + sb * qb
        start = pl.multiple_of(jnp.clip(u0 - window, 0, u_len - kw), 16)
        q = q_ref[sb * qb:(sb + 1) * qb, :]
        k = k_ref[pl.ds(start, kw), :]
        v1 = jnp.concatenate([v_ref[pl.ds(start, kw), :], ones], axis=1)
        absrel = jnp.abs(dmat + (u0 - start))
        valid = absrel <= window
        dist = absrel.astype(F32)
        ps = []
        for hh in range(2):
            head = slab * 2 + hh
            slope2 = slopes_ref[head] * (LOG2E * dil)
            sub = jnp.where(valid, dist * slope2 + shift, -NEG_INF)
            qh = jnp.where(first if hh == 0 else jnp.logical_not(first), q, jnp.zeros_like(q))
            s = lax.dot_general(qh, k, (((1,), (1,)), ((), ())), preferred_element_type=F32)
            ps.append(jnp.exp2(s - sub).astype(BF16))
        r = jnp.dot(jnp.concatenate(ps, axis=0), v1, preferred_element_type=F32)
        za, zb = r[:qb, LANES:], r[qb:, LANES:]
        if use_sink:
            za = za + jnp.exp2(jnp.full((1, 1), sink_ref[slab * 2] - shift, F32))
            zb = zb + jnp.exp2(jnp.full((1, 1), sink_ref[slab * 2 + 1] - shift, F32))
        o_ref[sb * qb:(sb + 1) * qb, :] = jnp.where(first, r[:qb, :LANES] / za, r[qb:, :LANES] / zb).astype(BF16)
        if emit_lse:
            outs[1][sb * qb:(sb + 1) * qb, :] = jnp.where(first, jnp.log2(za), jnp.log2(zb)) + shift


def _band(src, qcb, kcb, vcb, bsz, s, dil, ncb_src, window, slopes, sink, use_sink, emit_lse, ut, bound2):
    t, c = src.shape
    u_len = s // dil
    qb = min(LANES, u_len)
    kw = min(qb + 2 * window, u_len)
    ut = min(ut, u_len)
    nu = u_len // ut
    srcv = src.reshape(bsz * u_len, dil * c)
    nslab = 4
    out_shapes = [jax.ShapeDtypeStruct((bsz * u_len, dil * 512), BF16)]
    out_specs = [pl.BlockSpec((ut, LANES), lambda b, r, sl, i: (b * nu + i, r * nslab + sl))]
    if emit_lse:
        out_shapes.append(jax.ShapeDtypeStruct((bsz * u_len, dil * 512), F32))
        out_specs.append(pl.BlockSpec((ut, LANES), lambda b, r, sl, i: (b * nu + i, r * nslab + sl)))
    smem = pl.BlockSpec(memory_space=pltpu.SMEM)

    def call(kern, extra_in, extra_args, suffix):
        return pl.pallas_call(
            functools.partial(kern, window=window, qb=qb, kw=kw, ut=ut, u_len=u_len, dil=dil,
                              use_sink=use_sink, emit_lse=emit_lse),
            grid=(bsz, dil, nslab, nu),
            in_specs=extra_in + [
                smem, smem,
                pl.BlockSpec((ut, LANES), lambda b, r, sl, i: (b * nu + i, r * ncb_src + qcb(sl))),
                pl.BlockSpec((u_len, LANES), lambda b, r, sl, i: (b, r * ncb_src + kcb(sl))),
                pl.BlockSpec((u_len, LANES), lambda b, r, sl, i: (b, r * ncb_src + vcb(sl))),
            ],
            out_specs=out_specs,
            out_shape=out_shapes,
            compiler_params=_cparams(("parallel", "parallel", "parallel", "arbitrary")),
            name="band_w%d_d%d%s" % (window, dil, suffix),
        )(*extra_args, slopes, sink, srcv, srcv, srcv)

    res = lax.cond(bound2 <= MAX_FIXED_SHIFT,
                   lambda: call(_band_fixed_kernel, [smem], [bound2.reshape(1).astype(F32)], "_fixed"),
                   lambda: call(_band_kernel, [], [], "_online"))
    return [r.reshape(t, 512) for r in res]


def _memattn_kernel(q_ref, k_ref, v_ref, o_ref):
    for h in range(M_HEADS):
        sl = slice(h * M_HD, (h + 1) * M_HD)
        s = lax.dot_general(q_ref[:, sl], k_ref[:, sl], (((1,), (1,)), ((), ())), preferred_element_type=F32)
        mx = jnp.max(s, axis=1, keepdims=True)
        e = jnp.exp(s - mx)
        den = jnp.sum(e, axis=1, keepdims=True)
        pv = jnp.dot(e.astype(BF16), v_ref[:, sl], preferred_element_type=F32)
        o_ref[:, sl] = (pv / den).astype(BF16)


def _memattn(qm, mk, mv, bsz, s, tq):
    t = qm.shape[0]
    tq = min(tq, s)
    nq = s // tq
    wd = M_HEADS * M_HD
    return pl.pallas_call(
        _memattn_kernel,
        grid=(bsz, nq),
        in_specs=[pl.BlockSpec((tq, wd), lambda b, i: (b * nq + i, 0)),
                  pl.BlockSpec((N_MEM, wd), lambda b, i: (b, 0)),
                  pl.BlockSpec((N_MEM, wd), lambda b, i: (b, 0))],
        out_specs=pl.BlockSpec((tq, wd), lambda b, i: (b * nq + i, 0)),
        out_shape=jax.ShapeDtypeStruct((t, wd), BF16),
        compiler_params=_cparams(("parallel", "parallel")),
        name="memattn",
    )(qm, mk, mv)


def _merge_kernel(x_ref, ng_ref, wz_ref, wg_ref, bg_ref, wbr_ref, wout_ref,
                  oa_ref, ob_ref, oc_ref, om_ref, od0_ref, od1_ref, od2_ref, l0_ref, l1_ref, l2_ref,
                  y_ref, h_sc, acc_sc, o_sc):
    j = pl.program_id(1)

    @pl.when(j == 0)
    def _():
        x = x_ref[...]
        ms = jnp.mean(x * x, axis=-1, keepdims=True)
        h_sc[...] = ((x * lax.rsqrt(ms + EPS)) * ng_ref[...]).astype(BF16)
        acc_sc[...] = jnp.zeros(acc_sc.shape, F32)
        o_sc[...] = oa_ref[...].astype(F32)

    @pl.when(j == 1)
    def _():
        o_sc[...] = ob_ref[...].astype(F32)

    @pl.when(j == 2)
    def _():
        o_sc[...] = oc_ref[...].astype(F32)

    @pl.when(j == 3)
    def _():
        l0, l1, l2 = l0_ref[...], l1_ref[...], l2_ref[...]
        mx = jnp.maximum(jnp.maximum(l0, l1), l2)
        e0, e1, e2 = jnp.exp2(l0 - mx), jnp.exp2(l1 - mx), jnp.exp2(l2 - mx)
        den = e0 + e1 + e2
        o_sc[...] = ((e0 / den) * od0_ref[...].astype(F32) + (e1 / den) * od1_ref[...].astype(F32)
                     + (e2 / den) * od2_ref[...].astype(F32))

    @pl.when(j == 4)
    def _():
        o_sc[...] = om_ref[...].astype(F32)

    h = h_sc[...]
    z = jnp.dot(h, wz_ref[...], preferred_element_type=F32)
    g = jnp.dot(h, wg_ref[...], preferred_element_type=F32) + bg_ref[0]
    u = (o_sc[...] * (z / (1.0 + jnp.exp(-z)))).astype(BF16)
    tbr = jnp.dot(u, wbr_ref[0], preferred_element_type=F32)
    acc_sc[...] += tbr / (1.0 + jnp.exp(-g))

    @pl.when(j == N_BRANCH - 1)
    def _():
        y_ref[...] = x_ref[...] + jnp.dot(acc_sc[...].astype(BF16), wout_ref[...], preferred_element_type=F32)


def _merge(x2d, p, wz, wg, oa, ob, oc, om, ods, lses, tm):
    t, d = x2d.shape
    tm = min(tm, t)
    rowf = lambda wd: pl.BlockSpec((tm, wd), lambda i, j: (i, 0))
    return pl.pallas_call(
        _merge_kernel,
        grid=(t // tm, N_BRANCH),
        in_specs=[
            rowf(d),
            _resident((1, d), lambda i, j: (0, 0)),
            pl.BlockSpec((d, BRANCH_W), lambda i, j: (0, j)),
            pl.BlockSpec((d, d), lambda i, j: (0, j)),
            pl.BlockSpec((1, 1, d), lambda i, j: (j, 0, 0)),
            pl.BlockSpec((1, BRANCH_W, d), lambda i, j: (j, 0, 0)),
            _resident((d, d), lambda i, j: (0, 0)),
        ] + [rowf(BRANCH_W)] * 10,
        out_specs=rowf(d),
        out_shape=jax.ShapeDtypeStruct((t, d), F32),
        scratch_shapes=[pltpu.VMEM((tm, d), BF16), pltpu.VMEM((tm, d), F32), pltpu.VMEM((tm, BRANCH_W), F32)],
        compiler_params=_cparams(("parallel", "arbitrary")),
        name="merge",
    )(x2d, p["norm_g"].astype(F32).reshape(1, d), wz, wg, p["b_gate"].astype(F32).reshape(N_BRANCH, 1, d),
      p["w_br"].astype(BF16), p["w_out"].astype(BF16), oa, ob, oc, om, *ods, *lses)


def _alibi_slopes(n):
    return jnp.asarray(2.0 ** (-8.0 * np.arange(1, n + 1, dtype=np.float64) / n), F32)


def _encoder_layer(x, mem, layer_idx, p):
    bsz, s, d = x.shape
    t = bsz * s
    x2d = x.reshape(t, d)

    w_in = p["w_in"]
    w_pad = jnp.concatenate([w_in, jnp.zeros((d, 1), w_in.dtype)], axis=1)
    w_main = jnp.take(w_pad, jnp.asarray(_MAIN_COLS), axis=1).astype(BF16)
    pa, pb, pc, pd0, pd1, pd2, pm = _proj(x2d, p["norm_g"].astype(F32), w_main, _main_gain_vector(p),
                                          _MAIN_TILES, _MAIN_WIDTHS, tm=512)

    lam_init = 0.8 - 0.6 * math.exp(-0.3 * layer_idx)
    bound_a = (BOUND_MARGIN * A_DK * (A_DK ** -0.5 * LOG2E)
               * jnp.max(jnp.abs(p["a_qn"].astype(F32))) * jnp.max(jnp.abs(p["a_kn"].astype(F32))))
    oa = _flash(pa, pa, pa, 0, 4, 8, bsz, s, A_HEADS, 2, True, _alibi_slopes(A_HEADS),
                p["a_lam"].astype(F32), p["a_hn"].astype(F32).reshape(1, A_DV), lam_init, bound_a)

    qb, kb, vb = _mla_prep(pb, p, bsz, s, tm=512)
    dummy_lam = jnp.zeros((4, A_DK), F32)
    dummy_hn = jnp.ones((1, A_DV), F32)
    dqk = B_NOPE + B_ROPE
    bound_b = (BOUND_MARGIN * dqk * (dqk ** -0.5 * LOG2E)
               * jnp.max(jnp.abs(p["b_qn"].astype(F32))) * jnp.max(jnp.abs(p["b_kn"].astype(F32))))
    ob = _flash(qb, kb, vb, 0, 0, 0, bsz, s, B_HEADS, 1, False, _alibi_slopes(B_HEADS),
                dummy_lam, dummy_hn, 0.0, bound_b)

    sink2 = p["c_sink"].astype(F32) * LOG2E
    bound_c = jnp.maximum(
        BOUND_MARGIN * C_HD * (C_HD ** -0.5 * LOG2E)
        * jnp.max(jnp.abs(p["c_qn"].astype(F32))) * jnp.max(jnp.abs(p["c_kn"].astype(F32))),
        jnp.max(sink2))
    oc, = _band(pc, lambda sl: sl, lambda sl: 4 + sl // 2, lambda sl: 6 + sl // 2, bsz, s, 1, 8,
                C_WINDOW, _alibi_slopes(C_QH), sink2, True, False, 512, bound_c)

    ods, lses = [], []
    zero_sink = jnp.zeros((D_HEADS,), F32)
    bound_d = (BOUND_MARGIN * D_HD * (D_HD ** -0.5 * LOG2E)
               * jnp.max(jnp.abs(p["d_qn"].astype(F32))) * jnp.max(jnp.abs(p["d_kn"].astype(F32))))
    for g, (win, dil) in enumerate(DIL_PAIRS):
        og, lg = _band((pd0, pd1, pd2)[g], lambda sl: sl, lambda sl: 4 + sl,
                       lambda sl: 8 + sl, bsz, s, dil, 12, win // (2 * dil),
                       _alibi_slopes(D_HEADS), zero_sink, False, True, 512, bound_d)
        ods.append(og)
        lses.append(lg)

    wkv = p["w_mem_kv"].astype(BF16)
    mgain = jnp.concatenate([jnp.tile(p["m_kn"].astype(F32), M_HEADS), jnp.ones((M_HEADS * M_HD,), F32)]).reshape(1, -1)
    mtiles = [(0, SEG128, 0, 0), (256, SEG128, 0, 256), (512, SEG_NONE, 1, 0), (768, SEG_NONE, 1, 256)]
    mk, mv = _proj(mem.reshape(bsz * N_MEM, d), p["m_norm"].astype(F32), wkv, mgain, mtiles, [512, 512], tm=256)
    om = _memattn(pm, mk, mv, bsz, s, tq=512)

    wz = w_in[:, OFF_Z:OFF_G].astype(BF16)
    wg = w_in[:, OFF_G:].astype(BF16)
    y = _merge(x2d, p, wz, wg, oa, ob, oc, om, ods, lses, tm=512)
    return y.reshape(bsz, s, d)


def kernel(x_prompt, x_sample, mem_prompt, mem_sample, norm_g, w_in, a_qn, a_kn, a_lam, a_hn, b_cqn, b_ckvn, w_qb, w_kvb, b_qn, b_kn, c_qn, c_kn, c_sink, d_qn, d_kn, m_norm, w_mem_kv, m_qn, m_kn, b_gate, w_br, w_out):
    depth = norm_g.shape[0]
    y_prompt, y_sample = x_prompt, x_sample
    for l in range(depth):
        p = {
            "norm_g": norm_g[l], "w_in": w_in[l],
            "a_qn": a_qn[l], "a_kn": a_kn[l], "a_lam": a_lam[l], "a_hn": a_hn[l],
            "b_cqn": b_cqn[l], "b_ckvn": b_ckvn[l], "w_qb": w_qb[l], "w_kvb": w_kvb[l],
            "b_qn": b_qn[l], "b_kn": b_kn[l],
            "c_qn": c_qn[l], "c_kn": c_kn[l], "c_sink": c_sink[l],
            "d_qn": d_qn[l], "d_kn": d_kn[l],
            "m_norm": m_norm[l], "w_mem_kv": w_mem_kv[l], "m_qn": m_qn[l], "m_kn": m_kn[l],
            "b_gate": b_gate[l], "w_br": w_br[l], "w_out": w_out[l],
        }
        y_prompt = _encoder_layer(y_prompt, mem_prompt, l, p)
        y_sample = _encoder_layer(y_sample, mem_sample, l, p)
    return (y_prompt, y_sample)
```

```python
import functools
import math

import numpy as np
import jax
import jax.numpy as jnp
from jax import lax
from jax.experimental import pallas as pl
from jax.experimental.pallas import tpu as pltpu

F32 = jnp.float32
BF16 = jnp.bfloat16

D_MODEL = 1024
N_MEM = 256
BRANCH_W = 512
N_BRANCH = 5
NEG_INF = -1e30
EPS = 1e-6
A_HEADS, A_DK, A_DV = 4, 64, 128
B_HEADS, B_Q_LORA, B_KV_LORA, B_NOPE, B_ROPE, B_DV = 4, 256, 128, 64, 32, 128
ROPE_THETA = 10000.0
C_QH, C_KVH, C_HD, C_WINDOW = 8, 2, 64, 128
D_HEADS, D_HD = 8, 64
DIL_PAIRS = ((128, 1), (512, 4), (2048, 16))
N_DIL = 3
M_HEADS, M_HD = 4, 128

OFF_A_Q = 0
OFF_A_K = OFF_A_Q + 2 * A_HEADS * A_DK
OFF_A_V = OFF_A_K + 2 * A_HEADS * A_DK
OFF_B_CQ = OFF_A_V + A_HEADS * A_DV
OFF_B_CKV = OFF_B_CQ + B_Q_LORA
OFF_B_KR = OFF_B_CKV + B_KV_LORA
OFF_C_Q = OFF_B_KR + B_ROPE
OFF_C_K = OFF_C_Q + C_QH * C_HD
OFF_C_V = OFF_C_K + C_KVH * C_HD
OFF_D_Q = OFF_C_V + C_KVH * C_HD
OFF_D_K = OFF_D_Q + N_DIL * D_HEADS * D_HD
OFF_D_V = OFF_D_K + N_DIL * D_HEADS * D_HD
OFF_M_Q = OFF_D_V + N_DIL * D_HEADS * D_HD
OFF_Z = OFF_M_Q + M_HEADS * M_HD
OFF_G = OFF_Z + N_BRANCH * BRANCH_W
N_IN = OFF_G + N_BRANCH * D_MODEL

LANES = 128
COL_TILE = 256
VMEM_LIMIT = 56 * 1024 * 1024

SEG_NONE, SEG64, SEG128 = -1, 0, 1

LOG2E = 1.4426950408889634
MAX_FIXED_SHIFT = 30.0 * LOG2E
BOUND_MARGIN = 1.02
BAND_ROWS = 2048


def _cparams(sem):
    return pltpu.CompilerParams(dimension_semantics=sem, vmem_limit_bytes=VMEM_LIMIT)


def _resident(shape, index_map):
    return pl.BlockSpec(shape, index_map, pipeline_mode=pl.Buffered(1))


def _seg_matrix(width, segs):
    m = np.zeros((width, width), np.float32)
    for s, n in segs:
        m[s:s + n, s:s + n] = 1.0 / n
    return m


def _proj_kernel(x_ref, ng_ref, w_ref, gain_ref, mseg_ref, *out_refs, tiles):
    x = x_ref[...]
    ms = jnp.mean(x * x, axis=-1, keepdims=True)
    h = ((x * lax.rsqrt(ms + EPS)) * ng_ref[...]).astype(BF16)
    assert len(tiles) % 2 == 0
    for pair in range(0, len(tiles), 2):
        base = tiles[pair][0]
        assert tiles[pair + 1][0] == base + COL_TILE
        y2 = jnp.dot(h, w_ref[:, base:base + 2 * COL_TILE], preferred_element_type=F32)
        for half, (c0, seg, oi, oc) in enumerate(tiles[pair:pair + 2]):
            y = y2[:, half * COL_TILE:(half + 1) * COL_TILE]
            if seg != SEG_NONE:
                sq = (y * y).astype(BF16)
                segms = jnp.dot(sq, mseg_ref[seg], preferred_element_type=F32)
                y = (y * lax.rsqrt(segms + EPS)) * gain_ref[:, c0:c0 + COL_TILE]
            out_refs[oi][:, oc:oc + COL_TILE] = y.astype(BF16)


def _proj(x2d, norm_gain, w, gain, tiles, out_widths, tm):
    t, d = x2d.shape
    n = w.shape[1]
    tm = min(tm, t)
    mseg = jnp.asarray(
        np.stack([_seg_matrix(COL_TILE, [(s, 64) for s in range(0, COL_TILE, 64)]),
                  _seg_matrix(COL_TILE, [(s, 128) for s in range(0, COL_TILE, 128)])]), BF16)
    return pl.pallas_call(
        functools.partial(_proj_kernel, tiles=tuple(tiles)),
        grid=(t // tm,),
        in_specs=[
            pl.BlockSpec((tm, d), lambda i: (i, 0)),
            _resident((1, d), lambda i: (0, 0)),
            _resident((d, n), lambda i: (0, 0)),
            _resident((1, n), lambda i: (0, 0)),
            _resident((2, COL_TILE, COL_TILE), lambda i: (0, 0, 0)),
        ],
        out_specs=[pl.BlockSpec((tm, wd), lambda i: (i, 0)) for wd in out_widths],
        out_shape=[jax.ShapeDtypeStruct((t, wd), BF16) for wd in out_widths],
        compiler_params=_cparams(("parallel",)),
        name="proj",
    )(x2d, norm_gain.reshape(1, d), w, gain, mseg)


def _main_plan():
    zero = N_IN
    cols, gains, tiles = [], [], []
    widths = [3 * 512, 512, 1024] + [3 * 512] * N_DIL + [512]

    def add(out_idx, out_col, src_cols, seg, gain_key):
        assert len(src_cols) % COL_TILE == 0
        c0 = len(cols)
        cols.extend(src_cols)
        gains.extend([gain_key] * len(src_cols))
        for k in range(len(src_cols) // COL_TILE):
            tiles.append((c0 + k * COL_TILE, seg, out_idx, out_col + k * COL_TILE))

    hm = [m * A_HEADS * A_DK + h * A_DK + d for h in range(A_HEADS) for m in range(2) for d in range(A_DK)]
    add(0, 0, [OFF_A_Q + c for c in hm], SEG64, "a_q")
    add(0, 512, [OFF_A_K + c for c in hm], SEG64, "a_k")
    add(0, 1024, [OFF_A_V + c for c in range(A_HEADS * A_DV)], SEG_NONE, None)
    braw = ([OFF_B_CQ + c for c in range(B_Q_LORA)] + [OFF_B_CKV + c for c in range(B_KV_LORA)]
            + [OFF_B_KR + c for c in range(B_ROPE)] + [zero] * (512 - B_Q_LORA - B_KV_LORA - B_ROPE))
    add(1, 0, braw, SEG_NONE, None)
    add(2, 0, [OFF_C_Q + c for c in range(C_QH * C_HD)], SEG64, "c_q")
    dup = [kv * C_HD + d for kv in range(C_KVH) for _ in range(2) for d in range(C_HD)]
    add(2, 512, [OFF_C_K + c for c in dup], SEG64, "c_k")
    add(2, 768, [OFF_C_V + c for c in dup], SEG_NONE, None)
    for g in range(N_DIL):
        gsl = [g * D_HEADS * D_HD + c for c in range(D_HEADS * D_HD)]
        add(3 + g, 0, [OFF_D_Q + c for c in gsl], SEG64, "d_q")
        add(3 + g, 512, [OFF_D_K + c for c in gsl], SEG64, "d_k")
        add(3 + g, 1024, [OFF_D_V + c for c in gsl], SEG_NONE, None)
    add(3 + N_DIL, 0, [OFF_M_Q + c for c in range(M_HEADS * M_HD)], SEG128, "m_q")
    return np.asarray(cols, np.int32), gains, tiles, widths


_MAIN_COLS, _MAIN_GAINS, _MAIN_TILES, _MAIN_WIDTHS = _main_plan()


def _main_gain_vector(p):
    per_key = {
        "a_q": p["a_qn"] * (A_DK ** -0.5 * LOG2E), "a_k": p["a_kn"],
        "c_q": p["c_qn"] * (C_HD ** -0.5 * LOG2E), "c_k": p["c_kn"],
        "d_q": p["d_qn"] * (D_HD ** -0.5 * LOG2E), "d_k": p["d_kn"],
        "m_q": p["m_qn"] * (M_HD ** -0.5),
    }
    n = len(_MAIN_GAINS)
    pieces, k = [], 0
    while k < n:
        key = _MAIN_GAINS[k]
        k2 = k
        while k2 < n and _MAIN_GAINS[k2] == key:
            k2 += 1
        if key is None:
            pieces.append(jnp.ones((k2 - k,), F32))
        else:
            gvec = per_key[key].astype(F32)
            pieces.append(jnp.tile(gvec, (k2 - k) // gvec.shape[0]))
        k = k2
    return jnp.concatenate(pieces).reshape(1, n)


def _mla_prep_kernel(b_ref, wq_ref, wk_ref, wv_ref, place_ref, mseg_ref, cqg_ref, ckvg_ref,
                     qg_ref, kg_ref, ct_ref, s1_ref, s2_ref, q_out, k_out, v_out):
    braw = b_ref[...]
    cq = braw[:, :B_Q_LORA].astype(F32)
    cqn = ((cq * lax.rsqrt(jnp.mean(cq * cq, axis=-1, keepdims=True) + EPS)) * cqg_ref[...]).astype(BF16)
    ckv = braw[:, B_Q_LORA:B_Q_LORA + B_KV_LORA].astype(F32)
    ckvn = ((ckv * lax.rsqrt(jnp.mean(ckv * ckv, axis=-1, keepdims=True) + EPS)) * ckvg_ref[...]).astype(BF16)
    qb = jnp.dot(cqn, wq_ref[...], preferred_element_type=F32)
    kb = jnp.dot(ckvn, wk_ref[...], preferred_element_type=F32)
    kb = kb + jnp.dot(braw[:, B_Q_LORA + B_KV_LORA:], place_ref[...], preferred_element_type=F32)
    v_out[...] = jnp.dot(ckvn, wv_ref[...], preferred_element_type=F32).astype(BF16)
    ct, s1, s2 = ct_ref[...], s1_ref[...], s2_ref[...]

    def finish(t, gain_ref, out):
        for h in range(B_HEADS):
            y = t[:, h * LANES:(h + 1) * LANES]
            segms = jnp.dot((y * y).astype(BF16), mseg_ref[...], preferred_element_type=F32)
            yn = (y * lax.rsqrt(segms + EPS)) * gain_ref[...]
            half = B_ROPE // 2
            r = yn * ct + pltpu.roll(yn, half, 1) * s1 + pltpu.roll(yn, LANES - half, 1) * s2
            out[:, h * LANES:(h + 1) * LANES] = r.astype(BF16)

    finish(qb, qg_ref, q_out)
    finish(kb, kg_ref, k_out)


def _rope_tables(s):
    half = B_ROPE // 2
    inv = ROPE_THETA ** (-jnp.arange(half, dtype=F32) / half)
    ang = jnp.arange(s, dtype=F32)[:, None] * inv[None, :]
    cos, sin = jnp.cos(ang), jnp.sin(ang)
    one = jnp.ones((s, B_NOPE), F32)
    zpad = jnp.zeros((s, LANES - B_NOPE - B_ROPE), F32)
    z64 = jnp.zeros((s, B_NOPE), F32)
    zh = jnp.zeros((s, half), F32)
    ct = jnp.concatenate([one, cos, cos, zpad], axis=1)
    s1 = jnp.concatenate([z64, zh, sin, zpad], axis=1)
    s2 = jnp.concatenate([z64, -sin, zh, zpad], axis=1)
    return ct, s1, s2


def _mla_prep(braw, p, bsz, s, tm):
    t = braw.shape[0]
    tm = min(tm, s)
    pad = LANES - B_NOPE - B_ROPE
    wq = p["w_qb"].reshape(B_Q_LORA, B_HEADS, B_NOPE + B_ROPE)
    wq = jnp.pad(wq, ((0, 0), (0, 0), (0, pad))).reshape(B_Q_LORA, B_HEADS * LANES).astype(BF16)
    wkv = p["w_kvb"].reshape(B_KV_LORA, B_HEADS, B_NOPE + B_DV)
    wk = jnp.pad(wkv[:, :, :B_NOPE], ((0, 0), (0, 0), (0, LANES - B_NOPE))).reshape(B_KV_LORA, B_HEADS * LANES).astype(BF16)
    wv = wkv[:, :, B_NOPE:].reshape(B_KV_LORA, B_HEADS * B_DV).astype(BF16)
    place = np.zeros((LANES, B_HEADS * LANES), np.float32)
    for h in range(B_HEADS):
        for d in range(B_ROPE):
            place[d, h * LANES + B_NOPE + d] = 1.0
    mseg = _seg_matrix(LANES, [(0, B_NOPE), (B_NOPE, B_ROPE)])
    scale = (B_NOPE + B_ROPE) ** -0.5 * LOG2E
    zp = jnp.zeros((pad,), F32)
    qg = jnp.concatenate([p["b_qn"].astype(F32) * scale, zp]).reshape(1, LANES)
    kg = jnp.concatenate([p["b_kn"].astype(F32), zp]).reshape(1, LANES)
    ct, s1, s2 = _rope_tables(s)
    nst = s // tm
    full = lambda shape: _resident(shape, lambda b, i: tuple(0 for _ in shape))
    row = lambda wd: pl.BlockSpec((tm, wd), lambda b, i: (b * nst + i, 0))
    tab = pl.BlockSpec((tm, LANES), lambda b, i: (i, 0))
    return pl.pallas_call(
        _mla_prep_kernel,
        grid=(bsz, nst),
        in_specs=[row(512), full((B_Q_LORA, 512)), full((B_KV_LORA, 512)), full((B_KV_LORA, 512)),
                  full((LANES, 512)), full((LANES, LANES)), full((1, B_Q_LORA)), full((1, B_KV_LORA)),
                  full((1, LANES)), full((1, LANES)), tab, tab, tab],
        out_specs=[row(512), row(512), row(512)],
        out_shape=[jax.ShapeDtypeStruct((t, 512), BF16)] * 3,
        compiler_params=_cparams(("parallel", "parallel")),
        name="mla_prep",
    )(braw, wq, wk, wv, jnp.asarray(place, BF16), jnp.asarray(mseg, BF16),
      p["b_cqn"].astype(F32).reshape(1, -1), p["b_ckvn"].astype(F32).reshape(1, -1), qg, kg, ct, s1, s2)


def _flash_kernel(slopes_ref, lam_ref, hn_ref, q_ref, k_ref, v_ref, o_ref, m_sc, l_sc, acc_sc,
                  *, nmaps, alibi, tq, tk, seq, lam_init):
    h = pl.program_id(1)
    q0 = pl.program_id(2) * tq
    q = q_ref[...]
    if nmaps == 2:
        lane = lax.broadcasted_iota(jnp.int32, q.shape, 1)
        qs = [jnp.where(lane < A_DK, q, jnp.zeros_like(q)), jnp.where(lane >= A_DK, q, jnp.zeros_like(q))]
    else:
        qs = [q]
    m_sc[...] = jnp.full(m_sc.shape, NEG_INF, F32)
    l_sc[...] = jnp.zeros(l_sc.shape, F32)
    acc_sc[...] = jnp.zeros(acc_sc.shape, F32)
    if alibi:
        slope2 = slopes_ref[h] * LOG2E
        dmat = (lax.broadcasted_iota(jnp.int32, (tq, tk), 0)
                - lax.broadcasted_iota(jnp.int32, (tq, tk), 1)).astype(F32)

    def body(j, carry):
        k0 = pl.multiple_of(j * tk, tk)
        k = k_ref[pl.ds(k0, tk), :]
        v = v_ref[pl.ds(k0, tk), :]
        if alibi:
            bias = -slope2 * jnp.abs(dmat + (q0 - k0).astype(F32))
        for m in range(nmaps):
            s = lax.dot_general(qs[m], k, (((1,), (1,)), ((), ())), preferred_element_type=F32)
            if alibi:
                s = s + bias
            m_old = m_sc[m]
            m_new = jnp.maximum(m_old, jnp.max(s, axis=1, keepdims=True))
            alpha = jnp.exp2(m_old - m_new)
            pr = jnp.exp2(s - m_new)
            l_sc[m] = alpha * l_sc[m] + jnp.sum(pr, axis=1, keepdims=True)
            acc_sc[m] = alpha * acc_sc[m] + jnp.dot(pr.astype(BF16), v, preferred_element_type=F32)
            m_sc[m] = m_new
        return carry

    lax.fori_loop(0, seq // tk, body, 0)
    outs = [acc_sc[m] / l_sc[m] for m in range(nmaps)]
    o_ref[...] = _flash_epilogue(outs, lam_ref, hn_ref, lam_init).astype(BF16)


def _flash_epilogue(outs, lam_ref, hn_ref, lam_init):
    if len(outs) == 1:
        return outs[0]
    lf = lam_ref[...]
    e1 = jnp.exp(jnp.sum(lf[0:1] * lf[1:2], axis=1, keepdims=True))
    e2 = jnp.exp(jnp.sum(lf[2:3] * lf[3:4], axis=1, keepdims=True))
    lam = e1 - e2 + lam_init
    o = outs[0] - lam * outs[1]
    o = (o * lax.rsqrt(jnp.mean(o * o, axis=-1, keepdims=True) + EPS)) * hn_ref[...]
    return o * (1.0 - lam_init)


def _flash_fixed_kernel(shift_ref, slopes_ref, lam_ref, hn_ref, q_ref, k_ref, v_ref, o_ref, v1_sc, acc_sc,
                        *, nmaps, alibi, tq, tk, seq, lam_init, unroll):
    h = pl.program_id(1)
    i = pl.program_id(2)
    q0 = i * tq

    @pl.when(i == 0)
    def _():
        v1_sc[:, :LANES] = v_ref[...]
        v1_sc[:, LANES:] = jnp.ones((seq, LANES), BF16)

    q = q_ref[...]
    if nmaps == 2:
        lane = lax.broadcasted_iota(jnp.int32, q.shape, 1)
        qs = [jnp.where(lane < A_DK, q, jnp.zeros_like(q)), jnp.where(lane >= A_DK, q, jnp.zeros_like(q))]
    else:
        qs = [q]
    acc_sc[...] = jnp.zeros(acc_sc.shape, F32)
    shift = shift_ref[0]
    if alibi:
        slope2 = slopes_ref[h] * LOG2E
        t0 = (lax.broadcasted_iota(jnp.int32, (tq, tk), 0)
              - lax.broadcasted_iota(jnp.int32, (tq, tk), 1)).astype(F32) * slope2

    def body(j, carry):
        k0 = pl.multiple_of(j * tk, tk)
        k = k_ref[pl.ds(k0, tk), :]
        v1 = v1_sc[pl.ds(k0, tk), :]
        if alibi:
            sub = jnp.abs(t0 + slope2 * (q0 - k0).astype(F32)) + shift
        ps = []
        for m in range(nmaps):
            s = lax.dot_general(qs[m], k, (((1,), (1,)), ((), ())), preferred_element_type=F32)
            ps.append(jnp.exp2(s - sub if alibi else s - shift).astype(BF16))
        pr = ps[0] if nmaps == 1 else jnp.concatenate(ps, axis=0)
        acc_sc[...] += jnp.dot(pr, v1, preferred_element_type=F32)
        return carry

    lax.fori_loop(0, seq // tk, body, 0, unroll=unroll)
    outs = [acc_sc[m * tq:(m + 1) * tq, :LANES] / acc_sc[m * tq:(m + 1) * tq, LANES:] for m in range(nmaps)]
    o_ref[...] = _flash_epilogue(outs, lam_ref, hn_ref, lam_init).astype(BF16)


def _flash(qa, ka, va, qcb, kcb, vcb, bsz, s, heads, nmaps, alibi, slopes, lam, hn, lam_init, bound2):
    t = qa.shape[0]
    name = "flash_diff" if nmaps == 2 else "flash_mla"
    shift = bound2.reshape(1).astype(F32)

    def call(kern, tq, tk, scratch, extra_in, extra_args, suffix):
        tq_, tk_ = min(tq, s), min(tk, s)
        nq = s // tq_
        return pl.pallas_call(
            functools.partial(kern, nmaps=nmaps, alibi=alibi, tq=tq_, tk=tk_, seq=s, lam_init=lam_init),
            grid=(bsz, heads, nq),
            in_specs=extra_in + [
                pl.BlockSpec(memory_space=pltpu.SMEM),
                _resident(lam.shape, lambda b, h, i: (0, 0)),
                _resident(hn.shape, lambda b, h, i: (0, 0)),
                pl.BlockSpec((tq_, LANES), lambda b, h, i: (b * nq + i, qcb + h)),
                pl.BlockSpec((s, LANES), lambda b, h, i: (b, kcb + h)),
                pl.BlockSpec((s, LANES), lambda b, h, i: (b, vcb + h)),
            ],
            out_specs=pl.BlockSpec((tq_, LANES), lambda b, h, i: (b * nq + i, h)),
            out_shape=jax.ShapeDtypeStruct((t, heads * LANES), BF16),
            scratch_shapes=scratch(tq_),
            compiler_params=_cparams(("parallel", "parallel", "arbitrary")),
            name=name + suffix,
        )(*extra_args, slopes, lam, hn, qa, ka, va)

    def fixed():
        return call(functools.partial(_flash_fixed_kernel, unroll=True), 512, 512,
                    lambda tq_: [pltpu.VMEM((s, 2 * LANES), BF16), pltpu.VMEM((nmaps * tq_, 2 * LANES), F32)],
                    [pl.BlockSpec(memory_space=pltpu.SMEM)], [shift], "_fixed")

    def online():
        return call(_flash_kernel, 256, 512,
                    lambda tq_: [pltpu.VMEM((nmaps, tq_, 1), F32), pltpu.VMEM((nmaps, tq_, 1), F32),
                                 pltpu.VMEM((nmaps, tq_, LANES), F32)],
                    [], [], "_online")

    return lax.cond(bound2 <= MAX_FIXED_SHIFT, fixed, online)


def _band_kernel(slopes_ref, sink_ref, q_ref, k_ref, v_ref, *outs, window, qb, kw, ut, u_len, dil, use_sink, emit_lse):
    o_ref = outs[0]
    slab = pl.program_id(0)
    ubase = pl.program_id(3) * ut
    lane = lax.broadcasted_iota(jnp.int32, (kw, LANES), 1)
    qlane = lax.broadcasted_iota(jnp.int32, (qb, LANES), 1)
    dmat = (lax.broadcasted_iota(jnp.int32, (qb, kw), 0) - lax.broadcasted_iota(jnp.int32, (qb, kw), 1))
    for sb in range(ut // qb):
        u0 = ubase + sb * qb
        start = pl.multiple_of(jnp.clip(u0 - window, 0, u_len - kw), 16)
        q = q_ref[sb * qb:(sb + 1) * qb, :]
        k = k_ref[pl.ds(start, kw), :]
        v = v_ref[pl.ds(start, kw), :]
        rel = dmat + (u0 - start)
        valid = jnp.abs(rel) <= window
        dist = jnp.abs(rel).astype(F32) * float(dil)
        o_acc = jnp.zeros((qb, LANES), F32)
        lse_acc = jnp.zeros((qb, LANES), F32)
        for hh in range(2):
            head = slab * 2 + hh
            sel_q = (qlane < 64) if hh == 0 else (qlane >= 64)
            sel_v = (lane < 64) if hh == 0 else (lane >= 64)
            qh = jnp.where(sel_q, q, jnp.zeros_like(q))
            vh = jnp.where(sel_v, v, jnp.zeros_like(v))
            s = lax.dot_general(qh, k, (((1,), (1,)), ((), ())), preferred_element_type=F32)
            logits = jnp.where(valid, s - (slopes_ref[head] * LOG2E) * dist, NEG_INF)
            mx = jnp.max(logits, axis=1, keepdims=True)
            if use_sink:
                sk = sink_ref[head]
                mx = jnp.maximum(mx, sk)
            e = jnp.exp2(logits - mx)
            den = jnp.sum(e, axis=1, keepdims=True)
            if use_sink:
                den = den + jnp.exp2(sk - mx)
            pv = jnp.dot(e.astype(BF16), vh, preferred_element_type=F32)
            o_acc = o_acc + pv / den
            if emit_lse:
                lse_acc = lse_acc + jnp.where(sel_q, mx + jnp.log2(den), 0.0)
        o_ref[sb * qb:(sb + 1) * qb, :] = o_acc.astype(BF16)
        if emit_lse:
            outs[1][sb * qb:(sb + 1) * qb, :] = lse_acc


def _band_fixed_kernel(shift_ref, slopes_ref, sink_ref, q_ref, k_ref, v_ref, *outs,
                       window, qb, kw, ut, u_len, dil, use_sink, emit_lse):
    o_ref, sub_sc = outs[0], outs[-1]
    slab = pl.program_id(0)
    ubase = pl.program_id(3) * ut
    shift = shift_ref[0]
    qlane = lax.broadcasted_iota(jnp.int32, (qb, LANES), 1)
    first = qlane < 64
    ones = jnp.ones((kw, LANES), BF16)

    @pl.when((pl.program_id(1) == 0) & (pl.program_id(2) == 0) & (pl.program_id(3) == 0))
    def _():
        dmat =(lax.broadcasted_iota(jnp.int32, (qb, kw), 0) - lax.broadcasted_iota(jnp.int32, (qb, kw), 1))
        for case in range(3):
            absrel = jnp.abs(dmat + case * window)
            for hh in range(2):
                slope2 = slopes_ref[slab * 2 + hh] * (LOG2E * dil)
                sub_sc[case, hh] = jnp.where(absrel <= window, absrel.astype(F32) * slope2 + shift, -NEG_INF)

    for sb in range(ut // qb):
        u0 = ubase + sb * qb
        start = pl.multiple_of(jnp.clip(u0 - window, 0, u_len - kw), 16)
        case = (u0 - start) // window
        q = q_ref[sb * qb:(sb + 1) * qb, :]
        k = k_ref[pl.ds(start, kw), :]
        v1 = jnp.concatenate([v_ref[pl.ds(start, kw), :], ones], axis=1)
        ps = []
        for hh in range(2):
            qh = jnp.where(first if hh == 0 else jnp.logical_not(first), q, jnp.zeros_like(q))
            s = lax.dot_general(qh, k, (((1,), (1,)), ((), ())), preferred_element_type=F32)
            ps.append(jnp.exp2(s - sub_sc[case, hh]).astype(BF16))
        r = jnp.dot(jnp.concatenate(ps, axis=0), v1, preferred_element_type=F32)
        za, zb = r[:qb, LANES:], r[qb:, LANES:]
        if use_sink:
            za = za + jnp.exp2(jnp.full((1, 1), sink_ref[slab * 2] - shift, F32))
            zb = zb + jnp.exp2(jnp.full((1, 1), sink_ref[slab * 2 + 1] - shift, F32))
        o_ref[sb * qb:(sb + 1) * qb, :] = jnp.where(first, r[:qb, :LANES] / za, r[qb:, :LANES] / zb).astype(BF16)
        if emit_lse:
            outs[1][sb * qb:(sb + 1) * qb, :] = jnp.where(first, jnp.log2(za), jnp.log2(zb)) + shift


def _band(src, qcb, kcb, vcb, bsz, s, dil, ncb_src, window, slopes, sink, use_sink, emit_lse, ut, bound2):
    t, c = src.shape
    u_len = s // dil
    qb = min(LANES, u_len)
    kw = min(qb + 2 * window, u_len)
    ut = min(ut, u_len)
    nu = u_len // ut
    srcv = src.reshape(bsz * u_len, dil * c)
    nslab = 4
    out_shapes = [jax.ShapeDtypeStruct((bsz * u_len, dil * 512), BF16)]
    out_specs = [pl.BlockSpec((ut, LANES), lambda sl, b, r, i: (b * nu + i, r * nslab + sl))]
    if emit_lse:
        out_shapes.append(jax.ShapeDtypeStruct((bsz * u_len, dil * 512), F32))
        out_specs.append(pl.BlockSpec((ut, LANES), lambda sl, b, r, i: (b * nu + i, r * nslab + sl)))
    smem = pl.BlockSpec(memory_space=pltpu.SMEM)

    def call(kern, extra_in, extra_args, suffix, scratch=()):
        return pl.pallas_call(
            functools.partial(kern, window=window, qb=qb, kw=kw, ut=ut, u_len=u_len, dil=dil,
                              use_sink=use_sink, emit_lse=emit_lse),
            grid=(nslab, bsz, dil, nu),
            scratch_shapes=list(scratch),
            in_specs=extra_in + [
                smem, smem,
                pl.BlockSpec((ut, LANES), lambda sl, b, r, i: (b * nu + i, r * ncb_src + qcb(sl))),
                pl.BlockSpec((u_len, LANES), lambda sl, b, r, i: (b, r * ncb_src + kcb(sl))),
                pl.BlockSpec((u_len, LANES), lambda sl, b, r, i: (b, r * ncb_src + vcb(sl))),
            ],
            out_specs=out_specs,
            out_shape=out_shapes,
            compiler_params=_cparams(("arbitrary", "arbitrary", "arbitrary", "arbitrary")),
            name="band_w%d_d%d%s" % (window, dil, suffix),
        )(*extra_args, slopes, sink, srcv, srcv, srcv)

    res = lax.cond(bound2 <= MAX_FIXED_SHIFT,
                   lambda: call(_band_fixed_kernel, [smem], [bound2.reshape(1).astype(F32)], "_fixed",
                                [pltpu.VMEM((3, 2, qb, kw), F32)]),
                   lambda: call(_band_kernel, [], [], "_online"))
    return [r.reshape(t, 512) for r in res]


def _memattn_kernel(q_ref, k_ref, v_ref, o_ref):
    for h in range(M_HEADS):
        sl = slice(h * M_HD, (h + 1) * M_HD)
        s = lax.dot_general(q_ref[:, sl], k_ref[:, sl], (((1,), (1,)), ((), ())), preferred_element_type=F32)
        mx = jnp.max(s, axis=1, keepdims=True)
        e = jnp.exp(s - mx)
        den = jnp.sum(e, axis=1, keepdims=True)
        pv = jnp.dot(e.astype(BF16), v_ref[:, sl], preferred_element_type=F32)
        o_ref[:, sl] = (pv / den).astype(BF16)


def _memattn(qm, mk, mv, bsz, s, tq):
    t = qm.shape[0]
    tq = min(tq, s)
    nq = s // tq
    wd = M_HEADS * M_HD
    return pl.pallas_call(
        _memattn_kernel,
        grid=(bsz, nq),
        in_specs=[pl.BlockSpec((tq, wd), lambda b, i: (b * nq + i, 0)),
                  pl.BlockSpec((N_MEM, wd), lambda b, i: (b, 0)),
                  pl.BlockSpec((N_MEM, wd), lambda b, i: (b, 0))],
        out_specs=pl.BlockSpec((tq, wd), lambda b, i: (b * nq + i, 0)),
        out_shape=jax.ShapeDtypeStruct((t, wd), BF16),
        compiler_params=_cparams(("parallel", "parallel")),
        name="memattn",
    )(qm, mk, mv)


def _merge_kernel(x_ref, ng_ref, wz_ref, wg_ref, bg_ref, wbr_ref, wout_ref,
                  oa_ref, ob_ref, oc_ref, om_ref, od0_ref, od1_ref, od2_ref, l0_ref, l1_ref, l2_ref,
                  y_ref, h_sc, acc_sc, o_sc):
    j = pl.program_id(1)

    @pl.when(j == 0)
    def _():
        x = x_ref[...]
        ms = jnp.mean(x * x, axis=-1, keepdims=True)
        h_sc[...] = ((x * lax.rsqrt(ms + EPS)) * ng_ref[...]).astype(BF16)
        acc_sc[...] = jnp.zeros(acc_sc.shape, F32)
        o_sc[...] = oa_ref[...].astype(F32)

    @pl.when(j == 1)
    def _():
        o_sc[...] = ob_ref[...].astype(F32)

    @pl.when(j == 2)
    def _():
        o_sc[...] = oc_ref[...].astype(F32)

    @pl.when(j == 3)
    def _():
        l0, l1, l2 = l0_ref[...], l1_ref[...], l2_ref[...]
        mx = jnp.maximum(jnp.maximum(l0, l1), l2)
        e0, e1, e2 = jnp.exp2(l0 - mx), jnp.exp2(l1 - mx), jnp.exp2(l2 - mx)
        den = e0 + e1 + e2
        o_sc[...] = ((e0 / den) * od0_ref[...].astype(F32) + (e1 / den) * od1_ref[...].astype(F32)
                     + (e2 / den) * od2_ref[...].astype(F32))

    @pl.when(j == 4)
    def _():
        o_sc[...] = om_ref[...].astype(F32)

    h = h_sc[...]
    z = jnp.dot(h, wz_ref[...], preferred_element_type=F32)
    g = jnp.dot(h, wg_ref[...], preferred_element_type=F32) + bg_ref[0]
    u = (o_sc[...] * (z / (1.0 + jnp.exp(-z)))).astype(BF16)
    tbr = jnp.dot(u, wbr_ref[0], preferred_element_type=F32)
    acc_sc[...] += tbr / (1.0 + jnp.exp(-g))

    @pl.when(j == N_BRANCH - 1)
    def _():
        y_ref[...] = x_ref[...] + jnp.dot(acc_sc[...].astype(BF16), wout_ref[...], preferred_element_type=F32)


def _merge(x2d, p, wz, wg, oa, ob, oc, om, ods, lses, tm):
    t, d = x2d.shape
    tm = min(tm, t)
    rowf = lambda wd: pl.BlockSpec((tm, wd), lambda i, j: (i, 0))
    return pl.pallas_call(
        _merge_kernel,
        grid=(t // tm, N_BRANCH),
        in_specs=[
            rowf(d),
            _resident((1, d), lambda i, j: (0, 0)),
            pl.BlockSpec((d, BRANCH_W), lambda i, j: (0, j)),
            pl.BlockSpec((d, d), lambda i, j: (0, j)),
            pl.BlockSpec((1, 1, d), lambda i, j: (j, 0, 0)),
            pl.BlockSpec((1, BRANCH_W, d), lambda i, j: (j, 0, 0)),
            _resident((d, d), lambda i, j: (0, 0)),
        ] + [rowf(BRANCH_W)] * 10,
        out_specs=rowf(d),
        out_shape=jax.ShapeDtypeStruct((t, d), F32),
        scratch_shapes=[pltpu.VMEM((tm, d), BF16), pltpu.VMEM((tm, d), F32), pltpu.VMEM((tm, BRANCH_W), F32)],
        compiler_params=_cparams(("parallel", "arbitrary")),
        name="merge",
    )(x2d, p["norm_g"].astype(F32).reshape(1, d), wz, wg, p["b_gate"].astype(F32).reshape(N_BRANCH, 1, d),
      p["w_br_bf16"], p["w_out_bf16"], oa, ob, oc, om, *ods, *lses)


def _alibi_slopes(n):
    return jnp.asarray(2.0 ** (-8.0 * np.arange(1, n + 1, dtype=np.float64) / n), F32)


def _encoder_layer(x, mem, layer_idx, p):
    bsz, s, d = x.shape
    t = bsz * s
    x2d = x.reshape(t, d)

    pa, pb, pc, pd0, pd1, pd2, pm = _proj(x2d, p["norm_g"].astype(F32), p["w_main"], p["gain_main"],
                                          _MAIN_TILES, _MAIN_WIDTHS, tm=512)

    lam_init = 0.8 - 0.6 * math.exp(-0.3 * layer_idx)
    bound_a = (BOUND_MARGIN * A_DK * (A_DK ** -0.5 * LOG2E)
               * jnp.max(jnp.abs(p["a_qn"].astype(F32))) * jnp.max(jnp.abs(p["a_kn"].astype(F32))))
    oa = _flash(pa, pa, pa, 0, 4, 8, bsz, s, A_HEADS, 2, True, _alibi_slopes(A_HEADS),
                p["a_lam"].astype(F32), p["a_hn"].astype(F32).reshape(1, A_DV), lam_init, bound_a)

    qb, kb, vb = _mla_prep(pb, p, bsz, s, tm=512)
    dummy_lam = jnp.zeros((4, A_DK), F32)
    dummy_hn = jnp.ones((1, A_DV), F32)
    dqk = B_NOPE + B_ROPE
    bound_b = (BOUND_MARGIN * dqk * (dqk ** -0.5 * LOG2E)
               * jnp.max(jnp.abs(p["b_qn"].astype(F32))) * jnp.max(jnp.abs(p["b_kn"].astype(F32))))
    ob = _flash(qb, kb, vb, 0, 0, 0, bsz, s, B_HEADS, 1, False, _alibi_slopes(B_HEADS),
                dummy_lam, dummy_hn, 0.0, bound_b)

    sink2 = p["c_sink"].astype(F32) * LOG2E
    bound_c = jnp.maximum(
        BOUND_MARGIN * C_HD * (C_HD ** -0.5 * LOG2E)
        * jnp.max(jnp.abs(p["c_qn"].astype(F32))) * jnp.max(jnp.abs(p["c_kn"].astype(F32))),
        jnp.max(sink2))
    oc, = _band(pc, lambda sl: sl, lambda sl: 4 + sl // 2, lambda sl: 6 + sl // 2, bsz, s, 1, 8,
                C_WINDOW, _alibi_slopes(C_QH), sink2, True, False, BAND_ROWS, bound_c)

    ods, lses = [], []
    zero_sink = jnp.zeros((D_HEADS,), F32)
    bound_d = (BOUND_MARGIN * D_HD * (D_HD ** -0.5 * LOG2E)
               * jnp.max(jnp.abs(p["d_qn"].astype(F32))) * jnp.max(jnp.abs(p["d_kn"].astype(F32))))
    for g, (win, dil) in enumerate(DIL_PAIRS):
        og, lg = _band((pd0, pd1, pd2)[g], lambda sl: sl, lambda sl: 4 + sl,
                       lambda sl: 8 + sl, bsz, s, dil, 12, win // (2 * dil),
                       _alibi_slopes(D_HEADS), zero_sink, False, True, BAND_ROWS, bound_d)
        ods.append(og)
        lses.append(lg)

    mtiles = [(0, SEG128, 0, 0), (256, SEG128, 0, 256), (512, SEG_NONE, 1, 0), (768, SEG_NONE, 1, 256)]
    mk, mv = _proj(mem.reshape(bsz * N_MEM, d), p["m_norm"].astype(F32), p["w_mem_kv_bf16"], p["gain_mem"],
                   mtiles, [512, 512], tm=256)
    om = _memattn(pm, mk, mv, bsz, s, tq=512)

    y = _merge(x2d, p, p["wz"], p["wg"], oa, ob, oc, om, ods, lses, tm=512)
    return y.reshape(bsz, s, d)


def _prepare_layer(p):
    w_in = p["w_in"]
    w_pad = jnp.concatenate([w_in, jnp.zeros((w_in.shape[0], 1), w_in.dtype)], axis=1)
    return {
        "w_main": jnp.take(w_pad, jnp.asarray(_MAIN_COLS), axis=1).astype(BF16),
        "gain_main": _main_gain_vector(p),
        "wz": w_in[:, OFF_Z:OFF_G].astype(BF16),
        "wg": w_in[:, OFF_G:].astype(BF16),
        "w_mem_kv_bf16": p["w_mem_kv"].astype(BF16),
        "gain_mem": jnp.concatenate([jnp.tile(p["m_kn"].astype(F32), M_HEADS),
                                     jnp.ones((M_HEADS * M_HD,), F32)]).reshape(1, -1),
        "w_br_bf16": p["w_br"].astype(BF16),
        "w_out_bf16": p["w_out"].astype(BF16),
    }


def kernel(x_prompt, x_sample, mem_prompt, mem_sample, norm_g, w_in, a_qn, a_kn, a_lam, a_hn, b_cqn, b_ckvn, w_qb, w_kvb, b_qn, b_kn, c_qn, c_kn, c_sink, d_qn, d_kn, m_norm, w_mem_kv, m_qn, m_kn, b_gate, w_br, w_out):
    depth = norm_g.shape[0]
    y_prompt, y_sample = x_prompt, x_sample
    for l in range(depth):
        p = {
            "norm_g": norm_g[l], "w_in": w_in[l],
            "a_qn": a_qn[l], "a_kn": a_kn[l], "a_lam": a_lam[l], "a_hn": a_hn[l],
            "b_cqn": b_cqn[l], "b_ckvn": b_ckvn[l], "w_qb": w_qb[l], "w_kvb": w_kvb[l],
            "b_qn": b_qn[l], "b_kn": b_kn[l],
            "c_qn": c_qn[l], "c_kn": c_kn[l], "c_sink": c_sink[l],
            "d_qn": d_qn[l], "d_kn": d_kn[l],
            "m_norm": m_norm[l], "w_mem_kv": w_mem_kv[l], "m_qn": m_qn[l], "m_kn": m_kn[l],
            "b_gate": b_gate[l], "w_br": w_br[l], "w_out": w_out[l],
        }
        p.update(_prepare_layer(p))
        y_prompt = _encoder_layer(y_prompt, mem_prompt, l, p)
        y_sample = _encoder_layer(y_sample, mem_sample, l, p)
    return (y_prompt, y_sample)
```

```python
import functools
import math

import numpy as np
import jax
import jax.numpy as jnp
from jax import lax
from jax.experimental import pallas as pl
from jax.experimental.pallas import tpu as pltpu

F32 = jnp.float32
BF16 = jnp.bfloat16

D_MODEL = 1024
N_MEM = 256
BRANCH_W = 512
N_BRANCH = 5
NEG_INF = -1e30
EPS = 1e-6
A_HEADS, A_DK, A_DV = 4, 64, 128
B_HEADS, B_Q_LORA, B_KV_LORA, B_NOPE, B_ROPE, B_DV = 4, 256, 128, 64, 32, 128
ROPE_THETA = 10000.0
C_QH, C_KVH, C_HD, C_WINDOW = 8, 2, 64, 128
D_HEADS, D_HD = 8, 64
DIL_PAIRS = ((128, 1), (512, 4), (2048, 16))
N_DIL = 3
M_HEADS, M_HD = 4, 128

OFF_A_Q = 0
OFF_A_K = OFF_A_Q + 2 * A_HEADS * A_DK
OFF_A_V = OFF_A_K + 2 * A_HEADS * A_DK
OFF_B_CQ = OFF_A_V + A_HEADS * A_DV
OFF_B_CKV = OFF_B_CQ + B_Q_LORA
OFF_B_KR = OFF_B_CKV + B_KV_LORA
OFF_C_Q = OFF_B_KR + B_ROPE
OFF_C_K = OFF_C_Q + C_QH * C_HD
OFF_C_V = OFF_C_K + C_KVH * C_HD
OFF_D_Q = OFF_C_V + C_KVH * C_HD
OFF_D_K = OFF_D_Q + N_DIL * D_HEADS * D_HD
OFF_D_V = OFF_D_K + N_DIL * D_HEADS * D_HD
OFF_M_Q = OFF_D_V + N_DIL * D_HEADS * D_HD
OFF_Z = OFF_M_Q + M_HEADS * M_HD
OFF_G = OFF_Z + N_BRANCH * BRANCH_W
N_IN = OFF_G + N_BRANCH * D_MODEL

LANES = 128
COL_TILE = 256
VMEM_LIMIT = 56 * 1024 * 1024

SEG_NONE, SEG64, SEG128 = -1, 0, 1

LOG2E = 1.4426950408889634
MAX_FIXED_SHIFT = 30.0 * LOG2E
BOUND_MARGIN = 1.02
ZERO_EXP2 = 151.0
BAND_ROWS = 2048


def _cparams(sem):
    return pltpu.CompilerParams(dimension_semantics=sem, vmem_limit_bytes=VMEM_LIMIT)


def _resident(shape, index_map):
    return pl.BlockSpec(shape, index_map, pipeline_mode=pl.Buffered(1))


def _seg_matrix(width, segs):
    m = np.zeros((width, width), np.float32)
    for s, n in segs:
        m[s:s + n, s:s + n] = 1.0 / n
    return m


def _proj_kernel(x_ref, ng_ref, w_ref, gain_ref, mseg_ref, *out_refs, tiles):
    x = x_ref[...]
    ms = jnp.mean(x * x, axis=-1, keepdims=True)
    h = ((x * lax.rsqrt(ms + EPS)) * ng_ref[...]).astype(BF16)
    assert len(tiles) % 2 == 0
    for pair in range(0, len(tiles), 2):
        base = tiles[pair][0]
        assert tiles[pair + 1][0] == base + COL_TILE
        y2 = jnp.dot(h, w_ref[:, base:base + 2 * COL_TILE], preferred_element_type=F32)
        for half, (c0, seg, oi, oc) in enumerate(tiles[pair:pair + 2]):
            y = y2[:, half * COL_TILE:(half + 1) * COL_TILE]
            if seg != SEG_NONE:
                sq = (y * y).astype(BF16)
                segms = jnp.dot(sq, mseg_ref[seg], preferred_element_type=F32)
                y = (y * lax.rsqrt(segms + EPS)) * gain_ref[:, c0:c0 + COL_TILE]
            out_refs[oi][:, oc:oc + COL_TILE] = y.astype(BF16)


def _proj(x2d, norm_gain, w, gain, tiles, out_widths, tm):
    t, d = x2d.shape
    n = w.shape[1]
    tm = min(tm, t)
    mseg = jnp.asarray(
        np.stack([_seg_matrix(COL_TILE, [(s, 64) for s in range(0, COL_TILE, 64)]),
                  _seg_matrix(COL_TILE, [(s, 128) for s in range(0, COL_TILE, 128)])]), BF16)
    return pl.pallas_call(
        functools.partial(_proj_kernel, tiles=tuple(tiles)),
        grid=(t // tm,),
        in_specs=[
            pl.BlockSpec((tm, d), lambda i: (i, 0)),
            _resident((1, d), lambda i: (0, 0)),
            _resident((d, n), lambda i: (0, 0)),
            _resident((1, n), lambda i: (0, 0)),
            _resident((2, COL_TILE, COL_TILE), lambda i: (0, 0, 0)),
        ],
        out_specs=[pl.BlockSpec((tm, wd), lambda i: (i, 0)) for wd in out_widths],
        out_shape=[jax.ShapeDtypeStruct((t, wd), BF16) for wd in out_widths],
        compiler_params=_cparams(("parallel",)),
        name="proj",
    )(x2d, norm_gain.reshape(1, d), w, gain, mseg)


def _main_plan():
    zero = N_IN
    cols, gains, tiles = [], [], []
    widths = [3 * 512, 512, 1024] + [3 * 512] * N_DIL + [512]

    def add(out_idx, out_col, src_cols, seg, gain_key):
        assert len(src_cols) % COL_TILE == 0
        c0 = len(cols)
        cols.extend(src_cols)
        gains.extend([gain_key] * len(src_cols))
        for k in range(len(src_cols) // COL_TILE):
            tiles.append((c0 + k * COL_TILE, seg, out_idx, out_col + k * COL_TILE))

    hm = [m * A_HEADS * A_DK + h * A_DK + d for h in range(A_HEADS) for m in range(2) for d in range(A_DK)]
    add(0, 0, [OFF_A_Q + c for c in hm], SEG64, "a_q")
    add(0, 512, [OFF_A_K + c for c in hm], SEG64, "a_k")
    add(0, 1024, [OFF_A_V + c for c in range(A_HEADS * A_DV)], SEG_NONE, None)
    braw = ([OFF_B_CQ + c for c in range(B_Q_LORA)] + [OFF_B_CKV + c for c in range(B_KV_LORA)]
            + [OFF_B_KR + c for c in range(B_ROPE)] + [zero] * (512 - B_Q_LORA - B_KV_LORA - B_ROPE))
    add(1, 0, braw, SEG_NONE, None)
    add(2, 0, [OFF_C_Q + c for c in range(C_QH * C_HD)], SEG64, "c_q")
    dup = [kv * C_HD + d for kv in range(C_KVH) for _ in range(2) for d in range(C_HD)]
    add(2, 512, [OFF_C_K + c for c in dup], SEG64, "c_k")
    add(2, 768, [OFF_C_V + c for c in dup], SEG_NONE, None)
    for g in range(N_DIL):
        gsl = [g * D_HEADS * D_HD + c for c in range(D_HEADS * D_HD)]
        add(3 + g, 0, [OFF_D_Q + c for c in gsl], SEG64, "d_q")
        add(3 + g, 512, [OFF_D_K + c for c in gsl], SEG64, "d_k")
        add(3 + g, 1024, [OFF_D_V + c for c in gsl], SEG_NONE, None)
    add(3 + N_DIL, 0, [OFF_M_Q + c for c in range(M_HEADS * M_HD)], SEG128, "m_q")
    return np.asarray(cols, np.int32), gains, tiles, widths


_MAIN_COLS, _MAIN_GAINS, _MAIN_TILES, _MAIN_WIDTHS = _main_plan()


def _main_gain_vector(p):
    per_key = {
        "a_q": p["a_qn"] * (A_DK ** -0.5 * LOG2E), "a_k": p["a_kn"],
        "c_q": p["c_qn"] * (C_HD ** -0.5 * LOG2E), "c_k": p["c_kn"],
        "d_q": p["d_qn"] * (D_HD ** -0.5 * LOG2E), "d_k": p["d_kn"],
        "m_q": p["m_qn"] * (M_HD ** -0.5),
    }
    n = len(_MAIN_GAINS)
    pieces, k = [], 0
    while k < n:
        key = _MAIN_GAINS[k]
        k2 = k
        while k2 < n and _MAIN_GAINS[k2] == key:
            k2 += 1
        if key is None:
            pieces.append(jnp.ones((k2 - k,), F32))
        else:
            gvec = per_key[key].astype(F32)
            pieces.append(jnp.tile(gvec, (k2 - k) // gvec.shape[0]))
        k = k2
    return jnp.concatenate(pieces).reshape(1, n)


def _mla_prep_kernel(b_ref, wq_ref, wk_ref, wv_ref, place_ref, mseg_ref, cqg_ref, ckvg_ref,
                     qg_ref, kg_ref, ct_ref, s1_ref, s2_ref, q_out, k_out, v_out):
    braw = b_ref[...]
    cq = braw[:, :B_Q_LORA].astype(F32)
    cqn = ((cq * lax.rsqrt(jnp.mean(cq * cq, axis=-1, keepdims=True) + EPS)) * cqg_ref[...]).astype(BF16)
    ckv = braw[:, B_Q_LORA:B_Q_LORA + B_KV_LORA].astype(F32)
    ckvn = ((ckv * lax.rsqrt(jnp.mean(ckv * ckv, axis=-1, keepdims=True) + EPS)) * ckvg_ref[...]).astype(BF16)
    qb = jnp.dot(cqn, wq_ref[...], preferred_element_type=F32)
    kb = jnp.dot(ckvn, wk_ref[...], preferred_element_type=F32)
    kb = kb + jnp.dot(braw[:, B_Q_LORA + B_KV_LORA:], place_ref[...], preferred_element_type=F32)
    v_out[...] = jnp.dot(ckvn, wv_ref[...], preferred_element_type=F32).astype(BF16)
    ct, s1, s2 = ct_ref[...], s1_ref[...], s2_ref[...]

    def finish(t, gain_ref, out):
        for h in range(B_HEADS):
            y = t[:, h * LANES:(h + 1) * LANES]
            segms = jnp.dot((y * y).astype(BF16), mseg_ref[...], preferred_element_type=F32)
            yn = (y * lax.rsqrt(segms + EPS)) * gain_ref[...]
            half = B_ROPE // 2
            r = yn * ct + pltpu.roll(yn, half, 1) * s1 + pltpu.roll(yn, LANES - half, 1) * s2
            out[:, h * LANES:(h + 1) * LANES] = r.astype(BF16)

    finish(qb, qg_ref, q_out)
    finish(kb, kg_ref, k_out)


def _rope_tables(s):
    half = B_ROPE // 2
    inv = ROPE_THETA ** (-jnp.arange(half, dtype=F32) / half)
    ang = jnp.arange(s, dtype=F32)[:, None] * inv[None, :]
    cos, sin = jnp.cos(ang), jnp.sin(ang)
    one = jnp.ones((s, B_NOPE), F32)
    zpad = jnp.zeros((s, LANES - B_NOPE - B_ROPE), F32)
    z64 = jnp.zeros((s, B_NOPE), F32)
    zh = jnp.zeros((s, half), F32)
    ct = jnp.concatenate([one, cos, cos, zpad], axis=1)
    s1 = jnp.concatenate([z64, zh, sin, zpad], axis=1)
    s2 = jnp.concatenate([z64, -sin, zh, zpad], axis=1)
    return ct, s1, s2


def _mla_prep(braw, p, bsz, s, tm):
    t = braw.shape[0]
    tm = min(tm, s)
    pad = LANES - B_NOPE - B_ROPE
    wq = p["w_qb"].reshape(B_Q_LORA, B_HEADS, B_NOPE + B_ROPE)
    wq = jnp.pad(wq, ((0, 0), (0, 0), (0, pad))).reshape(B_Q_LORA, B_HEADS * LANES).astype(BF16)
    wkv = p["w_kvb"].reshape(B_KV_LORA, B_HEADS, B_NOPE + B_DV)
    wk = jnp.pad(wkv[:, :, :B_NOPE], ((0, 0), (0, 0), (0, LANES - B_NOPE))).reshape(B_KV_LORA, B_HEADS * LANES).astype(BF16)
    wv = wkv[:, :, B_NOPE:].reshape(B_KV_LORA, B_HEADS * B_DV).astype(BF16)
    place = np.zeros((LANES, B_HEADS * LANES), np.float32)
    for h in range(B_HEADS):
        for d in range(B_ROPE):
            place[d, h * LANES + B_NOPE + d] = 1.0
    mseg = _seg_matrix(LANES, [(0, B_NOPE), (B_NOPE, B_ROPE)])
    scale = (B_NOPE + B_ROPE) ** -0.5 * LOG2E
    zp = jnp.zeros((pad,), F32)
    qg = jnp.concatenate([p["b_qn"].astype(F32) * scale, zp]).reshape(1, LANES)
    kg = jnp.concatenate([p["b_kn"].astype(F32), zp]).reshape(1, LANES)
    ct, s1, s2 = _rope_tables(s)
    nst = s // tm
    full = lambda shape: _resident(shape, lambda b, i: tuple(0 for _ in shape))
    row = lambda wd: pl.BlockSpec((tm, wd), lambda b, i: (b * nst + i, 0))
    tab = pl.BlockSpec((tm, LANES), lambda b, i: (i, 0))
    return pl.pallas_call(
        _mla_prep_kernel,
        grid=(bsz, nst),
        in_specs=[row(512), full((B_Q_LORA, 512)), full((B_KV_LORA, 512)), full((B_KV_LORA, 512)),
                  full((LANES, 512)), full((LANES, LANES)), full((1, B_Q_LORA)), full((1, B_KV_LORA)),
                  full((1, LANES)), full((1, LANES)), tab, tab, tab],
        out_specs=[row(512), row(512), row(512)],
        out_shape=[jax.ShapeDtypeStruct((t, 512), BF16)] * 3,
        compiler_params=_cparams(("parallel", "parallel")),
        name="mla_prep",
    )(braw, wq, wk, wv, jnp.asarray(place, BF16), jnp.asarray(mseg, BF16),
      p["b_cqn"].astype(F32).reshape(1, -1), p["b_ckvn"].astype(F32).reshape(1, -1), qg, kg, ct, s1, s2)


def _flash_kernel(slopes_ref, lam_ref, hn_ref, q_ref, k_ref, v_ref, o_ref, m_sc, l_sc, acc_sc,
                  *, nmaps, alibi, tq, tk, seq, lam_init):
    h = pl.program_id(1)
    q0 = pl.program_id(2) * tq
    q = q_ref[...]
    if nmaps == 2:
        lane = lax.broadcasted_iota(jnp.int32, q.shape, 1)
        qs = [jnp.where(lane < A_DK, q, jnp.zeros_like(q)), jnp.where(lane >= A_DK, q, jnp.zeros_like(q))]
    else:
        qs = [q]
    m_sc[...] = jnp.full(m_sc.shape, NEG_INF, F32)
    l_sc[...] = jnp.zeros(l_sc.shape, F32)
    acc_sc[...] = jnp.zeros(acc_sc.shape, F32)
    if alibi:
        slope2 = slopes_ref[h] * LOG2E
        dmat = (lax.broadcasted_iota(jnp.int32, (tq, tk), 0)
                - lax.broadcasted_iota(jnp.int32, (tq, tk), 1)).astype(F32)

    def body(j, carry):
        k0 = pl.multiple_of(j * tk, tk)
        k = k_ref[pl.ds(k0, tk), :]
        v = v_ref[pl.ds(k0, tk), :]
        if alibi:
            bias = -slope2 * jnp.abs(dmat + (q0 - k0).astype(F32))
        for m in range(nmaps):
            s = lax.dot_general(qs[m], k, (((1,), (1,)), ((), ())), preferred_element_type=F32)
            if alibi:
                s = s + bias
            m_old = m_sc[m]
            m_new = jnp.maximum(m_old, jnp.max(s, axis=1, keepdims=True))
            alpha = jnp.exp2(m_old - m_new)
            pr = jnp.exp2(s - m_new)
            l_sc[m] = alpha * l_sc[m] + jnp.sum(pr, axis=1, keepdims=True)
            acc_sc[m] = alpha * acc_sc[m] + jnp.dot(pr.astype(BF16), v, preferred_element_type=F32)
            m_sc[m] = m_new
        return carry

    lax.fori_loop(0, seq // tk, body, 0)
    outs = [acc_sc[m] / l_sc[m] for m in range(nmaps)]
    o_ref[...] = _flash_epilogue(outs, lam_ref, hn_ref, lam_init).astype(BF16)


def _flash_epilogue(outs, lam_ref, hn_ref, lam_init):
    if len(outs) == 1:
        return outs[0]
    lf = lam_ref[...]
    e1 = jnp.exp(jnp.sum(lf[0:1] * lf[1:2], axis=1, keepdims=True))
    e2 = jnp.exp(jnp.sum(lf[2:3] * lf[3:4], axis=1, keepdims=True))
    lam = e1 - e2 + lam_init
    o = outs[0] - lam * outs[1]
    o = (o * lax.rsqrt(jnp.mean(o * o, axis=-1, keepdims=True) + EPS)) * hn_ref[...]
    return o * (1.0 - lam_init)


def _flash_fixed_kernel(shift_ref, slopes_ref, lam_ref, hn_ref, q_ref, k_ref, v_ref, o_ref, v1_sc, acc_sc,
                        *, nmaps, alibi, tq, tk, seq, lam_init, slopes_static):
    h = pl.program_id(1)
    i = pl.program_id(2)
    q0 = i * tq

    @pl.when(i == 0)
    def _():
        v1_sc[:, :LANES] = v_ref[...]
        v1_sc[:, LANES:] = jnp.ones((seq, LANES), BF16)

    q = q_ref[...]
    if nmaps == 2:
        lane = lax.broadcasted_iota(jnp.int32, q.shape, 1)
        qs = [jnp.where(lane < A_DK, q, jnp.zeros_like(q)), jnp.where(lane >= A_DK, q, jnp.zeros_like(q))]
    else:
        qs = [q]
    acc_sc[...] = jnp.zeros(acc_sc.shape, F32)
    shift = shift_ref[0]
    if alibi:
        slope2 = slopes_ref[h] * LOG2E
        t0 = (lax.broadcasted_iota(jnp.int32, (tq, tk), 0)
              - lax.broadcasted_iota(jnp.int32, (tq, tk), 1)).astype(F32) * slope2

    def step(j):
        k0 = j * tk if isinstance(j, int) else pl.multiple_of(j * tk, tk)
        k = k_ref[pl.ds(k0, tk), :]
        v1 = v1_sc[pl.ds(k0, tk), :]
        if alibi:
            sub = jnp.abs(t0 + slope2 * (q0 - k0).astype(F32)) + shift
        ps = []
        for m in range(nmaps):
            s = lax.dot_general(qs[m], k, (((1,), (1,)), ((), ())), preferred_element_type=F32)
            ps.append(jnp.exp2(s - sub if alibi else s - shift).astype(BF16))
        pr = ps[0] if nmaps == 1 else jnp.concatenate(ps, axis=0)
        acc_sc[...] += jnp.dot(pr, v1, preferred_element_type=F32)

    nkv = seq // tk
    if alibi:
        assert tq == tk
        for hd, slope in enumerate(slopes_static):
            reach = int(math.floor((ZERO_EXP2 / (slope * LOG2E) - 1.0) / tk)) + 1
            cnt = min(nkv, 2 * reach + 1)

            @pl.when(h == hd)
            def _(reach=reach, cnt=cnt):
                start = jnp.clip(i - reach, 0, nkv - cnt) if cnt < nkv else 0
                for jj in range(cnt):
                    step(start + jj)
    else:
        for jj in range(nkv):
            step(jj)
    outs = [acc_sc[m * tq:(m + 1) * tq, :LANES] / acc_sc[m * tq:(m + 1) * tq, LANES:] for m in range(nmaps)]
    o_ref[...] = _flash_epilogue(outs, lam_ref, hn_ref, lam_init).astype(BF16)


def _flash(qa, ka, va, qcb, kcb, vcb, bsz, s, heads, nmaps, alibi, slopes, lam, hn, lam_init, bound2):
    t = qa.shape[0]
    name = "flash_diff" if nmaps == 2 else "flash_mla"
    shift = bound2.reshape(1).astype(F32)

    def call(kern, tq, tk, scratch, extra_in, extra_args, suffix):
        tq_, tk_ = min(tq, s), min(tk, s)
        nq = s // tq_
        return pl.pallas_call(
            functools.partial(kern, nmaps=nmaps, alibi=alibi, tq=tq_, tk=tk_, seq=s, lam_init=lam_init),
            grid=(bsz, heads, nq),
            in_specs=extra_in + [
                pl.BlockSpec(memory_space=pltpu.SMEM),
                _resident(lam.shape, lambda b, h, i: (0, 0)),
                _resident(hn.shape, lambda b, h, i: (0, 0)),
                pl.BlockSpec((tq_, LANES), lambda b, h, i: (b * nq + i, qcb + h)),
                pl.BlockSpec((s, LANES), lambda b, h, i: (b, kcb + h)),
                pl.BlockSpec((s, LANES), lambda b, h, i: (b, vcb + h)),
            ],
            out_specs=pl.BlockSpec((tq_, LANES), lambda b, h, i: (b * nq + i, h)),
            out_shape=jax.ShapeDtypeStruct((t, heads * LANES), BF16),
            scratch_shapes=scratch(tq_),
            compiler_params=_cparams(("parallel", "parallel", "arbitrary")),
            name=name + suffix,
        )(*extra_args, slopes, lam, hn, qa, ka, va)

    def fixed():
        return call(functools.partial(_flash_fixed_kernel, slopes_static=_alibi_slopes_static(heads)), 512, 512,
                    lambda tq_: [pltpu.VMEM((s, 2 * LANES), BF16), pltpu.VMEM((nmaps * tq_, 2 * LANES), F32)],
                    [pl.BlockSpec(memory_space=pltpu.SMEM)], [shift], "_fixed")

    def online():
        return call(_flash_kernel, 256, 512,
                    lambda tq_: [pltpu.VMEM((nmaps, tq_, 1), F32), pltpu.VMEM((nmaps, tq_, 1), F32),
                                 pltpu.VMEM((nmaps, tq_, LANES), F32)],
                    [], [], "_online")

    return lax.cond(bound2 <= MAX_FIXED_SHIFT, fixed, online)


def _band_kernel(slopes_ref, sink_ref, q_ref, k_ref, v_ref, *outs, window, qb, kw, ut, u_len, dil, use_sink, emit_lse):
    o_ref = outs[0]
    slab = pl.program_id(0)
    ubase = pl.program_id(3) * ut
    lane = lax.broadcasted_iota(jnp.int32, (kw, LANES), 1)
    qlane = lax.broadcasted_iota(jnp.int32, (qb, LANES), 1)
    dmat = (lax.broadcasted_iota(jnp.int32, (qb, kw), 0) - lax.broadcasted_iota(jnp.int32, (qb, kw), 1))
    for sb in range(ut // qb):
        u0 = ubase + sb * qb
        start = pl.multiple_of(jnp.clip(u0 - window, 0, u_len - kw), 16)
        q = q_ref[sb * qb:(sb + 1) * qb, :]
        k = k_ref[pl.ds(start, kw), :]
        v = v_ref[pl.ds(start, kw), :]
        rel = dmat + (u0 - start)
        valid = jnp.abs(rel) <= window
        dist = jnp.abs(rel).astype(F32) * float(dil)
        o_acc = jnp.zeros((qb, LANES), F32)
        lse_acc = jnp.zeros((qb, LANES), F32)
        for hh in range(2):
            head = slab * 2 + hh
            sel_q = (qlane < 64) if hh == 0 else (qlane >= 64)
            sel_v = (lane < 64) if hh == 0 else (lane >= 64)
            qh = jnp.where(sel_q, q, jnp.zeros_like(q))
            vh = jnp.where(sel_v, v, jnp.zeros_like(v))
            s = lax.dot_general(qh, k, (((1,), (1,)), ((), ())), preferred_element_type=F32)
            logits = jnp.where(valid, s - (slopes_ref[head] * LOG2E) * dist, NEG_INF)
            mx = jnp.max(logits, axis=1, keepdims=True)
            if use_sink:
                sk = sink_ref[head]
                mx = jnp.maximum(mx, sk)
            e = jnp.exp2(logits - mx)
            den = jnp.sum(e, axis=1, keepdims=True)
            if use_sink:
                den = den + jnp.exp2(sk - mx)
            pv = jnp.dot(e.astype(BF16), vh, preferred_element_type=F32)
            o_acc = o_acc + pv / den
            if emit_lse:
                lse_acc = lse_acc + jnp.where(sel_q, mx + jnp.log2(den), 0.0)
        o_ref[sb * qb:(sb + 1) * qb, :] = o_acc.astype(BF16)
        if emit_lse:
            outs[1][sb * qb:(sb + 1) * qb, :] = lse_acc


def _band_fixed_kernel(shift_ref, slopes_ref, sink_ref, q_ref, k_ref, v_ref, *outs,
                       window, qb, kw, ut, u_len, dil, use_sink, emit_lse):
    o_ref, sub_sc = outs[0], outs[-1]
    slab = pl.program_id(0)
    ubase = pl.program_id(3) * ut
    shift = shift_ref[0]
    qlane = lax.broadcasted_iota(jnp.int32, (qb, LANES), 1)
    first = qlane < 64
    ones = jnp.ones((kw, LANES), BF16)

    @pl.when((pl.program_id(1) == 0) & (pl.program_id(2) == 0) & (pl.program_id(3) == 0))
    def _():
        dmat =(lax.broadcasted_iota(jnp.int32, (qb, kw), 0) - lax.broadcasted_iota(jnp.int32, (qb, kw), 1))
        for case in range(3):
            absrel = jnp.abs(dmat + case * window)
            for hh in range(2):
                slope2 = slopes_ref[slab * 2 + hh] * (LOG2E * dil)
                sub_sc[case, hh] = jnp.where(absrel <= window, absrel.astype(F32) * slope2 + shift, -NEG_INF)

    for sb in range(ut // qb):
        u0 = ubase + sb * qb
        start = pl.multiple_of(jnp.clip(u0 - window, 0, u_len - kw), 16)
        case = (u0 - start) // window
        q = q_ref[sb * qb:(sb + 1) * qb, :]
        k = k_ref[pl.ds(start, kw), :]
        v1 = jnp.concatenate([v_ref[pl.ds(start, kw), :], ones], axis=1)
        ps = []
        for hh in range(2):
            qh = jnp.where(first if hh == 0 else jnp.logical_not(first), q, jnp.zeros_like(q))
            s = lax.dot_general(qh, k, (((1,), (1,)), ((), ())), preferred_element_type=F32)
            ps.append(jnp.exp2(s - sub_sc[case, hh]).astype(BF16))
        r = jnp.dot(jnp.concatenate(ps, axis=0), v1, preferred_element_type=F32)
        za, zb = r[:qb, LANES:], r[qb:, LANES:]
        if use_sink:
            za = za + jnp.exp2(jnp.full((1, 1), sink_ref[slab * 2] - shift, F32))
            zb = zb + jnp.exp2(jnp.full((1, 1), sink_ref[slab * 2 + 1] - shift, F32))
        o_ref[sb * qb:(sb + 1) * qb, :] = jnp.where(first, r[:qb, :LANES] / za, r[qb:, :LANES] / zb).astype(BF16)
        if emit_lse:
            outs[1][sb * qb:(sb + 1) * qb, :] = jnp.where(first, jnp.log2(za), jnp.log2(zb)) + shift


def _band(src, qcb, kcb, vcb, bsz, s, dil, ncb_src, window, slopes, sink, use_sink, emit_lse, ut, bound2):
    t, c = src.shape
    u_len = s // dil
    qb = min(LANES, u_len)
    kw = min(qb + 2 * window, u_len)
    ut = min(ut, u_len)
    nu = u_len // ut
    srcv = src.reshape(bsz * u_len, dil * c)
    nslab = 4
    out_shapes = [jax.ShapeDtypeStruct((bsz * u_len, dil * 512), BF16)]
    out_specs = [pl.BlockSpec((ut, LANES), lambda sl, b, r, i: (b * nu + i, r * nslab + sl))]
    if emit_lse:
        out_shapes.append(jax.ShapeDtypeStruct((bsz * u_len, dil * 512), F32))
        out_specs.append(pl.BlockSpec((ut, LANES), lambda sl, b, r, i: (b * nu + i, r * nslab + sl)))
    smem = pl.BlockSpec(memory_space=pltpu.SMEM)

    def call(kern, extra_in, extra_args, suffix, scratch=()):
        return pl.pallas_call(
            functools.partial(kern, window=window, qb=qb, kw=kw, ut=ut, u_len=u_len, dil=dil,
                              use_sink=use_sink, emit_lse=emit_lse),
            grid=(nslab, bsz, dil, nu),
            scratch_shapes=list(scratch),
            in_specs=extra_in + [
                smem, smem,
                pl.BlockSpec((ut, LANES), lambda sl, b, r, i: (b * nu + i, r * ncb_src + qcb(sl))),
                pl.BlockSpec((u_len, LANES), lambda sl, b, r, i: (b, r * ncb_src + kcb(sl))),
                pl.BlockSpec((u_len, LANES), lambda sl, b, r, i: (b, r * ncb_src + vcb(sl))),
            ],
            out_specs=out_specs,
            out_shape=out_shapes,
            compiler_params=_cparams(("arbitrary", "arbitrary", "arbitrary", "arbitrary")),
            name="band_w%d_d%d%s" % (window, dil, suffix),
        )(*extra_args, slopes, sink, srcv, srcv, srcv)

    res = lax.cond(bound2 <= MAX_FIXED_SHIFT,
                   lambda: call(_band_fixed_kernel, [smem], [bound2.reshape(1).astype(F32)], "_fixed",
                                [pltpu.VMEM((3, 2, qb, kw), F32)]),
                   lambda: call(_band_kernel, [], [], "_online"))
    return [r.reshape(t, 512) for r in res]


def _memattn_kernel(q_ref, k_ref, v_ref, o_ref):
    for h in range(M_HEADS):
        sl = slice(h * M_HD, (h + 1) * M_HD)
        s = lax.dot_general(q_ref[:, sl], k_ref[:, sl], (((1,), (1,)), ((), ())), preferred_element_type=F32)
        mx = jnp.max(s, axis=1, keepdims=True)
        e = jnp.exp(s - mx)
        den = jnp.sum(e, axis=1, keepdims=True)
        pv = jnp.dot(e.astype(BF16), v_ref[:, sl], preferred_element_type=F32)
        o_ref[:, sl] = (pv / den).astype(BF16)


def _memattn(qm, mk, mv, bsz, s, tq):
    t = qm.shape[0]
    tq = min(tq, s)
    nq = s // tq
    wd = M_HEADS * M_HD
    return pl.pallas_call(
        _memattn_kernel,
        grid=(bsz, nq),
        in_specs=[pl.BlockSpec((tq, wd), lambda b, i: (b * nq + i, 0)),
                  pl.BlockSpec((N_MEM, wd), lambda b, i: (b, 0)),
                  pl.BlockSpec((N_MEM, wd), lambda b, i: (b, 0))],
        out_specs=pl.BlockSpec((tq, wd), lambda b, i: (b * nq + i, 0)),
        out_shape=jax.ShapeDtypeStruct((t, wd), BF16),
        compiler_params=_cparams(("parallel", "parallel")),
        name="memattn",
    )(qm, mk, mv)


def _merge_kernel(x_ref, ng_ref, wz_ref, wg_ref, bg_ref, wbr_ref, wout_ref,
                  oa_ref, ob_ref, oc_ref, om_ref, od0_ref, od1_ref, od2_ref, l0_ref, l1_ref, l2_ref,
                  y_ref, h_sc, acc_sc, o_sc):
    j = pl.program_id(1)

    @pl.when(j == 0)
    def _():
        x = x_ref[...]
        ms = jnp.mean(x * x, axis=-1, keepdims=True)
        h_sc[...] = ((x * lax.rsqrt(ms + EPS)) * ng_ref[...]).astype(BF16)
        acc_sc[...] = jnp.zeros(acc_sc.shape, F32)
        o_sc[...] = oa_ref[...].astype(F32)

    @pl.when(j == 1)
    def _():
        o_sc[...] = ob_ref[...].astype(F32)

    @pl.when(j == 2)
    def _():
        o_sc[...] = oc_ref[...].astype(F32)

    @pl.when(j == 3)
    def _():
        l0, l1, l2 = l0_ref[...], l1_ref[...], l2_ref[...]
        mx = jnp.maximum(jnp.maximum(l0, l1), l2)
        e0, e1, e2 = jnp.exp2(l0 - mx), jnp.exp2(l1 - mx), jnp.exp2(l2 - mx)
        den = e0 + e1 + e2
        o_sc[...] = ((e0 / den) * od0_ref[...].astype(F32) + (e1 / den) * od1_ref[...].astype(F32)
                     + (e2 / den) * od2_ref[...].astype(F32))

    @pl.when(j == 4)
    def _():
        o_sc[...] = om_ref[...].astype(F32)

    h = h_sc[...]
    z = jnp.dot(h, wz_ref[...], preferred_element_type=F32)
    g = jnp.dot(h, wg_ref[...], preferred_element_type=F32) + bg_ref[0]
    u = (o_sc[...] * (z / (1.0 + jnp.exp(-z)))).astype(BF16)
    tbr = jnp.dot(u, wbr_ref[0], preferred_element_type=F32)
    acc_sc[...] += tbr / (1.0 + jnp.exp(-g))

    @pl.when(j == N_BRANCH - 1)
    def _():
        y_ref[...] = x_ref[...] + jnp.dot(acc_sc[...].astype(BF16), wout_ref[...], preferred_element_type=F32)


def _merge(x2d, p, wz, wg, oa, ob, oc, om, ods, lses, tm):
    t, d = x2d.shape
    tm = min(tm, t)
    rowf = lambda wd: pl.BlockSpec((tm, wd), lambda i, j: (i, 0))
    return pl.pallas_call(
        _merge_kernel,
        grid=(t // tm, N_BRANCH),
        in_specs=[
            rowf(d),
            _resident((1, d), lambda i, j: (0, 0)),
            pl.BlockSpec((d, BRANCH_W), lambda i, j: (0, j)),
            pl.BlockSpec((d, d), lambda i, j: (0, j)),
            pl.BlockSpec((1, 1, d), lambda i, j: (j, 0, 0)),
            pl.BlockSpec((1, BRANCH_W, d), lambda i, j: (j, 0, 0)),
            _resident((d, d), lambda i, j: (0, 0)),
        ] + [rowf(BRANCH_W)] * 10,
        out_specs=rowf(d),
        out_shape=jax.ShapeDtypeStruct((t, d), F32),
        scratch_shapes=[pltpu.VMEM((tm, d), BF16), pltpu.VMEM((tm, d), F32), pltpu.VMEM((tm, BRANCH_W), F32)],
        compiler_params=_cparams(("parallel", "arbitrary")),
        name="merge",
    )(x2d, p["norm_g"].astype(F32).reshape(1, d), wz, wg, p["b_gate"].astype(F32).reshape(N_BRANCH, 1, d),
      p["w_br_bf16"], p["w_out_bf16"], oa, ob, oc, om, *ods, *lses)


def _alibi_slopes_static(n):
    return tuple(float(v) for v in 2.0 ** (-8.0 * np.arange(1, n + 1, dtype=np.float64) / n))


def _alibi_slopes(n):
    return jnp.asarray(_alibi_slopes_static(n), F32)


def _encoder_layer(x, mem, layer_idx, p):
    bsz, s, d = x.shape
    t = bsz * s
    x2d = x.reshape(t, d)

    pa, pb, pc, pd0, pd1, pd2, pm = _proj(x2d, p["norm_g"].astype(F32), p["w_main"], p["gain_main"],
                                          _MAIN_TILES, _MAIN_WIDTHS, tm=512)

    lam_init = 0.8 - 0.6 * math.exp(-0.3 * layer_idx)
    bound_a = (BOUND_MARGIN * A_DK * (A_DK ** -0.5 * LOG2E)
               * jnp.max(jnp.abs(p["a_qn"].astype(F32))) * jnp.max(jnp.abs(p["a_kn"].astype(F32))))
    oa = _flash(pa, pa, pa, 0, 4, 8, bsz, s, A_HEADS, 2, True, _alibi_slopes(A_HEADS),
                p["a_lam"].astype(F32), p["a_hn"].astype(F32).reshape(1, A_DV), lam_init, bound_a)

    qb, kb, vb = _mla_prep(pb, p, bsz, s, tm=512)
    dummy_lam = jnp.zeros((4, A_DK), F32)
    dummy_hn = jnp.ones((1, A_DV), F32)
    dqk = B_NOPE + B_ROPE
    bound_b = (BOUND_MARGIN * dqk * (dqk ** -0.5 * LOG2E)
               * jnp.max(jnp.abs(p["b_qn"].astype(F32))) * jnp.max(jnp.abs(p["b_kn"].astype(F32))))
    ob = _flash(qb, kb, vb, 0, 0, 0, bsz, s, B_HEADS, 1, False, _alibi_slopes(B_HEADS),
                dummy_lam, dummy_hn, 0.0, bound_b)

    sink2 = p["c_sink"].astype(F32) * LOG2E
    bound_c = jnp.maximum(
        BOUND_MARGIN * C_HD * (C_HD ** -0.5 * LOG2E)
        * jnp.max(jnp.abs(p["c_qn"].astype(F32))) * jnp.max(jnp.abs(p["c_kn"].astype(F32))),
        jnp.max(sink2))
    oc, = _band(pc, lambda sl: sl, lambda sl: 4 + sl // 2, lambda sl: 6 + sl // 2, bsz, s, 1, 8,
                C_WINDOW, _alibi_slopes(C_QH), sink2, True, False, BAND_ROWS, bound_c)

    ods, lses = [], []
    zero_sink = jnp.zeros((D_HEADS,), F32)
    bound_d = (BOUND_MARGIN * D_HD * (D_HD ** -0.5 * LOG2E)
               * jnp.max(jnp.abs(p["d_qn"].astype(F32))) * jnp.max(jnp.abs(p["d_kn"].astype(F32))))
    for g, (win, dil) in enumerate(DIL_PAIRS):
        og, lg = _band((pd0, pd1, pd2)[g], lambda sl: sl, lambda sl: 4 + sl,
                       lambda sl: 8 + sl, bsz, s, dil, 12, win // (2 * dil),
                       _alibi_slopes(D_HEADS), zero_sink, False, True, BAND_ROWS, bound_d)
        ods.append(og)
        lses.append(lg)

    mtiles = [(0, SEG128, 0, 0), (256, SEG128, 0, 256), (512, SEG_NONE, 1, 0), (768, SEG_NONE, 1, 256)]
    mk, mv = _proj(mem.reshape(bsz * N_MEM, d), p["m_norm"].astype(F32), p["w_mem_kv_bf16"], p["gain_mem"],
                   mtiles, [512, 512], tm=256)
    om = _memattn(pm, mk, mv, bsz, s, tq=512)

    y = _merge(x2d, p, p["wz"], p["wg"], oa, ob, oc, om, ods, lses, tm=512)
    return y.reshape(bsz, s, d)


def _prepare_layer(p):
    w_in = p["w_in"]
    pieces, k, n = [], 0, len(_MAIN_COLS)
    while k < n:
        k2 = k + 1
        if _MAIN_COLS[k] == N_IN:
            while k2 < n and _MAIN_COLS[k2] == N_IN:
                k2 += 1
            pieces.append(jnp.zeros((w_in.shape[0], k2 - k), BF16))
        else:
            while k2 < n and _MAIN_COLS[k2] == _MAIN_COLS[k2 - 1] + 1:
                k2 += 1
            pieces.append(w_in[:, int(_MAIN_COLS[k]):int(_MAIN_COLS[k]) + (k2 - k)].astype(BF16))
        k = k2
    return {
        "w_main": jnp.concatenate(pieces, axis=1),
        "gain_main": _main_gain_vector(p),
        "wz": w_in[:, OFF_Z:OFF_G].astype(BF16),
        "wg": w_in[:, OFF_G:].astype(BF16),
        "w_mem_kv_bf16": p["w_mem_kv"].astype(BF16),
        "gain_mem": jnp.concatenate([jnp.tile(p["m_kn"].astype(F32), M_HEADS),
                                     jnp.ones((M_HEADS * M_HD,), F32)]).reshape(1, -1),
        "w_br_bf16": p["w_br"].astype(BF16),
        "w_out_bf16": p["w_out"].astype(BF16),
    }


def kernel(x_prompt, x_sample, mem_prompt, mem_sample, norm_g, w_in, a_qn, a_kn, a_lam, a_hn, b_cqn, b_ckvn, w_qb, w_kvb, b_qn, b_kn, c_qn, c_kn, c_sink, d_qn, d_kn, m_norm, w_mem_kv, m_qn, m_kn, b_gate, w_br, w_out):
    depth = norm_g.shape[0]
    y_prompt, y_sample = x_prompt, x_sample
    for l in range(depth):
        p = {
            "norm_g": norm_g[l], "w_in": w_in[l],
            "a_qn": a_qn[l], "a_kn": a_kn[l], "a_lam": a_lam[l], "a_hn": a_hn[l],
            "b_cqn": b_cqn[l], "b_ckvn": b_ckvn[l], "w_qb": w_qb[l], "w_kvb": w_kvb[l],
            "b_qn": b_qn[l], "b_kn": b_kn[l],
            "c_qn": c_qn[l], "c_kn": c_kn[l], "c_sink": c_sink[l],
            "d_qn": d_qn[l], "d_kn": d_kn[l],
            "m_norm": m_norm[l], "w_mem_kv": w_mem_kv[l], "m_qn": m_qn[l], "m_kn": m_kn[l],
            "b_gate": b_gate[l], "w_br": w_br[l], "w_out": w_out[l],
        }
        p.update(_prepare_layer(p))
        y_prompt = _encoder_layer(y_prompt, mem_prompt, l, p)
        y_sample = _encoder_layer(y_sample, mem_sample, l, p)
    return (y_prompt, y_sample)
```

```python
import functools
import math

import numpy as np
import jax
import jax.numpy as jnp
from jax import lax
from jax.experimental import pallas as pl
from jax.experimental.pallas import tpu as pltpu

F32 = jnp.float32
BF16 = jnp.bfloat16

D_MODEL = 1024
N_MEM = 256
BRANCH_W = 512
N_BRANCH = 5
NEG_INF = -1e30
EPS = 1e-6
A_HEADS, A_DK, A_DV = 4, 64, 128
B_HEADS, B_Q_LORA, B_KV_LORA, B_NOPE, B_ROPE, B_DV = 4, 256, 128, 64, 32, 128
ROPE_THETA = 10000.0
C_QH, C_KVH, C_HD, C_WINDOW = 8, 2, 64, 128
D_HEADS, D_HD = 8, 64
DIL_PAIRS = ((128, 1), (512, 4), (2048, 16))
N_DIL = 3
M_HEADS, M_HD = 4, 128

OFF_A_Q = 0
OFF_A_K = OFF_A_Q + 2 * A_HEADS * A_DK
OFF_A_V = OFF_A_K + 2 * A_HEADS * A_DK
OFF_B_CQ = OFF_A_V + A_HEADS * A_DV
OFF_B_CKV = OFF_B_CQ + B_Q_LORA
OFF_B_KR = OFF_B_CKV + B_KV_LORA
OFF_C_Q = OFF_B_KR + B_ROPE
OFF_C_K = OFF_C_Q + C_QH * C_HD
OFF_C_V = OFF_C_K + C_KVH * C_HD
OFF_D_Q = OFF_C_V + C_KVH * C_HD
OFF_D_K = OFF_D_Q + N_DIL * D_HEADS * D_HD
OFF_D_V = OFF_D_K + N_DIL * D_HEADS * D_HD
OFF_M_Q = OFF_D_V + N_DIL * D_HEADS * D_HD
OFF_Z = OFF_M_Q + M_HEADS * M_HD
OFF_G = OFF_Z + N_BRANCH * BRANCH_W
N_IN = OFF_G + N_BRANCH * D_MODEL

LANES = 128
COL_TILE = 256
VMEM_LIMIT = 56 * 1024 * 1024

SEG_NONE, SEG64, SEG128 = -1, 0, 1

LOG2E = 1.4426950408889634
MAX_FIXED_SHIFT = 30.0 * LOG2E
BOUND_MARGIN = 1.02
PROJ_STAGE_SLABS = 4
ZERO_EXP2 = 151.0
BAND_ROWS = 2048


def _cparams(sem):
    return pltpu.CompilerParams(dimension_semantics=sem, vmem_limit_bytes=VMEM_LIMIT)


def _resident(shape, index_map):
    return pl.BlockSpec(shape, index_map, pipeline_mode=pl.Buffered(1))


def _seg_matrix(width, segs):
    m = np.zeros((width, width), np.float32)
    for s, n in segs:
        m[s:s + n, s:s + n] = 1.0 / n
    return m


def _proj_kernel(x_ref, ng_ref, w_ref, gain_ref, mseg_ref, *refs, tiles, out_widths, out_dils):
    out_refs = refs[:len(out_widths)]
    ybuf = refs[len(out_widths)] if max(out_dils) > 1 else None
    nbuf_used = 0
    x = x_ref[...]
    ms = jnp.mean(x * x, axis=-1, keepdims=True)
    h = ((x * lax.rsqrt(ms + EPS)) * ng_ref[...]).astype(BF16)
    assert len(tiles) % 2 == 0
    for pair in range(0, len(tiles), 2):
        base = tiles[pair][0]
        assert tiles[pair + 1][0] == base + COL_TILE
        y2 = jnp.dot(h, w_ref[:, base:base + 2 * COL_TILE], preferred_element_type=F32)
        for half, (c0, seg, oi, oc) in enumerate(tiles[pair:pair + 2]):
            y = y2[:, half * COL_TILE:(half + 1) * COL_TILE]
            if seg != SEG_NONE:
                sq = (y * y).astype(BF16)
                segms = jnp.dot(sq, mseg_ref[seg], preferred_element_type=F32)
                y = (y * lax.rsqrt(segms + EPS)) * gain_ref[:, c0:c0 + COL_TILE]
            dil = out_dils[oi]
            if dil == 1:
                out_refs[oi][:, oc:oc + COL_TILE] = y.astype(BF16)
                continue
            rows = y.shape[0] // dil
            for lh in range(COL_TILE // LANES):
                buf = ybuf.at[nbuf_used % PROJ_STAGE_SLABS]
                nbuf_used += 1
                buf[...] = y[:, lh * LANES:(lh + 1) * LANES]
                for r in range(dil):
                    col = r * out_widths[oi] + oc + lh * LANES
                    out_refs[oi][:, col:col + LANES] = buf[pl.ds(r, rows, stride=dil), :].astype(BF16)


def _proj(x2d, norm_gain, w, gain, tiles, out_widths, tm, out_dils=None):
    t, d = x2d.shape
    n = w.shape[1]
    tm = min(tm, t)
    out_dils = tuple(out_dils) if out_dils is not None else (1,) * len(out_widths)
    assert all(tm % (16 * dl) == 0 for dl in out_dils)
    mseg = jnp.asarray(
        np.stack([_seg_matrix(COL_TILE, [(s, 64) for s in range(0, COL_TILE, 64)]),
                  _seg_matrix(COL_TILE, [(s, 128) for s in range(0, COL_TILE, 128)])]), BF16)
    return pl.pallas_call(
        functools.partial(_proj_kernel, tiles=tuple(tiles), out_widths=tuple(out_widths), out_dils=out_dils),
        grid=(t // tm,),
        in_specs=[
            pl.BlockSpec((tm, d), lambda i: (i, 0)),
            _resident((1, d), lambda i: (0, 0)),
            _resident((d, n), lambda i: (0, 0)),
            _resident((1, n), lambda i: (0, 0)),
            _resident((2, COL_TILE, COL_TILE), lambda i: (0, 0, 0)),
        ],
        out_specs=[pl.BlockSpec((tm // dl, dl * wd), lambda i: (i, 0)) for wd, dl in zip(out_widths, out_dils)],
        out_shape=[jax.ShapeDtypeStruct((t // dl, dl * wd), BF16) for wd, dl in zip(out_widths, out_dils)],
        scratch_shapes=[pltpu.VMEM((PROJ_STAGE_SLABS, tm, LANES), F32)] if max(out_dils) > 1 else [],
        compiler_params=_cparams(("parallel",)),
        name="proj",
    )(x2d, norm_gain.reshape(1, d), w, gain, mseg)


def _main_plan():
    zero = N_IN
    cols, gains, tiles = [], [], []
    widths = [3 * 512, 512, 1024] + [3 * 512] * N_DIL + [512]

    def add(out_idx, out_col, src_cols, seg, gain_key):
        assert len(src_cols) % COL_TILE == 0
        c0 = len(cols)
        cols.extend(src_cols)
        gains.extend([gain_key] * len(src_cols))
        for k in range(len(src_cols) // COL_TILE):
            tiles.append((c0 + k * COL_TILE, seg, out_idx, out_col + k * COL_TILE))

    hm = [m * A_HEADS * A_DK + h * A_DK + d for h in range(A_HEADS) for m in range(2) for d in range(A_DK)]
    add(0, 0, [OFF_A_Q + c for c in hm], SEG64, "a_q")
    add(0, 512, [OFF_A_K + c for c in hm], SEG64, "a_k")
    add(0, 1024, [OFF_A_V + c for c in range(A_HEADS * A_DV)], SEG_NONE, None)
    braw = ([OFF_B_CQ + c for c in range(B_Q_LORA)] + [OFF_B_CKV + c for c in range(B_KV_LORA)]
            + [OFF_B_KR + c for c in range(B_ROPE)] + [zero] * (512 - B_Q_LORA - B_KV_LORA - B_ROPE))
    add(1, 0, braw, SEG_NONE, None)
    add(2, 0, [OFF_C_Q + c for c in range(C_QH * C_HD)], SEG64, "c_q")
    dup = [kv * C_HD + d for kv in range(C_KVH) for _ in range(2) for d in range(C_HD)]
    add(2, 512, [OFF_C_K + c for c in dup], SEG64, "c_k")
    add(2, 768, [OFF_C_V + c for c in dup], SEG_NONE, None)
    for g in range(N_DIL):
        gsl = [g * D_HEADS * D_HD + c for c in range(D_HEADS * D_HD)]
        add(3 + g, 0, [OFF_D_Q + c for c in gsl], SEG64, "d_q")
        add(3 + g, 512, [OFF_D_K + c for c in gsl], SEG64, "d_k")
        add(3 + g, 1024, [OFF_D_V + c for c in gsl], SEG_NONE, None)
    add(3 + N_DIL, 0, [OFF_M_Q + c for c in range(M_HEADS * M_HD)], SEG128, "m_q")
    return np.asarray(cols, np.int32), gains, tiles, widths


_MAIN_COLS, _MAIN_GAINS, _MAIN_TILES, _MAIN_WIDTHS = _main_plan()


def _main_gain_vector(p):
    per_key = {
        "a_q": p["a_qn"] * (A_DK ** -0.5 * LOG2E), "a_k": p["a_kn"],
        "c_q": p["c_qn"] * (C_HD ** -0.5 * LOG2E), "c_k": p["c_kn"],
        "d_q": p["d_qn"] * (D_HD ** -0.5 * LOG2E), "d_k": p["d_kn"],
        "m_q": p["m_qn"] * (M_HD ** -0.5),
    }
    n = len(_MAIN_GAINS)
    pieces, k = [], 0
    while k < n:
        key = _MAIN_GAINS[k]
        k2 = k
        while k2 < n and _MAIN_GAINS[k2] == key:
            k2 += 1
        if key is None:
            pieces.append(jnp.ones((k2 - k,), F32))
        else:
            gvec = per_key[key].astype(F32)
            pieces.append(jnp.tile(gvec, (k2 - k) // gvec.shape[0]))
        k = k2
    return jnp.concatenate(pieces).reshape(1, n)


def _mla_prep_kernel(b_ref, wq_ref, wk_ref, wv_ref, place_ref, mseg_ref, cqg_ref, ckvg_ref,
                     qg_ref, kg_ref, ct_ref, s1_ref, s2_ref, q_out, k_out, v_out):
    braw = b_ref[...]
    cq = braw[:, :B_Q_LORA].astype(F32)
    cqn = ((cq * lax.rsqrt(jnp.mean(cq * cq, axis=-1, keepdims=True) + EPS)) * cqg_ref[...]).astype(BF16)
    ckv = braw[:, B_Q_LORA:B_Q_LORA + B_KV_LORA].astype(F32)
    ckvn = ((ckv * lax.rsqrt(jnp.mean(ckv * ckv, axis=-1, keepdims=True) + EPS)) * ckvg_ref[...]).astype(BF16)
    qb = jnp.dot(cqn, wq_ref[...], preferred_element_type=F32)
    kb = jnp.dot(ckvn, wk_ref[...], preferred_element_type=F32)
    kb = kb + jnp.dot(braw[:, B_Q_LORA + B_KV_LORA:], place_ref[...], preferred_element_type=F32)
    v_out[...] = jnp.dot(ckvn, wv_ref[...], preferred_element_type=F32).astype(BF16)
    ct, s1, s2 = ct_ref[...], s1_ref[...], s2_ref[...]

    def finish(t, gain_ref, out):
        for h in range(B_HEADS):
            y = t[:, h * LANES:(h + 1) * LANES]
            segms = jnp.dot((y * y).astype(BF16), mseg_ref[...], preferred_element_type=F32)
            yn = (y * lax.rsqrt(segms + EPS)) * gain_ref[...]
            half = B_ROPE // 2
            r = yn * ct + pltpu.roll(yn, half, 1) * s1 + pltpu.roll(yn, LANES - half, 1) * s2
            out[:, h * LANES:(h + 1) * LANES] = r.astype(BF16)

    finish(qb, qg_ref, q_out)
    finish(kb, kg_ref, k_out)


def _rope_tables(s):
    half = B_ROPE // 2
    inv = ROPE_THETA ** (-jnp.arange(half, dtype=F32) / half)
    ang = jnp.arange(s, dtype=F32)[:, None] * inv[None, :]
    cos, sin = jnp.cos(ang), jnp.sin(ang)
    one = jnp.ones((s, B_NOPE), F32)
    zpad = jnp.zeros((s, LANES - B_NOPE - B_ROPE), F32)
    z64 = jnp.zeros((s, B_NOPE), F32)
    zh = jnp.zeros((s, half), F32)
    ct = jnp.concatenate([one, cos, cos, zpad], axis=1)
    s1 = jnp.concatenate([z64, zh, sin, zpad], axis=1)
    s2 = jnp.concatenate([z64, -sin, zh, zpad], axis=1)
    return ct, s1, s2


def _mla_prep(braw, p, bsz, s, tm):
    t = braw.shape[0]
    tm = min(tm, s)
    pad = LANES - B_NOPE - B_ROPE
    wq = p["w_qb"].reshape(B_Q_LORA, B_HEADS, B_NOPE + B_ROPE)
    wq = jnp.pad(wq, ((0, 0), (0, 0), (0, pad))).reshape(B_Q_LORA, B_HEADS * LANES).astype(BF16)
    wkv = p["w_kvb"].reshape(B_KV_LORA, B_HEADS, B_NOPE + B_DV)
    wk = jnp.pad(wkv[:, :, :B_NOPE], ((0, 0), (0, 0), (0, LANES - B_NOPE))).reshape(B_KV_LORA, B_HEADS * LANES).astype(BF16)
    wv = wkv[:, :, B_NOPE:].reshape(B_KV_LORA, B_HEADS * B_DV).astype(BF16)
    place = np.zeros((LANES, B_HEADS * LANES), np.float32)
    for h in range(B_HEADS):
        for d in range(B_ROPE):
            place[d, h * LANES + B_NOPE + d] = 1.0
    mseg = _seg_matrix(LANES, [(0, B_NOPE), (B_NOPE, B_ROPE)])
    scale = (B_NOPE + B_ROPE) ** -0.5 * LOG2E
    zp = jnp.zeros((pad,), F32)
    qg = jnp.concatenate([p["b_qn"].astype(F32) * scale, zp]).reshape(1, LANES)
    kg = jnp.concatenate([p["b_kn"].astype(F32), zp]).reshape(1, LANES)
    ct, s1, s2 = _rope_tables(s)
    nst = s // tm
    full = lambda shape: _resident(shape, lambda b, i: tuple(0 for _ in shape))
    row = lambda wd: pl.BlockSpec((tm, wd), lambda b, i: (b * nst + i, 0))
    tab = pl.BlockSpec((tm, LANES), lambda b, i: (i, 0))
    return pl.pallas_call(
        _mla_prep_kernel,
        grid=(bsz, nst),
        in_specs=[row(512), full((B_Q_LORA, 512)), full((B_KV_LORA, 512)), full((B_KV_LORA, 512)),
                  full((LANES, 512)), full((LANES, LANES)), full((1, B_Q_LORA)), full((1, B_KV_LORA)),
                  full((1, LANES)), full((1, LANES)), tab, tab, tab],
        out_specs=[row(512), row(512), row(512)],
        out_shape=[jax.ShapeDtypeStruct((t, 512), BF16)] * 3,
        compiler_params=_cparams(("parallel", "parallel")),
        name="mla_prep",
    )(braw, wq, wk, wv, jnp.asarray(place, BF16), jnp.asarray(mseg, BF16),
      p["b_cqn"].astype(F32).reshape(1, -1), p["b_ckvn"].astype(F32).reshape(1, -1), qg, kg, ct, s1, s2)


def _flash_kernel(slopes_ref, lam_ref, hn_ref, q_ref, k_ref, v_ref, o_ref, m_sc, l_sc, acc_sc,
                  *, nmaps, alibi, tq, tk, seq, lam_init):
    h = pl.program_id(1)
    q0 = pl.program_id(2) * tq
    q = q_ref[...]
    if nmaps == 2:
        lane = lax.broadcasted_iota(jnp.int32, q.shape, 1)
        qs = [jnp.where(lane < A_DK, q, jnp.zeros_like(q)), jnp.where(lane >= A_DK, q, jnp.zeros_like(q))]
    else:
        qs = [q]
    m_sc[...] = jnp.full(m_sc.shape, NEG_INF, F32)
    l_sc[...] = jnp.zeros(l_sc.shape, F32)
    acc_sc[...] = jnp.zeros(acc_sc.shape, F32)
    if alibi:
        slope2 = slopes_ref[h] * LOG2E
        dmat = (lax.broadcasted_iota(jnp.int32, (tq, tk), 0)
                - lax.broadcasted_iota(jnp.int32, (tq, tk), 1)).astype(F32)

    def body(j, carry):
        k0 = pl.multiple_of(j * tk, tk)
        k = k_ref[pl.ds(k0, tk), :]
        v = v_ref[pl.ds(k0, tk), :]
        if alibi:
            bias = -slope2 * jnp.abs(dmat + (q0 - k0).astype(F32))
        for m in range(nmaps):
            s = lax.dot_general(qs[m], k, (((1,), (1,)), ((), ())), preferred_element_type=F32)
            if alibi:
                s = s + bias
            m_old = m_sc[m]
            m_new = jnp.maximum(m_old, jnp.max(s, axis=1, keepdims=True))
            alpha = jnp.exp2(m_old - m_new)
            pr = jnp.exp2(s - m_new)
            l_sc[m] = alpha * l_sc[m] + jnp.sum(pr, axis=1, keepdims=True)
            acc_sc[m] = alpha * acc_sc[m] + jnp.dot(pr.astype(BF16), v, preferred_element_type=F32)
            m_sc[m] = m_new
        return carry

    lax.fori_loop(0, seq // tk, body, 0)
    outs = [acc_sc[m] / l_sc[m] for m in range(nmaps)]
    o_ref[...] = _flash_epilogue(outs, lam_ref, hn_ref, lam_init).astype(BF16)


def _flash_epilogue(outs, lam_ref, hn_ref, lam_init):
    if len(outs) == 1:
        return outs[0]
    lf = lam_ref[...]
    e1 = jnp.exp(jnp.sum(lf[0:1] * lf[1:2], axis=1, keepdims=True))
    e2 = jnp.exp(jnp.sum(lf[2:3] * lf[3:4], axis=1, keepdims=True))
    lam = e1 - e2 + lam_init
    o = outs[0] - lam * outs[1]
    o = (o * lax.rsqrt(jnp.mean(o * o, axis=-1, keepdims=True) + EPS)) * hn_ref[...]
    return o * (1.0 - lam_init)


def _flash_fixed_kernel(shift_ref, slopes_ref, lam_ref, hn_ref, q_ref, k_ref, v_ref, o_ref, v1_sc, acc_sc,
                        *, nmaps, alibi, tq, tk, seq, lam_init, slopes_static):
    h = pl.program_id(1)
    i = pl.program_id(2)
    q0 = i * tq

    @pl.when(i == 0)
    def _():
        v1_sc[:, :LANES] = v_ref[...]
        v1_sc[:, LANES:] = jnp.ones((seq, LANES), BF16)

    q = q_ref[...]
    if nmaps == 2:
        lane = lax.broadcasted_iota(jnp.int32, q.shape, 1)
        qs = [jnp.where(lane < A_DK, q, jnp.zeros_like(q)), jnp.where(lane >= A_DK, q, jnp.zeros_like(q))]
    else:
        qs = [q]
    acc_sc[...] = jnp.zeros(acc_sc.shape, F32)
    shift = shift_ref[0]
    if alibi:
        slope2 = slopes_ref[h] * LOG2E
        t0 = (lax.broadcasted_iota(jnp.int32, (tq, tk), 0)
              - lax.broadcasted_iota(jnp.int32, (tq, tk), 1)).astype(F32) * slope2

    def step(j):
        k0 = j * tk if isinstance(j, int) else pl.multiple_of(j * tk, tk)
        k = k_ref[pl.ds(k0, tk), :]
        v1 = v1_sc[pl.ds(k0, tk), :]
        if alibi:
            sub = jnp.abs(t0 + slope2 * (q0 - k0).astype(F32)) + shift
        ps = []
        for m in range(nmaps):
            s = lax.dot_general(qs[m], k, (((1,), (1,)), ((), ())), preferred_element_type=F32)
            ps.append(jnp.exp2(s - sub if alibi else s - shift).astype(BF16))
        pr = ps[0] if nmaps == 1 else jnp.concatenate(ps, axis=0)
        acc_sc[...] += jnp.dot(pr, v1, preferred_element_type=F32)

    nkv = seq // tk
    if alibi:
        assert tq == tk
        for hd, slope in enumerate(slopes_static):
            reach = int(math.floor((ZERO_EXP2 / (slope * LOG2E) - 1.0) / tk)) + 1
            cnt = min(nkv, 2 * reach + 1)

            @pl.when(h == hd)
            def _(reach=reach, cnt=cnt):
                start = jnp.clip(i - reach, 0, nkv - cnt) if cnt < nkv else 0
                for jj in range(cnt):
                    step(start + jj)
    else:
        for jj in range(nkv):
            step(jj)
    outs = [acc_sc[m * tq:(m + 1) * tq, :LANES] / acc_sc[m * tq:(m + 1) * tq, LANES:] for m in range(nmaps)]
    o_ref[...] = _flash_epilogue(outs, lam_ref, hn_ref, lam_init).astype(BF16)


def _flash(qa, ka, va, qcb, kcb, vcb, bsz, s, heads, nmaps, alibi, slopes, lam, hn, lam_init, bound2):
    t = qa.shape[0]
    name = "flash_diff" if nmaps == 2 else "flash_mla"
    shift = bound2.reshape(1).astype(F32)

    def call(kern, tq, tk, scratch, extra_in, extra_args, suffix):
        tq_, tk_ = min(tq, s), min(tk, s)
        nq = s // tq_
        return pl.pallas_call(
            functools.partial(kern, nmaps=nmaps, alibi=alibi, tq=tq_, tk=tk_, seq=s, lam_init=lam_init),
            grid=(bsz, heads, nq),
            in_specs=extra_in + [
                pl.BlockSpec(memory_space=pltpu.SMEM),
                _resident(lam.shape, lambda b, h, i: (0, 0)),
                _resident(hn.shape, lambda b, h, i: (0, 0)),
                pl.BlockSpec((tq_, LANES), lambda b, h, i: (b * nq + i, qcb + h)),
                pl.BlockSpec((s, LANES), lambda b, h, i: (b, kcb + h)),
                pl.BlockSpec((s, LANES), lambda b, h, i: (b, vcb + h)),
            ],
            out_specs=pl.BlockSpec((tq_, LANES), lambda b, h, i: (b * nq + i, h)),
            out_shape=jax.ShapeDtypeStruct((t, heads * LANES), BF16),
            scratch_shapes=scratch(tq_),
            compiler_params=_cparams(("parallel", "parallel", "arbitrary")),
            name=name + suffix,
        )(*extra_args, slopes, lam, hn, qa, ka, va)

    def fixed():
        return call(functools.partial(_flash_fixed_kernel, slopes_static=_alibi_slopes_static(heads)), 512, 512,
                    lambda tq_: [pltpu.VMEM((s, 2 * LANES), BF16), pltpu.VMEM((nmaps * tq_, 2 * LANES), F32)],
                    [pl.BlockSpec(memory_space=pltpu.SMEM)], [shift], "_fixed")

    def online():
        return call(_flash_kernel, 256, 512,
                    lambda tq_: [pltpu.VMEM((nmaps, tq_, 1), F32), pltpu.VMEM((nmaps, tq_, 1), F32),
                                 pltpu.VMEM((nmaps, tq_, LANES), F32)],
                    [], [], "_online")

    return lax.cond(bound2 <= MAX_FIXED_SHIFT, fixed, online)


def _band_kernel(slopes_ref, sink_ref, q_ref, k_ref, v_ref, *outs, window, qb, kw, ut, u_len, dil, use_sink, emit_lse):
    o_ref = outs[0]
    slab = pl.program_id(0)
    ubase = pl.program_id(3) * ut
    lane = lax.broadcasted_iota(jnp.int32, (kw, LANES), 1)
    qlane = lax.broadcasted_iota(jnp.int32, (qb, LANES), 1)
    dmat = (lax.broadcasted_iota(jnp.int32, (qb, kw), 0) - lax.broadcasted_iota(jnp.int32, (qb, kw), 1))
    for sb in range(ut // qb):
        u0 = ubase + sb * qb
        start = pl.multiple_of(jnp.clip(u0 - window, 0, u_len - kw), 16)
        q = q_ref[sb * qb:(sb + 1) * qb, :]
        k = k_ref[pl.ds(start, kw), :]
        v = v_ref[pl.ds(start, kw), :]
        rel = dmat + (u0 - start)
        valid = jnp.abs(rel) <= window
        dist = jnp.abs(rel).astype(F32) * float(dil)
        o_acc = jnp.zeros((qb, LANES), F32)
        lse_acc = jnp.zeros((qb, LANES), F32)
        for hh in range(2):
            head = slab * 2 + hh
            sel_q = (qlane < 64) if hh == 0 else (qlane >= 64)
            sel_v = (lane < 64) if hh == 0 else (lane >= 64)
            qh = jnp.where(sel_q, q, jnp.zeros_like(q))
            vh = jnp.where(sel_v, v, jnp.zeros_like(v))
            s = lax.dot_general(qh, k, (((1,), (1,)), ((), ())), preferred_element_type=F32)
            logits = jnp.where(valid, s - (slopes_ref[head] * LOG2E) * dist, NEG_INF)
            mx = jnp.max(logits, axis=1, keepdims=True)
            if use_sink:
                sk = sink_ref[head]
                mx = jnp.maximum(mx, sk)
            e = jnp.exp2(logits - mx)
            den = jnp.sum(e, axis=1, keepdims=True)
            if use_sink:
                den = den + jnp.exp2(sk - mx)
            pv = jnp.dot(e.astype(BF16), vh, preferred_element_type=F32)
            o_acc = o_acc + pv / den
            if emit_lse:
                lse_acc = lse_acc + jnp.where(sel_q, mx + jnp.log2(den), 0.0)
        o_ref[sb * qb:(sb + 1) * qb, :] = o_acc.astype(BF16)
        if emit_lse:
            outs[1][sb * qb:(sb + 1) * qb, :] = lse_acc


def _band_fixed_kernel(shift_ref, slopes_ref, sink_ref, q_ref, k_ref, v_ref, *outs,
                       window, qb, kw, ut, u_len, dil, use_sink, emit_lse):
    o_ref, sub_sc = outs[0], outs[-1]
    slab = pl.program_id(0)
    ubase = pl.program_id(3) * ut
    shift = shift_ref[0]
    qlane = lax.broadcasted_iota(jnp.int32, (qb, LANES), 1)
    first = qlane < 64
    ones = jnp.ones((kw, LANES), BF16)

    @pl.when((pl.program_id(1) == 0) & (pl.program_id(2) == 0) & (pl.program_id(3) == 0))
    def _():
        dmat =(lax.broadcasted_iota(jnp.int32, (qb, kw), 0) - lax.broadcasted_iota(jnp.int32, (qb, kw), 1))
        for case in range(3):
            absrel = jnp.abs(dmat + case * window)
            for hh in range(2):
                slope2 = slopes_ref[slab * 2 + hh] * (LOG2E * dil)
                sub_sc[case, hh] = jnp.where(absrel <= window, absrel.astype(F32) * slope2 + shift, -NEG_INF)

    for sb in range(ut // qb):
        u0 = ubase + sb * qb
        start = pl.multiple_of(jnp.clip(u0 - window, 0, u_len - kw), 16)
        case = (u0 - start) // window
        q = q_ref[sb * qb:(sb + 1) * qb, :]
        k = k_ref[pl.ds(start, kw), :]
        v1 = jnp.concatenate([v_ref[pl.ds(start, kw), :], ones], axis=1)
        ps = []
        for hh in range(2):
            qh = jnp.where(first if hh == 0 else jnp.logical_not(first), q, jnp.zeros_like(q))
            s = lax.dot_general(qh, k, (((1,), (1,)), ((), ())), preferred_element_type=F32)
            ps.append(jnp.exp2(s - sub_sc[case, hh]).astype(BF16))
        r = jnp.dot(jnp.concatenate(ps, axis=0), v1, preferred_element_type=F32)
        za, zb = r[:qb, LANES:], r[qb:, LANES:]
        if use_sink:
            za = za + jnp.exp2(jnp.full((1, 1), sink_ref[slab * 2] - shift, F32))
            zb = zb + jnp.exp2(jnp.full((1, 1), sink_ref[slab * 2 + 1] - shift, F32))
        o_ref[sb * qb:(sb + 1) * qb, :] = jnp.where(first, r[:qb, :LANES] / za, r[qb:, :LANES] / zb).astype(BF16)
        if emit_lse:
            outs[1][sb * qb:(sb + 1) * qb, :] = jnp.where(first, jnp.log2(za), jnp.log2(zb)) + shift


def _band(srcv, qcb, kcb, vcb, bsz, s, dil, ncb_src, window, slopes, sink, use_sink, emit_lse, ut, bound2):
    u_len = s // dil
    assert srcv.shape[0] == bsz * u_len
    qb = min(LANES, u_len)
    kw = min(qb + 2 * window, u_len)
    ut = min(ut, u_len)
    nu = u_len // ut
    nslab = 4
    out_shapes = [jax.ShapeDtypeStruct((bsz * u_len, dil * 512), BF16)]
    out_specs = [pl.BlockSpec((ut, LANES), lambda sl, b, r, i: (b * nu + i, r * nslab + sl))]
    if emit_lse:
        out_shapes.append(jax.ShapeDtypeStruct((bsz * u_len, dil * 512), F32))
        out_specs.append(pl.BlockSpec((ut, LANES), lambda sl, b, r, i: (b * nu + i, r * nslab + sl)))
    smem = pl.BlockSpec(memory_space=pltpu.SMEM)

    def call(kern, extra_in, extra_args, suffix, scratch=()):
        return pl.pallas_call(
            functools.partial(kern, window=window, qb=qb, kw=kw, ut=ut, u_len=u_len, dil=dil,
                              use_sink=use_sink, emit_lse=emit_lse),
            grid=(nslab, bsz, dil, nu),
            scratch_shapes=list(scratch),
            in_specs=extra_in + [
                smem, smem,
                pl.BlockSpec((ut, LANES), lambda sl, b, r, i: (b * nu + i, r * ncb_src + qcb(sl))),
                pl.BlockSpec((u_len, LANES), lambda sl, b, r, i: (b, r * ncb_src + kcb(sl))),
                pl.BlockSpec((u_len, LANES), lambda sl, b, r, i: (b, r * ncb_src + vcb(sl))),
            ],
            out_specs=out_specs,
            out_shape=out_shapes,
            compiler_params=_cparams(("arbitrary", "arbitrary", "arbitrary", "arbitrary")),
            name="band_w%d_d%d%s" % (window, dil, suffix),
        )(*extra_args, slopes, sink, srcv, srcv, srcv)

    res = lax.cond(bound2 <= MAX_FIXED_SHIFT,
                   lambda: call(_band_fixed_kernel, [smem], [bound2.reshape(1).astype(F32)], "_fixed",
                                [pltpu.VMEM((3, 2, qb, kw), F32)]),
                   lambda: call(_band_kernel, [], [], "_online"))
    return list(res)


def _memattn_kernel(q_ref, k_ref, v_ref, o_ref):
    for h in range(M_HEADS):
        sl = slice(h * M_HD, (h + 1) * M_HD)
        s = lax.dot_general(q_ref[:, sl], k_ref[:, sl], (((1,), (1,)), ((), ())), preferred_element_type=F32)
        mx = jnp.max(s, axis=1, keepdims=True)
        e = jnp.exp(s - mx)
        den = jnp.sum(e, axis=1, keepdims=True)
        pv = jnp.dot(e.astype(BF16), v_ref[:, sl], preferred_element_type=F32)
        o_ref[:, sl] = (pv / den).astype(BF16)


def _memattn(qm, mk, mv, bsz, s, tq):
    t = qm.shape[0]
    tq = min(tq, s)
    nq = s // tq
    wd = M_HEADS * M_HD
    return pl.pallas_call(
        _memattn_kernel,
        grid=(bsz, nq),
        in_specs=[pl.BlockSpec((tq, wd), lambda b, i: (b * nq + i, 0)),
                  pl.BlockSpec((N_MEM, wd), lambda b, i: (b, 0)),
                  pl.BlockSpec((N_MEM, wd), lambda b, i: (b, 0))],
        out_specs=pl.BlockSpec((tq, wd), lambda b, i: (b * nq + i, 0)),
        out_shape=jax.ShapeDtypeStruct((t, wd), BF16),
        compiler_params=_cparams(("parallel", "parallel")),
        name="memattn",
    )(qm, mk, mv)


def _merge_kernel(x_ref, ng_ref, wz_ref, wg_ref, bg_ref, wbr_ref, wout_ref,
                  oa_ref, ob_ref, oc_ref, om_ref, od0_ref, od1_ref, od2_ref, l0_ref, l1_ref, l2_ref,
                  y_ref, h_sc, acc_sc, o_sc, *stage, dils):
    j = pl.program_id(1)
    tm = x_ref.shape[0]
    nsl = BRANCH_W // LANES

    def token_major(ref, dil, slabs):
        if dil == 1:
            return ref[...].astype(F32)
        rows = tm // dil
        for r in range(dil):
            for c in range(nsl):
                col = r * BRANCH_W + c * LANES
                slabs[c, pl.ds(r, rows, stride=dil), :] = ref[:, col:col + LANES].astype(F32)
        return jnp.concatenate([slabs[c] for c in range(nsl)], axis=1)

    @pl.when(j == 0)
    def _():
        x = x_ref[...]
        ms = jnp.mean(x * x, axis=-1, keepdims=True)
        h_sc[...] = ((x * lax.rsqrt(ms + EPS)) * ng_ref[...]).astype(BF16)
        acc_sc[...] = jnp.zeros(acc_sc.shape, F32)
        o_sc[...] = oa_ref[...].astype(F32)

    @pl.when(j == 1)
    def _():
        o_sc[...] = ob_ref[...].astype(F32)

    @pl.when(j == 2)
    def _():
        o_sc[...] = oc_ref[...].astype(F32)

    @pl.when(j == 3)
    def _():
        st = iter(stage)
        ls = [token_major(ref, dl, next(st) if dl > 1 else None) for ref, dl in zip((l0_ref, l1_ref, l2_ref), dils)]
        os_ = [token_major(ref, dl, next(st) if dl > 1 else None) for ref, dl in zip((od0_ref, od1_ref, od2_ref), dils)]
        mx = jnp.maximum(jnp.maximum(ls[0], ls[1]), ls[2])
        e0, e1, e2 = jnp.exp2(ls[0] - mx), jnp.exp2(ls[1] - mx), jnp.exp2(ls[2] - mx)
        den = e0 + e1 + e2
        o_sc[...] = (e0 / den) * os_[0] + (e1 / den) * os_[1] + (e2 / den) * os_[2]

    @pl.when(j == 4)
    def _():
        o_sc[...] = om_ref[...].astype(F32)

    h = h_sc[...]
    z = jnp.dot(h, wz_ref[...], preferred_element_type=F32)
    g = jnp.dot(h, wg_ref[...], preferred_element_type=F32) + bg_ref[0]
    u = (o_sc[...] * (z / (1.0 + jnp.exp(-z)))).astype(BF16)
    tbr = jnp.dot(u, wbr_ref[0], preferred_element_type=F32)
    acc_sc[...] += tbr / (1.0 + jnp.exp(-g))

    @pl.when(j == N_BRANCH - 1)
    def _():
        y_ref[...] = x_ref[...] + jnp.dot(acc_sc[...].astype(BF16), wout_ref[...], preferred_element_type=F32)


def _merge(x2d, p, wz, wg, oa, ob, oc, om, ods, lses, tm):
    t, d = x2d.shape
    tm = min(tm, t)
    rowf = lambda wd: pl.BlockSpec((tm, wd), lambda i, j: (i, 0))
    dils = tuple(dl for _, dl in DIL_PAIRS)
    classf = [pl.BlockSpec((tm // dl, dl * BRANCH_W), lambda i, j: (i, 0)) for dl in dils]
    nstage = 2 * sum(1 for dl in dils if dl > 1)
    return pl.pallas_call(
        functools.partial(_merge_kernel, dils=dils),
        grid=(t // tm, N_BRANCH),
        in_specs=[
            rowf(d),
            _resident((1, d), lambda i, j: (0, 0)),
            pl.BlockSpec((d, BRANCH_W), lambda i, j: (0, j)),
            pl.BlockSpec((d, d), lambda i, j: (0, j)),
            pl.BlockSpec((1, 1, d), lambda i, j: (j, 0, 0)),
            pl.BlockSpec((1, BRANCH_W, d), lambda i, j: (j, 0, 0)),
            _resident((d, d), lambda i, j: (0, 0)),
        ] + [rowf(BRANCH_W)] * 4 + classf + classf,
        out_specs=rowf(d),
        out_shape=jax.ShapeDtypeStruct((t, d), F32),
        scratch_shapes=[pltpu.VMEM((tm, d), BF16), pltpu.VMEM((tm, d), F32), pltpu.VMEM((tm, BRANCH_W), F32)]
        + [pltpu.VMEM((BRANCH_W // LANES, tm, LANES), F32)] * nstage,
        compiler_params=_cparams(("parallel", "arbitrary")),
        name="merge",
    )(x2d, p["norm_g"].astype(F32).reshape(1, d), wz, wg, p["b_gate"].astype(F32).reshape(N_BRANCH, 1, d),
      p["w_br_bf16"], p["w_out_bf16"], oa, ob, oc, om, *ods, *lses)


def _alibi_slopes_static(n):
    return tuple(float(v) for v in 2.0 ** (-8.0 * np.arange(1, n + 1, dtype=np.float64) / n))


def _alibi_slopes(n):
    return jnp.asarray(_alibi_slopes_static(n), F32)


def _encoder_layer(x, mem, layer_idx, p):
    bsz, s, d = x.shape
    t = bsz * s
    x2d = x.reshape(t, d)

    pa, pb, pc, pd0, pd1, pd2, pm = _proj(x2d, p["norm_g"].astype(F32), p["w_main"], p["gain_main"],
                                          _MAIN_TILES, _MAIN_WIDTHS, tm=512,
                                          out_dils=[1, 1, 1] + [dl for _, dl in DIL_PAIRS] + [1])

    lam_init = 0.8 - 0.6 * math.exp(-0.3 * layer_idx)
    bound_a = (BOUND_MARGIN * A_DK * (A_DK ** -0.5 * LOG2E)
               * jnp.max(jnp.abs(p["a_qn"].astype(F32))) * jnp.max(jnp.abs(p["a_kn"].astype(F32))))
    oa = _flash(pa, pa, pa, 0, 4, 8, bsz, s, A_HEADS, 2, True, _alibi_slopes(A_HEADS),
                p["a_lam"].astype(F32), p["a_hn"].astype(F32).reshape(1, A_DV), lam_init, bound_a)

    qb, kb, vb = _mla_prep(pb, p, bsz, s, tm=512)
    dummy_lam = jnp.zeros((4, A_DK), F32)
    dummy_hn = jnp.ones((1, A_DV), F32)
    dqk = B_NOPE + B_ROPE
    bound_b = (BOUND_MARGIN * dqk * (dqk ** -0.5 * LOG2E)
               * jnp.max(jnp.abs(p["b_qn"].astype(F32))) * jnp.max(jnp.abs(p["b_kn"].astype(F32))))
    ob = _flash(qb, kb, vb, 0, 0, 0, bsz, s, B_HEADS, 1, False, _alibi_slopes(B_HEADS),
                dummy_lam, dummy_hn, 0.0, bound_b)

    sink2 = p["c_sink"].astype(F32) * LOG2E
    bound_c = jnp.maximum(
        BOUND_MARGIN * C_HD * (C_HD ** -0.5 * LOG2E)
        * jnp.max(jnp.abs(p["c_qn"].astype(F32))) * jnp.max(jnp.abs(p["c_kn"].astype(F32))),
        jnp.max(sink2))
    oc, = _band(pc, lambda sl: sl, lambda sl: 4 + sl // 2, lambda sl: 6 + sl // 2, bsz, s, 1, 8,
                C_WINDOW, _alibi_slopes(C_QH), sink2, True, False, BAND_ROWS, bound_c)

    ods, lses = [], []
    zero_sink = jnp.zeros((D_HEADS,), F32)
    bound_d = (BOUND_MARGIN * D_HD * (D_HD ** -0.5 * LOG2E)
               * jnp.max(jnp.abs(p["d_qn"].astype(F32))) * jnp.max(jnp.abs(p["d_kn"].astype(F32))))
    for g, (win, dil) in enumerate(DIL_PAIRS):
        og, lg = _band((pd0, pd1, pd2)[g], lambda sl: sl, lambda sl: 4 + sl,
                       lambda sl: 8 + sl, bsz, s, dil, 12, win // (2 * dil),
                       _alibi_slopes(D_HEADS), zero_sink, False, True, BAND_ROWS, bound_d)
        ods.append(og)
        lses.append(lg)

    mtiles = [(0, SEG128, 0, 0), (256, SEG128, 0, 256), (512, SEG_NONE, 1, 0), (768, SEG_NONE, 1, 256)]
    mk, mv = _proj(mem.reshape(bsz * N_MEM, d), p["m_norm"].astype(F32), p["w_mem_kv_bf16"], p["gain_mem"],
                   mtiles, [512, 512], tm=256)
    om = _memattn(pm, mk, mv, bsz, s, tq=512)

    y = _merge(x2d, p, p["wz"], p["wg"], oa, ob, oc, om, ods, lses, tm=512)
    return y.reshape(bsz, s, d)


def _prepare_layer(p):
    w_in = p["w_in"]
    pieces, k, n = [], 0, len(_MAIN_COLS)
    while k < n:
        k2 = k + 1
        if _MAIN_COLS[k] == N_IN:
            while k2 < n and _MAIN_COLS[k2] == N_IN:
                k2 += 1
            pieces.append(jnp.zeros((w_in.shape[0], k2 - k), BF16))
        else:
            while k2 < n and _MAIN_COLS[k2] == _MAIN_COLS[k2 - 1] + 1:
                k2 += 1
            pieces.append(w_in[:, int(_MAIN_COLS[k]):int(_MAIN_COLS[k]) + (k2 - k)].astype(BF16))
        k = k2
    return {
        "w_main": jnp.concatenate(pieces, axis=1),
        "gain_main": _main_gain_vector(p),
        "wz": w_in[:, OFF_Z:OFF_G].astype(BF16),
        "wg": w_in[:, OFF_G:].astype(BF16),
        "w_mem_kv_bf16": p["w_mem_kv"].astype(BF16),
        "gain_mem": jnp.concatenate([jnp.tile(p["m_kn"].astype(F32), M_HEADS),
                                     jnp.ones((M_HEADS * M_HD,), F32)]).reshape(1, -1),
        "w_br_bf16": p["w_br"].astype(BF16),
        "w_out_bf16": p["w_out"].astype(BF16),
    }


def kernel(x_prompt, x_sample, mem_prompt, mem_sample, norm_g, w_in, a_qn, a_kn, a_lam, a_hn, b_cqn, b_ckvn, w_qb, w_kvb, b_qn, b_kn, c_qn, c_kn, c_sink, d_qn, d_kn, m_norm, w_mem_kv, m_qn, m_kn, b_gate, w_br, w_out):
    depth = norm_g.shape[0]
    y_prompt, y_sample = x_prompt, x_sample
    for l in range(depth):
        p = {
            "norm_g": norm_g[l], "w_in": w_in[l],
            "a_qn": a_qn[l], "a_kn": a_kn[l], "a_lam": a_lam[l], "a_hn": a_hn[l],
            "b_cqn": b_cqn[l], "b_ckvn": b_ckvn[l], "w_qb": w_qb[l], "w_kvb": w_kvb[l],
            "b_qn": b_qn[l], "b_kn": b_kn[l],
            "c_qn": c_qn[l], "c_kn": c_kn[l], "c_sink": c_sink[l],
            "d_qn": d_qn[l], "d_kn": d_kn[l],
            "m_norm": m_norm[l], "w_mem_kv": w_mem_kv[l], "m_qn": m_qn[l], "m_kn": m_kn[l],
            "b_gate": b_gate[l], "w_br": w_br[l], "w_out": w_out[l],
        }
        p.update(_prepare_layer(p))
        y_prompt = _encoder_layer(y_prompt, mem_prompt, l, p)
        y_sample = _encoder_layer(y_sample, mem_sample, l, p)
    return (y_prompt, y_sample)
```

```python
import functools
import math

import numpy as np
import jax
import jax.numpy as jnp
from jax import lax
from jax.experimental import pallas as pl
from jax.experimental.pallas import tpu as pltpu

F32 = jnp.float32
BF16 = jnp.bfloat16

D_MODEL = 1024
N_MEM = 256
BRANCH_W = 512
N_BRANCH = 5
NEG_INF = -1e30
EPS = 1e-6
A_HEADS, A_DK, A_DV = 4, 64, 128
B_HEADS, B_Q_LORA, B_KV_LORA, B_NOPE, B_ROPE, B_DV = 4, 256, 128, 64, 32, 128
ROPE_THETA = 10000.0
C_QH, C_KVH, C_HD, C_WINDOW = 8, 2, 64, 128
D_HEADS, D_HD = 8, 64
DIL_PAIRS = ((128, 1), (512, 4), (2048, 16))
N_DIL = 3
M_HEADS, M_HD = 4, 128

OFF_A_Q = 0
OFF_A_K = OFF_A_Q + 2 * A_HEADS * A_DK
OFF_A_V = OFF_A_K + 2 * A_HEADS * A_DK
OFF_B_CQ = OFF_A_V + A_HEADS * A_DV
OFF_B_CKV = OFF_B_CQ + B_Q_LORA
OFF_B_KR = OFF_B_CKV + B_KV_LORA
OFF_C_Q = OFF_B_KR + B_ROPE
OFF_C_K = OFF_C_Q + C_QH * C_HD
OFF_C_V = OFF_C_K + C_KVH * C_HD
OFF_D_Q = OFF_C_V + C_KVH * C_HD
OFF_D_K = OFF_D_Q + N_DIL * D_HEADS * D_HD
OFF_D_V = OFF_D_K + N_DIL * D_HEADS * D_HD
OFF_M_Q = OFF_D_V + N_DIL * D_HEADS * D_HD
OFF_Z = OFF_M_Q + M_HEADS * M_HD
OFF_G = OFF_Z + N_BRANCH * BRANCH_W
N_IN = OFF_G + N_BRANCH * D_MODEL

LANES = 128
COL_TILE = 256
VMEM_LIMIT = 56 * 1024 * 1024

SEG_NONE, SEG64, SEG128 = -1, 0, 1

LOG2E = 1.4426950408889634
MAX_FIXED_SHIFT = 30.0 * LOG2E
BOUND_MARGIN = 1.02
PROJ_STAGE_SLABS = 4
ZERO_EXP2 = 151.0
BAND_ROWS = 2048


def _cparams(sem):
    return pltpu.CompilerParams(dimension_semantics=sem, vmem_limit_bytes=VMEM_LIMIT)


def _resident(shape, index_map):
    return pl.BlockSpec(shape, index_map, pipeline_mode=pl.Buffered(1))


def _seg_matrix(width, segs):
    m = np.zeros((width, width), np.float32)
    for s, n in segs:
        m[s:s + n, s:s + n] = 1.0 / n
    return m


def _proj_kernel(x_ref, ng_ref, w_ref, gain_ref, mseg_ref, *refs, tiles, out_widths, out_dils):
    out_refs = refs[:len(out_widths)]
    ybuf = refs[len(out_widths)] if max(out_dils) > 1 else None
    nbuf_used = 0
    x = x_ref[...]
    ms = jnp.mean(x * x, axis=-1, keepdims=True)
    h = ((x * lax.rsqrt(ms + EPS)) * ng_ref[...]).astype(BF16)
    assert len(tiles) % 2 == 0
    for pair in range(0, len(tiles), 2):
        base = tiles[pair][0]
        assert tiles[pair + 1][0] == base + COL_TILE
        y2 = jnp.dot(h, w_ref[:, base:base + 2 * COL_TILE], preferred_element_type=F32)
        for half, (c0, seg, oi, oc) in enumerate(tiles[pair:pair + 2]):
            y = y2[:, half * COL_TILE:(half + 1) * COL_TILE]
            if seg != SEG_NONE:
                sq = (y * y).astype(BF16)
                segms = jnp.dot(sq, mseg_ref[seg], preferred_element_type=F32)
                y = (y * lax.rsqrt(segms + EPS)) * gain_ref[:, c0:c0 + COL_TILE]
            dil = out_dils[oi]
            if dil == 1:
                out_refs[oi][:, oc:oc + COL_TILE] = y.astype(BF16)
                continue
            rows = y.shape[0] // dil
            for lh in range(COL_TILE // LANES):
                buf = ybuf.at[nbuf_used % PROJ_STAGE_SLABS]
                nbuf_used += 1
                buf[...] = y[:, lh * LANES:(lh + 1) * LANES]
                for r in range(dil):
                    col = r * out_widths[oi] + oc + lh * LANES
                    out_refs[oi][:, col:col + LANES] = buf[pl.ds(r, rows, stride=dil), :].astype(BF16)


def _proj(x2d, norm_gain, w, gain, tiles, out_widths, tm, out_dils=None):
    t, d = x2d.shape
    n = w.shape[1]
    tm = min(tm, t)
    out_dils = tuple(out_dils) if out_dils is not None else (1,) * len(out_widths)
    assert all(tm % (16 * dl) == 0 for dl in out_dils)
    mseg = jnp.asarray(
        np.stack([_seg_matrix(COL_TILE, [(s, 64) for s in range(0, COL_TILE, 64)]),
                  _seg_matrix(COL_TILE, [(s, 128) for s in range(0, COL_TILE, 128)])]), BF16)
    return pl.pallas_call(
        functools.partial(_proj_kernel, tiles=tuple(tiles), out_widths=tuple(out_widths), out_dils=out_dils),
        grid=(t // tm,),
        in_specs=[
            pl.BlockSpec((tm, d), lambda i: (i, 0)),
            _resident((1, d), lambda i: (0, 0)),
            _resident((d, n), lambda i: (0, 0)),
            _resident((1, n), lambda i: (0, 0)),
            _resident((2, COL_TILE, COL_TILE), lambda i: (0, 0, 0)),
        ],
        out_specs=[pl.BlockSpec((tm // dl, dl * wd), lambda i: (i, 0)) for wd, dl in zip(out_widths, out_dils)],
        out_shape=[jax.ShapeDtypeStruct((t // dl, dl * wd), BF16) for wd, dl in zip(out_widths, out_dils)],
        scratch_shapes=[pltpu.VMEM((PROJ_STAGE_SLABS, tm, LANES), F32)] if max(out_dils) > 1 else [],
        compiler_params=_cparams(("parallel",)),
        name="proj",
    )(x2d, norm_gain.reshape(1, d), w, gain, mseg)


def _main_plan():
    zero = N_IN
    cols, gains, tiles = [], [], []
    widths = [3 * 512, 512, 1024] + [3 * 512] * N_DIL + [512]

    def add(out_idx, out_col, src_cols, seg, gain_key):
        assert len(src_cols) % COL_TILE == 0
        c0 = len(cols)
        cols.extend(src_cols)
        gains.extend([gain_key] * len(src_cols))
        for k in range(len(src_cols) // COL_TILE):
            tiles.append((c0 + k * COL_TILE, seg, out_idx, out_col + k * COL_TILE))

    hm = [m * A_HEADS * A_DK + h * A_DK + d for h in range(A_HEADS) for m in range(2) for d in range(A_DK)]
    add(0, 0, [OFF_A_Q + c for c in hm], SEG64, "a_q")
    add(0, 512, [OFF_A_K + c for c in hm], SEG64, "a_k")
    add(0, 1024, [OFF_A_V + c for c in range(A_HEADS * A_DV)], SEG_NONE, None)
    braw = ([OFF_B_CQ + c for c in range(B_Q_LORA)] + [OFF_B_CKV + c for c in range(B_KV_LORA)]
            + [OFF_B_KR + c for c in range(B_ROPE)] + [zero] * (512 - B_Q_LORA - B_KV_LORA - B_ROPE))
    add(1, 0, braw, SEG_NONE, None)
    add(2, 0, [OFF_C_Q + c for c in range(C_QH * C_HD)], SEG64, "c_q")
    dup = [kv * C_HD + d for kv in range(C_KVH) for _ in range(2) for d in range(C_HD)]
    add(2, 512, [OFF_C_K + c for c in dup], SEG64, "c_k")
    add(2, 768, [OFF_C_V + c for c in dup], SEG_NONE, None)
    for g in range(N_DIL):
        gsl = [g * D_HEADS * D_HD + c for c in range(D_HEADS * D_HD)]
        add(3 + g, 0, [OFF_D_Q + c for c in gsl], SEG64, "d_q")
        add(3 + g, 512, [OFF_D_K + c for c in gsl], SEG64, "d_k")
        add(3 + g, 1024, [OFF_D_V + c for c in gsl], SEG_NONE, None)
    add(3 + N_DIL, 0, [OFF_M_Q + c for c in range(M_HEADS * M_HD)], SEG128, "m_q")
    return np.asarray(cols, np.int32), gains, tiles, widths


_MAIN_COLS, _MAIN_GAINS, _MAIN_TILES, _MAIN_WIDTHS = _main_plan()


def _main_gain_vector(p):
    per_key = {
        "a_q": p["a_qn"] * (A_DK ** -0.5 * LOG2E), "a_k": p["a_kn"],
        "c_q": p["c_qn"] * (C_HD ** -0.5 * LOG2E), "c_k": p["c_kn"],
        "d_q": p["d_qn"] * (D_HD ** -0.5 * LOG2E), "d_k": p["d_kn"],
        "m_q": p["m_qn"] * (M_HD ** -0.5),
    }
    n = len(_MAIN_GAINS)
    pieces, k = [], 0
    while k < n:
        key = _MAIN_GAINS[k]
        k2 = k
        while k2 < n and _MAIN_GAINS[k2] == key:
            k2 += 1
        if key is None:
            pieces.append(jnp.ones((k2 - k,), F32))
        else:
            gvec = per_key[key].astype(F32)
            pieces.append(jnp.tile(gvec, (k2 - k) // gvec.shape[0]))
        k = k2
    return jnp.concatenate(pieces).reshape(1, n)


def _mla_prep_kernel(b_ref, wq_ref, wk_ref, wv_ref, place_ref, mseg_ref, cqg_ref, ckvg_ref,
                     qg_ref, kg_ref, ct_ref, s1_ref, s2_ref, q_out, k_out, v_out):
    braw = b_ref[...]
    cq = braw[:, :B_Q_LORA].astype(F32)
    cqn = ((cq * lax.rsqrt(jnp.mean(cq * cq, axis=-1, keepdims=True) + EPS)) * cqg_ref[...]).astype(BF16)
    ckv = braw[:, B_Q_LORA:B_Q_LORA + B_KV_LORA].astype(F32)
    ckvn = ((ckv * lax.rsqrt(jnp.mean(ckv * ckv, axis=-1, keepdims=True) + EPS)) * ckvg_ref[...]).astype(BF16)
    qb = jnp.dot(cqn, wq_ref[...], preferred_element_type=F32)
    kb = jnp.dot(ckvn, wk_ref[...], preferred_element_type=F32)
    kb = kb + jnp.dot(braw[:, B_Q_LORA + B_KV_LORA:], place_ref[...], preferred_element_type=F32)
    v_out[...] = jnp.dot(ckvn, wv_ref[...], preferred_element_type=F32).astype(BF16)
    ct, s1, s2 = ct_ref[...], s1_ref[...], s2_ref[...]

    def finish(t, gain_ref, out):
        for h in range(B_HEADS):
            y = t[:, h * LANES:(h + 1) * LANES]
            segms = jnp.dot((y * y).astype(BF16), mseg_ref[...], preferred_element_type=F32)
            yn = (y * lax.rsqrt(segms + EPS)) * gain_ref[...]
            half = B_ROPE // 2
            r = yn * ct + pltpu.roll(yn, half, 1) * s1 + pltpu.roll(yn, LANES - half, 1) * s2
            out[:, h * LANES:(h + 1) * LANES] = r.astype(BF16)

    finish(qb, qg_ref, q_out)
    finish(kb, kg_ref, k_out)


def _rope_tables(s):
    half = B_ROPE // 2
    inv = ROPE_THETA ** (-jnp.arange(half, dtype=F32) / half)
    ang = jnp.arange(s, dtype=F32)[:, None] * inv[None, :]
    cos, sin = jnp.cos(ang), jnp.sin(ang)
    one = jnp.ones((s, B_NOPE), F32)
    zpad = jnp.zeros((s, LANES - B_NOPE - B_ROPE), F32)
    z64 = jnp.zeros((s, B_NOPE), F32)
    zh = jnp.zeros((s, half), F32)
    ct = jnp.concatenate([one, cos, cos, zpad], axis=1)
    s1 = jnp.concatenate([z64, zh, sin, zpad], axis=1)
    s2 = jnp.concatenate([z64, -sin, zh, zpad], axis=1)
    return ct, s1, s2


def _mla_prep(braw, p, bsz, s, tm):
    t = braw.shape[0]
    tm = min(tm, s)
    pad = LANES - B_NOPE - B_ROPE
    wq = p["w_qb"].reshape(B_Q_LORA, B_HEADS, B_NOPE + B_ROPE)
    wq = jnp.pad(wq, ((0, 0), (0, 0), (0, pad))).reshape(B_Q_LORA, B_HEADS * LANES).astype(BF16)
    wkv = p["w_kvb"].reshape(B_KV_LORA, B_HEADS, B_NOPE + B_DV)
    wk = jnp.pad(wkv[:, :, :B_NOPE], ((0, 0), (0, 0), (0, LANES - B_NOPE))).reshape(B_KV_LORA, B_HEADS * LANES).astype(BF16)
    wv = wkv[:, :, B_NOPE:].reshape(B_KV_LORA, B_HEADS * B_DV).astype(BF16)
    place = np.zeros((LANES, B_HEADS * LANES), np.float32)
    for h in range(B_HEADS):
        for d in range(B_ROPE):
            place[d, h * LANES + B_NOPE + d] = 1.0
    mseg = _seg_matrix(LANES, [(0, B_NOPE), (B_NOPE, B_ROPE)])
    scale = (B_NOPE + B_ROPE) ** -0.5 * LOG2E
    zp = jnp.zeros((pad,), F32)
    qg = jnp.concatenate([p["b_qn"].astype(F32) * scale, zp]).reshape(1, LANES)
    kg = jnp.concatenate([p["b_kn"].astype(F32), zp]).reshape(1, LANES)
    ct, s1, s2 = _rope_tables(s)
    nst = s // tm
    full = lambda shape: _resident(shape, lambda b, i: tuple(0 for _ in shape))
    row = lambda wd: pl.BlockSpec((tm, wd), lambda b, i: (b * nst + i, 0))
    tab = pl.BlockSpec((tm, LANES), lambda b, i: (i, 0))
    return pl.pallas_call(
        _mla_prep_kernel,
        grid=(bsz, nst),
        in_specs=[row(512), full((B_Q_LORA, 512)), full((B_KV_LORA, 512)), full((B_KV_LORA, 512)),
                  full((LANES, 512)), full((LANES, LANES)), full((1, B_Q_LORA)), full((1, B_KV_LORA)),
                  full((1, LANES)), full((1, LANES)), tab, tab, tab],
        out_specs=[row(512), row(512), row(512)],
        out_shape=[jax.ShapeDtypeStruct((t, 512), BF16)] * 3,
        compiler_params=_cparams(("parallel", "parallel")),
        name="mla_prep",
    )(braw, wq, wk, wv, jnp.asarray(place, BF16), jnp.asarray(mseg, BF16),
      p["b_cqn"].astype(F32).reshape(1, -1), p["b_ckvn"].astype(F32).reshape(1, -1), qg, kg, ct, s1, s2)


def _flash_kernel(slopes_ref, lam_ref, hn_ref, q_ref, k_ref, v_ref, o_ref, m_sc, l_sc, acc_sc,
                  *, nmaps, alibi, tq, tk, seq, lam_init):
    h = pl.program_id(1)
    q0 = pl.program_id(2) * tq
    q = q_ref[...]
    if nmaps == 2:
        lane = lax.broadcasted_iota(jnp.int32, q.shape, 1)
        qs = [jnp.where(lane < A_DK, q, jnp.zeros_like(q)), jnp.where(lane >= A_DK, q, jnp.zeros_like(q))]
    else:
        qs = [q]
    m_sc[...] = jnp.full(m_sc.shape, NEG_INF, F32)
    l_sc[...] = jnp.zeros(l_sc.shape, F32)
    acc_sc[...] = jnp.zeros(acc_sc.shape, F32)
    if alibi:
        slope2 = slopes_ref[h] * LOG2E
        dmat = (lax.broadcasted_iota(jnp.int32, (tq, tk), 0)
                - lax.broadcasted_iota(jnp.int32, (tq, tk), 1)).astype(F32)

    def body(j, carry):
        k0 = pl.multiple_of(j * tk, tk)
        k = k_ref[pl.ds(k0, tk), :]
        v = v_ref[pl.ds(k0, tk), :]
        if alibi:
            bias = -slope2 * jnp.abs(dmat + (q0 - k0).astype(F32))
        for m in range(nmaps):
            s = lax.dot_general(qs[m], k, (((1,), (1,)), ((), ())), preferred_element_type=F32)
            if alibi:
                s = s + bias
            m_old = m_sc[m]
            m_new = jnp.maximum(m_old, jnp.max(s, axis=1, keepdims=True))
            alpha = jnp.exp2(m_old - m_new)
            pr = jnp.exp2(s - m_new)
            l_sc[m] = alpha * l_sc[m] + jnp.sum(pr, axis=1, keepdims=True)
            acc_sc[m] = alpha * acc_sc[m] + jnp.dot(pr.astype(BF16), v, preferred_element_type=F32)
            m_sc[m] = m_new
        return carry

    lax.fori_loop(0, seq // tk, body, 0)
    outs = [acc_sc[m] / l_sc[m] for m in range(nmaps)]
    o_ref[...] = _flash_epilogue(outs, lam_ref, hn_ref, lam_init).astype(BF16)


def _flash_epilogue(outs, lam_ref, hn_ref, lam_init):
    if len(outs) == 1:
        return outs[0]
    lf = lam_ref[...]
    e1 = jnp.exp(jnp.sum(lf[0:1] * lf[1:2], axis=1, keepdims=True))
    e2 = jnp.exp(jnp.sum(lf[2:3] * lf[3:4], axis=1, keepdims=True))
    lam = e1 - e2 + lam_init
    o = outs[0] - lam * outs[1]
    o = (o * lax.rsqrt(jnp.mean(o * o, axis=-1, keepdims=True) + EPS)) * hn_ref[...]
    return o * (1.0 - lam_init)


def _flash_fixed_kernel(shift_ref, slopes_ref, lam_ref, hn_ref, q_ref, k_ref, v_ref, o_ref, v1_sc, acc_sc,
                        *, nmaps, alibi, tq, tk, seq, lam_init, slopes_static):
    h = pl.program_id(1)
    i = pl.program_id(2)
    q0 = i * tq

    @pl.when(i == 0)
    def _():
        v1_sc[:, :LANES] = v_ref[...]
        v1_sc[:, LANES:] = jnp.ones((seq, LANES), BF16)

    q = q_ref[...]
    if nmaps == 2:
        lane = lax.broadcasted_iota(jnp.int32, q.shape, 1)
        qs = [jnp.where(lane < A_DK, q, jnp.zeros_like(q)), jnp.where(lane >= A_DK, q, jnp.zeros_like(q))]
    else:
        qs = [q]
    acc_sc[...] = jnp.zeros(acc_sc.shape, F32)
    shift = shift_ref[0]
    if alibi:
        slope2 = slopes_ref[h] * LOG2E
        t0 = (lax.broadcasted_iota(jnp.int32, (tq, tk), 0)
              - lax.broadcasted_iota(jnp.int32, (tq, tk), 1)).astype(F32) * slope2

    def step(j):
        k0 = j * tk if isinstance(j, int) else pl.multiple_of(j * tk, tk)
        k = k_ref[pl.ds(k0, tk), :]
        v1 = v1_sc[pl.ds(k0, tk), :]
        if alibi:
            sub = jnp.abs(t0 + slope2 * (q0 - k0).astype(F32)) + shift
        ps = []
        for m in range(nmaps):
            s = lax.dot_general(qs[m], k, (((1,), (1,)), ((), ())), preferred_element_type=F32)
            ps.append(jnp.exp2(s - sub if alibi else s - shift).astype(BF16))
        pr = ps[0] if nmaps == 1 else jnp.concatenate(ps, axis=0)
        acc_sc[...] += jnp.dot(pr, v1, preferred_element_type=F32)

    nkv = seq // tk
    if alibi:
        assert tq == tk
        for hd, slope in enumerate(slopes_static):
            reach = int(math.floor((ZERO_EXP2 / (slope * LOG2E) - 1.0) / tk)) + 1
            cnt = min(nkv, 2 * reach + 1)

            @pl.when(h == hd)
            def _(reach=reach, cnt=cnt):
                start = jnp.clip(i - reach, 0, nkv - cnt) if cnt < nkv else 0
                for jj in range(cnt):
                    step(start + jj)
    else:
        for jj in range(nkv):
            step(jj)
    outs = [acc_sc[m * tq:(m + 1) * tq, :LANES] / acc_sc[m * tq:(m + 1) * tq, LANES:] for m in range(nmaps)]
    o_ref[...] = _flash_epilogue(outs, lam_ref, hn_ref, lam_init).astype(BF16)


def _flash(qa, ka, va, qcb, kcb, vcb, bsz, s, heads, nmaps, alibi, slopes, lam, hn, lam_init, bound2):
    t = qa.shape[0]
    name = "flash_diff" if nmaps == 2 else "flash_mla"
    shift = bound2.reshape(1).astype(F32)

    def call(kern, tq, tk, scratch, extra_in, extra_args, suffix):
        tq_, tk_ = min(tq, s), min(tk, s)
        nq = s // tq_
        return pl.pallas_call(
            functools.partial(kern, nmaps=nmaps, alibi=alibi, tq=tq_, tk=tk_, seq=s, lam_init=lam_init),
            grid=(bsz, heads, nq),
            in_specs=extra_in + [
                pl.BlockSpec(memory_space=pltpu.SMEM),
                _resident(lam.shape, lambda b, h, i: (0, 0)),
                _resident(hn.shape, lambda b, h, i: (0, 0)),
                pl.BlockSpec((tq_, LANES), lambda b, h, i: (b * nq + i, qcb + h)),
                pl.BlockSpec((s, LANES), lambda b, h, i: (b, kcb + h)),
                pl.BlockSpec((s, LANES), lambda b, h, i: (b, vcb + h)),
            ],
            out_specs=pl.BlockSpec((tq_, LANES), lambda b, h, i: (b * nq + i, h)),
            out_shape=jax.ShapeDtypeStruct((t, heads * LANES), BF16),
            scratch_shapes=scratch(tq_),
            compiler_params=_cparams(("parallel", "parallel", "arbitrary")),
            name=name + suffix,
        )(*extra_args, slopes, lam, hn, qa, ka, va)

    def fixed():
        return call(functools.partial(_flash_fixed_kernel, slopes_static=_alibi_slopes_static(heads)), 512, 512,
                    lambda tq_: [pltpu.VMEM((s, 2 * LANES), BF16), pltpu.VMEM((nmaps * tq_, 2 * LANES), F32)],
                    [pl.BlockSpec(memory_space=pltpu.SMEM)], [shift], "_fixed")

    def online():
        return call(_flash_kernel, 256, 512,
                    lambda tq_: [pltpu.VMEM((nmaps, tq_, 1), F32), pltpu.VMEM((nmaps, tq_, 1), F32),
                                 pltpu.VMEM((nmaps, tq_, LANES), F32)],
                    [], [], "_online")

    return lax.cond(bound2 <= MAX_FIXED_SHIFT, fixed, online)


def _band_kernel(slopes_ref, sink_ref, q_ref, k_ref, v_ref, *outs, window, qb, kw, ut, u_len, dil, use_sink, emit_lse):
    o_ref = outs[0]
    slab = pl.program_id(0)
    ubase = pl.program_id(3) * ut
    lane = lax.broadcasted_iota(jnp.int32, (kw, LANES), 1)
    qlane = lax.broadcasted_iota(jnp.int32, (qb, LANES), 1)
    dmat = (lax.broadcasted_iota(jnp.int32, (qb, kw), 0) - lax.broadcasted_iota(jnp.int32, (qb, kw), 1))
    for sb in range(ut // qb):
        u0 = ubase + sb * qb
        start = pl.multiple_of(jnp.clip(u0 - window, 0, u_len - kw), 16)
        q = q_ref[sb * qb:(sb + 1) * qb, :]
        k = k_ref[pl.ds(start, kw), :]
        v = v_ref[pl.ds(start, kw), :]
        rel = dmat + (u0 - start)
        valid = jnp.abs(rel) <= window
        dist = jnp.abs(rel).astype(F32) * float(dil)
        o_acc = jnp.zeros((qb, LANES), F32)
        lse_acc = jnp.zeros((qb, LANES), F32)
        for hh in range(2):
            head = slab * 2 + hh
            sel_q = (qlane < 64) if hh == 0 else (qlane >= 64)
            sel_v = (lane < 64) if hh == 0 else (lane >= 64)
            qh = jnp.where(sel_q, q, jnp.zeros_like(q))
            vh = jnp.where(sel_v, v, jnp.zeros_like(v))
            s = lax.dot_general(qh, k, (((1,), (1,)), ((), ())), preferred_element_type=F32)
            logits = jnp.where(valid, s - (slopes_ref[head] * LOG2E) * dist, NEG_INF)
            mx = jnp.max(logits, axis=1, keepdims=True)
            if use_sink:
                sk = sink_ref[head]
                mx = jnp.maximum(mx, sk)
            e = jnp.exp2(logits - mx)
            den = jnp.sum(e, axis=1, keepdims=True)
            if use_sink:
                den = den + jnp.exp2(sk - mx)
            pv = jnp.dot(e.astype(BF16), vh, preferred_element_type=F32)
            o_acc = o_acc + pv / den
            if emit_lse:
                lse_acc = lse_acc + jnp.where(sel_q, mx + jnp.log2(den), 0.0)
        o_ref[sb * qb:(sb + 1) * qb, :] = o_acc.astype(BF16)
        if emit_lse:
            outs[1][sb * qb:(sb + 1) * qb, :] = lse_acc


def _band_fixed_kernel(shift_ref, slopes_ref, sink_ref, q_ref, k_ref, v_ref, *outs,
                       window, qb, kw, ut, u_len, dil, use_sink, emit_lse, sps):
    o_ref, sub_sc = outs[0], outs[-1]
    slab0 = pl.program_id(0) * sps
    ubase = pl.program_id(3) * ut
    shift = shift_ref[0]
    qlane = lax.broadcasted_iota(jnp.int32, (2 * qb, LANES), 1)
    qrow = lax.broadcasted_iota(jnp.int32, (2 * qb, LANES), 0)
    keep = (qlane < 64) == (qrow < qb)
    first = lax.broadcasted_iota(jnp.int32, (qb, LANES), 1) < 64
    ones = jnp.ones((kw, LANES), BF16)

    @pl.when((pl.program_id(1) == 0) & (pl.program_id(2) == 0) & (pl.program_id(3) == 0))
    def _():
        dmat = (lax.broadcasted_iota(jnp.int32, (qb, kw), 0) - lax.broadcasted_iota(jnp.int32, (qb, kw), 1))
        for case in range(3):
            absrel = jnp.abs(dmat + case * window)
            for hh in range(2 * sps):
                slope2 = slopes_ref[slab0 * 2 + hh] * (LOG2E * dil)
                sub_sc[case, hh // 2, (hh % 2) * qb:(hh % 2 + 1) * qb] = jnp.where(
                    absrel <= window, absrel.astype(F32) * slope2 + shift, -NEG_INF)

    for sb in range(ut // qb):
        u0 = ubase + sb * qb
        start = pl.multiple_of(jnp.clip(u0 - window, 0, u_len - kw), 16)
        case = (u0 - start) // window
        for sl in range(sps):
            lanes = slice(sl * LANES, (sl + 1) * LANES)
            q = q_ref[sb * qb:(sb + 1) * qb, lanes]
            k = k_ref[pl.ds(start, kw), lanes]
            v1 = jnp.concatenate([v_ref[pl.ds(start, kw), lanes], ones], axis=1)
            q2 = jnp.concatenate([q, q], axis=0)
            q2 = jnp.where(keep, q2, jnp.zeros_like(q2))
            s2 = lax.dot_general(q2, k, (((1,), (1,)), ((), ())), preferred_element_type=F32)
            r = jnp.dot(jnp.exp2(s2 - sub_sc[case, sl]).astype(BF16), v1, preferred_element_type=F32)
            za, zb = r[:qb, LANES:], r[qb:, LANES:]
            if use_sink:
                head = (slab0 + sl) * 2
                za = za + jnp.exp2(jnp.full((1, 1), sink_ref[head] - shift, F32))
                zb = zb + jnp.exp2(jnp.full((1, 1), sink_ref[head + 1] - shift, F32))
            o_ref[sb * qb:(sb + 1) * qb, lanes] = jnp.where(first, r[:qb, :LANES] / za, r[qb:, :LANES] / zb).astype(BF16)
            if emit_lse:
                outs[1][sb * qb:(sb + 1) * qb, lanes] = jnp.where(first, jnp.log2(za), jnp.log2(zb)) + shift


def _band(srcv, qcb, kcb, vcb, bsz, s, dil, ncb_src, window, slopes, sink, use_sink, emit_lse, ut, bound2):
    u_len = s // dil
    assert srcv.shape[0] == bsz * u_len
    qb = min(LANES, u_len)
    kw = min(qb + 2 * window, u_len)
    nslab = 4
    out_shapes = [jax.ShapeDtypeStruct((bsz * u_len, dil * 512), BF16)]
    if emit_lse:
        out_shapes.append(jax.ShapeDtypeStruct((bsz * u_len, dil * 512), F32))
    smem = pl.BlockSpec(memory_space=pltpu.SMEM)

    def call(kern, extra_in, extra_args, suffix, sps, rows, scratch=()):
        wd = sps * LANES
        ut_ = min(rows, u_len)
        nu = u_len // ut_
        assert all(f(sl) == f(0) + sl for f in (qcb, kcb, vcb) for sl in range(sps)) and ncb_src % sps == 0
        out_spec = pl.BlockSpec((ut_, wd), lambda g, b, r, i: (b * nu + i, r * (nslab // sps) + g))
        return pl.pallas_call(
            functools.partial(kern, window=window, qb=qb, kw=kw, ut=ut_, u_len=u_len, dil=dil,
                              use_sink=use_sink, emit_lse=emit_lse),
            grid=(nslab // sps, bsz, dil, nu),
            scratch_shapes=list(scratch),
            in_specs=extra_in + [
                smem, smem,
                pl.BlockSpec((ut_, wd), lambda g, b, r, i: (b * nu + i, (r * ncb_src + qcb(g * sps)) // sps)),
                pl.BlockSpec((u_len, wd), lambda g, b, r, i: (b, (r * ncb_src + kcb(g * sps)) // sps)),
                pl.BlockSpec((u_len, wd), lambda g, b, r, i: (b, (r * ncb_src + vcb(g * sps)) // sps)),
            ],
            out_specs=[out_spec] * len(out_shapes),
            out_shape=out_shapes,
            compiler_params=_cparams(("arbitrary", "arbitrary", "arbitrary", "arbitrary")),
            name="band_w%d_d%d%s" % (window, dil, suffix),
        )(*extra_args, slopes, sink, srcv, srcv, srcv)

    sps = nslab if (dil > 1 and kcb(0) % nslab == 0 and vcb(0) % nslab == 0 and qcb(0) % nslab == 0) else 1
    res = lax.cond(bound2 <= MAX_FIXED_SHIFT,
                   lambda: call(functools.partial(_band_fixed_kernel, sps=sps), [smem],
                                [bound2.reshape(1).astype(F32)], "_fixed", sps, ut // sps,
                                [pltpu.VMEM((3, sps, 2 * qb, kw), F32)]),
                   lambda: call(_band_kernel, [], [], "_online", 1, ut))
    return list(res)


def _memattn_kernel(q_ref, k_ref, v_ref, o_ref):
    for h in range(M_HEADS):
        sl = slice(h * M_HD, (h + 1) * M_HD)
        s = lax.dot_general(q_ref[:, sl], k_ref[:, sl], (((1,), (1,)), ((), ())), preferred_element_type=F32)
        mx = jnp.max(s, axis=1, keepdims=True)
        e = jnp.exp(s - mx)
        den = jnp.sum(e, axis=1, keepdims=True)
        pv = jnp.dot(e.astype(BF16), v_ref[:, sl], preferred_element_type=F32)
        o_ref[:, sl] = (pv / den).astype(BF16)


def _memattn(qm, mk, mv, bsz, s, tq):
    t = qm.shape[0]
    tq = min(tq, s)
    nq = s // tq
    wd = M_HEADS * M_HD
    return pl.pallas_call(
        _memattn_kernel,
        grid=(bsz, nq),
        in_specs=[pl.BlockSpec((tq, wd), lambda b, i: (b * nq + i, 0)),
                  pl.BlockSpec((N_MEM, wd), lambda b, i: (b, 0)),
                  pl.BlockSpec((N_MEM, wd), lambda b, i: (b, 0))],
        out_specs=pl.BlockSpec((tq, wd), lambda b, i: (b * nq + i, 0)),
        out_shape=jax.ShapeDtypeStruct((t, wd), BF16),
        compiler_params=_cparams(("parallel", "parallel")),
        name="memattn",
    )(qm, mk, mv)


def _merge_kernel(x_ref, ng_ref, wz_ref, wg_ref, bg_ref, wbr_ref, wout_ref,
                  oa_ref, ob_ref, oc_ref, om_ref, od0_ref, od1_ref, od2_ref, l0_ref, l1_ref, l2_ref,
                  y_ref, h_sc, acc_sc, o_sc, *stage, dils):
    j = pl.program_id(1)
    tm = x_ref.shape[0]
    nsl = BRANCH_W // LANES

    def token_major(ref, dil, slabs):
        if dil == 1:
            return ref[...].astype(F32)
        rows = tm // dil
        for r in range(dil):
            for c in range(nsl):
                col = r * BRANCH_W + c * LANES
                slabs[c, pl.ds(r, rows, stride=dil), :] = ref[:, col:col + LANES].astype(F32)
        return jnp.concatenate([slabs[c] for c in range(nsl)], axis=1)

    @pl.when(j == 0)
    def _():
        x = x_ref[...]
        ms = jnp.mean(x * x, axis=-1, keepdims=True)
        h_sc[...] = ((x * lax.rsqrt(ms + EPS)) * ng_ref[...]).astype(BF16)
        acc_sc[...] = jnp.zeros(acc_sc.shape, F32)
        o_sc[...] = oa_ref[...].astype(F32)

    @pl.when(j == 1)
    def _():
        o_sc[...] = ob_ref[...].astype(F32)

    @pl.when(j == 2)
    def _():
        o_sc[...] = oc_ref[...].astype(F32)

    @pl.when(j == 3)
    def _():
        st = iter(stage)
        ls = [token_major(ref, dl, next(st) if dl > 1 else None) for ref, dl in zip((l0_ref, l1_ref, l2_ref), dils)]
        os_ = [token_major(ref, dl, next(st) if dl > 1 else None) for ref, dl in zip((od0_ref, od1_ref, od2_ref), dils)]
        mx = jnp.maximum(jnp.maximum(ls[0], ls[1]), ls[2])
        e0, e1, e2 = jnp.exp2(ls[0] - mx), jnp.exp2(ls[1] - mx), jnp.exp2(ls[2] - mx)
        den = e0 + e1 + e2
        o_sc[...] = (e0 / den) * os_[0] + (e1 / den) * os_[1] + (e2 / den) * os_[2]

    @pl.when(j == 4)
    def _():
        o_sc[...] = om_ref[...].astype(F32)

    h = h_sc[...]
    z = jnp.dot(h, wz_ref[...], preferred_element_type=F32)
    g = jnp.dot(h, wg_ref[...], preferred_element_type=F32) + bg_ref[0]
    u = (o_sc[...] * (z / (1.0 + jnp.exp(-z)))).astype(BF16)
    tbr = jnp.dot(u, wbr_ref[0], preferred_element_type=F32)
    acc_sc[...] += tbr / (1.0 + jnp.exp(-g))

    @pl.when(j == N_BRANCH - 1)
    def _():
        y_ref[...] = x_ref[...] + jnp.dot(acc_sc[...].astype(BF16), wout_ref[...], preferred_element_type=F32)


def _merge(x2d, p, wz, wg, oa, ob, oc, om, ods, lses, tm):
    t, d = x2d.shape
    tm = min(tm, t)
    rowf = lambda wd: pl.BlockSpec((tm, wd), lambda i, j: (i, 0))
    dils = tuple(dl for _, dl in DIL_PAIRS)
    classf = [pl.BlockSpec((tm // dl, dl * BRANCH_W), lambda i, j: (i, 0)) for dl in dils]
    nstage = 2 * sum(1 for dl in dils if dl > 1)
    return pl.pallas_call(
        functools.partial(_merge_kernel, dils=dils),
        grid=(t // tm, N_BRANCH),
        in_specs=[
            rowf(d),
            _resident((1, d), lambda i, j: (0, 0)),
            pl.BlockSpec((d, BRANCH_W), lambda i, j: (0, j)),
            pl.BlockSpec((d, d), lambda i, j: (0, j)),
            pl.BlockSpec((1, 1, d), lambda i, j: (j, 0, 0)),
            pl.BlockSpec((1, BRANCH_W, d), lambda i, j: (j, 0, 0)),
            _resident((d, d), lambda i, j: (0, 0)),
        ] + [rowf(BRANCH_W)] * 4 + classf + classf,
        out_specs=rowf(d),
        out_shape=jax.ShapeDtypeStruct((t, d), F32),
        scratch_shapes=[pltpu.VMEM((tm, d), BF16), pltpu.VMEM((tm, d), F32), pltpu.VMEM((tm, BRANCH_W), F32)]
        + [pltpu.VMEM((BRANCH_W // LANES, tm, LANES), F32)] * nstage,
        compiler_params=_cparams(("parallel", "arbitrary")),
        name="merge",
    )(x2d, p["norm_g"].astype(F32).reshape(1, d), wz, wg, p["b_gate"].astype(F32).reshape(N_BRANCH, 1, d),
      p["w_br_bf16"], p["w_out_bf16"], oa, ob, oc, om, *ods, *lses)


def _alibi_slopes_static(n):
    return tuple(float(v) for v in 2.0 ** (-8.0 * np.arange(1, n + 1, dtype=np.float64) / n))


def _alibi_slopes(n):
    return jnp.asarray(_alibi_slopes_static(n), F32)


def _encoder_layer(x, mem, layer_idx, p):
    bsz, s, d = x.shape
    t = bsz * s
    x2d = x.reshape(t, d)

    pa, pb, pc, pd0, pd1, pd2, pm = _proj(x2d, p["norm_g"].astype(F32), p["w_main"], p["gain_main"],
                                          _MAIN_TILES, _MAIN_WIDTHS, tm=512,
                                          out_dils=[1, 1, 1] + [dl for _, dl in DIL_PAIRS] + [1])

    lam_init = 0.8 - 0.6 * math.exp(-0.3 * layer_idx)
    bound_a = (BOUND_MARGIN * A_DK * (A_DK ** -0.5 * LOG2E)
               * jnp.max(jnp.abs(p["a_qn"].astype(F32))) * jnp.max(jnp.abs(p["a_kn"].astype(F32))))
    oa = _flash(pa, pa, pa, 0, 4, 8, bsz, s, A_HEADS, 2, True, _alibi_slopes(A_HEADS),
                p["a_lam"].astype(F32), p["a_hn"].astype(F32).reshape(1, A_DV), lam_init, bound_a)

    qb, kb, vb = _mla_prep(pb, p, bsz, s, tm=512)
    dummy_lam = jnp.zeros((4, A_DK), F32)
    dummy_hn = jnp.ones((1, A_DV), F32)
    dqk = B_NOPE + B_ROPE
    bound_b = (BOUND_MARGIN * dqk * (dqk ** -0.5 * LOG2E)
               * jnp.max(jnp.abs(p["b_qn"].astype(F32))) * jnp.max(jnp.abs(p["b_kn"].astype(F32))))
    ob = _flash(qb, kb, vb, 0, 0, 0, bsz, s, B_HEADS, 1, False, _alibi_slopes(B_HEADS),
                dummy_lam, dummy_hn, 0.0, bound_b)

    sink2 = p["c_sink"].astype(F32) * LOG2E
    bound_c = jnp.maximum(
        BOUND_MARGIN * C_HD * (C_HD ** -0.5 * LOG2E)
        * jnp.max(jnp.abs(p["c_qn"].astype(F32))) * jnp.max(jnp.abs(p["c_kn"].astype(F32))),
        jnp.max(sink2))
    oc, = _band(pc, lambda sl: sl, lambda sl: 4 + sl // 2, lambda sl: 6 + sl // 2, bsz, s, 1, 8,
                C_WINDOW, _alibi_slopes(C_QH), sink2, True, False, BAND_ROWS, bound_c)

    ods, lses = [], []
    zero_sink = jnp.zeros((D_HEADS,), F32)
    bound_d = (BOUND_MARGIN * D_HD * (D_HD ** -0.5 * LOG2E)
               * jnp.max(jnp.abs(p["d_qn"].astype(F32))) * jnp.max(jnp.abs(p["d_kn"].astype(F32))))
    for g, (win, dil) in enumerate(DIL_PAIRS):
        og, lg = _band((pd0, pd1, pd2)[g], lambda sl: sl, lambda sl: 4 + sl,
                       lambda sl: 8 + sl, bsz, s, dil, 12, win // (2 * dil),
                       _alibi_slopes(D_HEADS), zero_sink, False, True, BAND_ROWS, bound_d)
        ods.append(og)
        lses.append(lg)

    mtiles = [(0, SEG128, 0, 0), (256, SEG128, 0, 256), (512, SEG_NONE, 1, 0), (768, SEG_NONE, 1, 256)]
    mk, mv = _proj(mem.reshape(bsz * N_MEM, d), p["m_norm"].astype(F32), p["w_mem_kv_bf16"], p["gain_mem"],
                   mtiles, [512, 512], tm=256)
    om = _memattn(pm, mk, mv, bsz, s, tq=512)

    y = _merge(x2d, p, p["wz"], p["wg"], oa, ob, oc, om, ods, lses, tm=512)
    return y.reshape(bsz, s, d)


def _prepare_layer(p):
    w_in = p["w_in"]
    pieces, k, n = [], 0, len(_MAIN_COLS)
    while k < n:
        k2 = k + 1
        if _MAIN_COLS[k] == N_IN:
            while k2 < n and _MAIN_COLS[k2] == N_IN:
                k2 += 1
            pieces.append(jnp.zeros((w_in.shape[0], k2 - k), BF16))
        else:
            while k2 < n and _MAIN_COLS[k2] == _MAIN_COLS[k2 - 1] + 1:
                k2 += 1
            pieces.append(w_in[:, int(_MAIN_COLS[k]):int(_MAIN_COLS[k]) + (k2 - k)].astype(BF16))
        k = k2
    return {
        "w_main": jnp.concatenate(pieces, axis=1),
        "gain_main": _main_gain_vector(p),
        "wz": w_in[:, OFF_Z:OFF_G].astype(BF16),
        "wg": w_in[:, OFF_G:].astype(BF16),
        "w_mem_kv_bf16": p["w_mem_kv"].astype(BF16),
        "gain_mem": jnp.concatenate([jnp.tile(p["m_kn"].astype(F32), M_HEADS),
                                     jnp.ones((M_HEADS * M_HD,), F32)]).reshape(1, -1),
        "w_br_bf16": p["w_br"].astype(BF16),
        "w_out_bf16": p["w_out"].astype(BF16),
    }


def kernel(x_prompt, x_sample, mem_prompt, mem_sample, norm_g, w_in, a_qn, a_kn, a_lam, a_hn, b_cqn, b_ckvn, w_qb, w_kvb, b_qn, b_kn, c_qn, c_kn, c_sink, d_qn, d_kn, m_norm, w_mem_kv, m_qn, m_kn, b_gate, w_br, w_out):
    depth = norm_g.shape[0]
    y_prompt, y_sample = x_prompt, x_sample
    for l in range(depth):
        p = {
            "norm_g": norm_g[l], "w_in": w_in[l],
            "a_qn": a_qn[l], "a_kn": a_kn[l], "a_lam": a_lam[l], "a_hn": a_hn[l],
            "b_cqn": b_cqn[l], "b_ckvn": b_ckvn[l], "w_qb": w_qb[l], "w_kvb": w_kvb[l],
            "b_qn": b_qn[l], "b_kn": b_kn[l],
            "c_qn": c_qn[l], "c_kn": c_kn[l], "c_sink": c_sink[l],
            "d_qn": d_qn[l], "d_kn": d_kn[l],
            "m_norm": m_norm[l], "w_mem_kv": w_mem_kv[l], "m_qn": m_qn[l], "m_kn": m_kn[l],
            "b_gate": b_gate[l], "w_br": w_br[l], "w_out": w_out[l],
        }
        p.update(_prepare_layer(p))
        y_prompt = _encoder_layer(y_prompt, mem_prompt, l, p)
        y_sample = _encoder_layer(y_sample, mem_sample, l, p)
    return (y_prompt, y_sample)
```

```python
import functools
import math

import numpy as np
import jax
import jax.numpy as jnp
from jax import lax
from jax.experimental import pallas as pl
from jax.experimental.pallas import tpu as pltpu

F32 = jnp.float32
BF16 = jnp.bfloat16

D_MODEL = 1024
N_MEM = 256
BRANCH_W = 512
N_BRANCH = 5
NEG_INF = -1e30
EPS = 1e-6
A_HEADS, A_DK, A_DV = 4, 64, 128
B_HEADS, B_Q_LORA, B_KV_LORA, B_NOPE, B_ROPE, B_DV = 4, 256, 128, 64, 32, 128
ROPE_THETA = 10000.0
C_QH, C_KVH, C_HD, C_WINDOW = 8, 2, 64, 128
D_HEADS, D_HD = 8, 64
DIL_PAIRS = ((128, 1), (512, 4), (2048, 16))
N_DIL = 3
M_HEADS, M_HD = 4, 128

OFF_A_Q = 0
OFF_A_K = OFF_A_Q + 2 * A_HEADS * A_DK
OFF_A_V = OFF_A_K + 2 * A_HEADS * A_DK
OFF_B_CQ = OFF_A_V + A_HEADS * A_DV
OFF_B_CKV = OFF_B_CQ + B_Q_LORA
OFF_B_KR = OFF_B_CKV + B_KV_LORA
OFF_C_Q = OFF_B_KR + B_ROPE
OFF_C_K = OFF_C_Q + C_QH * C_HD
OFF_C_V = OFF_C_K + C_KVH * C_HD
OFF_D_Q = OFF_C_V + C_KVH * C_HD
OFF_D_K = OFF_D_Q + N_DIL * D_HEADS * D_HD
OFF_D_V = OFF_D_K + N_DIL * D_HEADS * D_HD
OFF_M_Q = OFF_D_V + N_DIL * D_HEADS * D_HD
OFF_Z = OFF_M_Q + M_HEADS * M_HD
OFF_G = OFF_Z + N_BRANCH * BRANCH_W
N_IN = OFF_G + N_BRANCH * D_MODEL

LANES = 128
COL_TILE = 256
VMEM_LIMIT = 56 * 1024 * 1024

SEG_NONE, SEG64, SEG128 = -1, 0, 1

LOG2E = 1.4426950408889634
MAX_FIXED_SHIFT = 30.0 * LOG2E
BOUND_MARGIN = 1.02
PROJ_STAGE_SLABS = 4
ZERO_EXP2 = 151.0
BAND_ROWS = 2048


def _cparams(sem):
    return pltpu.CompilerParams(dimension_semantics=sem, vmem_limit_bytes=VMEM_LIMIT)


def _resident(shape, index_map):
    return pl.BlockSpec(shape, index_map, pipeline_mode=pl.Buffered(1))


def _seg_matrix(width, segs):
    m = np.zeros((width, width), np.float32)
    for s, n in segs:
        m[s:s + n, s:s + n] = 1.0 / n
    return m


def _proj_kernel(x_ref, ng_ref, w_ref, gain_ref, mseg_ref, *refs, tiles, out_widths, out_dils):
    out_refs = refs[:len(out_widths)]
    ybuf = refs[len(out_widths)] if max(out_dils) > 1 else None
    nbuf_used = 0
    x = x_ref[...]
    ms = jnp.mean(x * x, axis=-1, keepdims=True)
    h = ((x * lax.rsqrt(ms + EPS)) * ng_ref[...]).astype(BF16)
    assert len(tiles) % 2 == 0
    for pair in range(0, len(tiles), 2):
        base = tiles[pair][0]
        assert tiles[pair + 1][0] == base + COL_TILE
        y2 = jnp.dot(h, w_ref[:, base:base + 2 * COL_TILE], preferred_element_type=F32)
        for half, (c0, seg, oi, oc) in enumerate(tiles[pair:pair + 2]):
            y = y2[:, half * COL_TILE:(half + 1) * COL_TILE]
            if seg != SEG_NONE:
                sq = (y * y).astype(BF16)
                segms = jnp.dot(sq, mseg_ref[seg], preferred_element_type=F32)
                y = (y * lax.rsqrt(segms + EPS)) * gain_ref[:, c0:c0 + COL_TILE]
            dil = out_dils[oi]
            if dil == 1:
                out_refs[oi][:, oc:oc + COL_TILE] = y.astype(BF16)
                continue
            rows = y.shape[0] // dil
            for lh in range(COL_TILE // LANES):
                buf = ybuf.at[nbuf_used % PROJ_STAGE_SLABS]
                nbuf_used += 1
                buf[...] = y[:, lh * LANES:(lh + 1) * LANES]
                for r in range(dil):
                    col = r * out_widths[oi] + oc + lh * LANES
                    out_refs[oi][:, col:col + LANES] = buf[pl.ds(r, rows, stride=dil), :].astype(BF16)


def _proj(x2d, norm_gain, w, gain, tiles, out_widths, tm, out_dils=None):
    t, d = x2d.shape
    n = w.shape[1]
    tm = min(tm, t)
    out_dils = tuple(out_dils) if out_dils is not None else (1,) * len(out_widths)
    assert all(tm % (16 * dl) == 0 for dl in out_dils)
    mseg = jnp.asarray(
        np.stack([_seg_matrix(COL_TILE, [(s, 64) for s in range(0, COL_TILE, 64)]),
                  _seg_matrix(COL_TILE, [(s, 128) for s in range(0, COL_TILE, 128)])]), BF16)
    return pl.pallas_call(
        functools.partial(_proj_kernel, tiles=tuple(tiles), out_widths=tuple(out_widths), out_dils=out_dils),
        grid=(t // tm,),
        in_specs=[
            pl.BlockSpec((tm, d), lambda i: (i, 0)),
            _resident((1, d), lambda i: (0, 0)),
            _resident((d, n), lambda i: (0, 0)),
            _resident((1, n), lambda i: (0, 0)),
            _resident((2, COL_TILE, COL_TILE), lambda i: (0, 0, 0)),
        ],
        out_specs=[pl.BlockSpec((tm // dl, dl * wd), lambda i: (i, 0)) for wd, dl in zip(out_widths, out_dils)],
        out_shape=[jax.ShapeDtypeStruct((t // dl, dl * wd), BF16) for wd, dl in zip(out_widths, out_dils)],
        scratch_shapes=[pltpu.VMEM((PROJ_STAGE_SLABS, tm, LANES), F32)] if max(out_dils) > 1 else [],
        compiler_params=_cparams(("parallel",)),
        name="proj",
    )(x2d, norm_gain.reshape(1, d), w, gain, mseg)


def _main_plan():
    zero = N_IN
    cols, gains, tiles = [], [], []
    widths = [3 * 512, 512, 1024] + [3 * 512] * N_DIL + [512]

    def add(out_idx, out_col, src_cols, seg, gain_key):
        assert len(src_cols) % COL_TILE == 0
        c0 = len(cols)
        cols.extend(src_cols)
        gains.extend([gain_key] * len(src_cols))
        for k in range(len(src_cols) // COL_TILE):
            tiles.append((c0 + k * COL_TILE, seg, out_idx, out_col + k * COL_TILE))

    hm = [m * A_HEADS * A_DK + h * A_DK + d for h in range(A_HEADS) for m in range(2) for d in range(A_DK)]
    add(0, 0, [OFF_A_Q + c for c in hm], SEG64, "a_q")
    add(0, 512, [OFF_A_K + c for c in hm], SEG64, "a_k")
    add(0, 1024, [OFF_A_V + c for c in range(A_HEADS * A_DV)], SEG_NONE, None)
    braw = ([OFF_B_CQ + c for c in range(B_Q_LORA)] + [OFF_B_CKV + c for c in range(B_KV_LORA)]
            + [OFF_B_KR + c for c in range(B_ROPE)] + [zero] * (512 - B_Q_LORA - B_KV_LORA - B_ROPE))
    add(1, 0, braw, SEG_NONE, None)
    add(2, 0, [OFF_C_Q + c for c in range(C_QH * C_HD)], SEG64, "c_q")
    dup = [kv * C_HD + d for kv in range(C_KVH) for _ in range(2) for d in range(C_HD)]
    add(2, 512, [OFF_C_K + c for c in dup], SEG64, "c_k")
    add(2, 768, [OFF_C_V + c for c in dup], SEG_NONE, None)
    for g in range(N_DIL):
        gsl = [g * D_HEADS * D_HD + c for c in range(D_HEADS * D_HD)]
        add(3 + g, 0, [OFF_D_Q + c for c in gsl], SEG64, "d_q")
        add(3 + g, 512, [OFF_D_K + c for c in gsl], SEG64, "d_k")
        add(3 + g, 1024, [OFF_D_V + c for c in gsl], SEG_NONE, None)
    add(3 + N_DIL, 0, [OFF_M_Q + c for c in range(M_HEADS * M_HD)], SEG128, "m_q")
    return np.asarray(cols, np.int32), gains, tiles, widths


_MAIN_COLS, _MAIN_GAINS, _MAIN_TILES, _MAIN_WIDTHS = _main_plan()


def _main_gain_vector(p):
    per_key = {
        "a_q": p["a_qn"] * (A_DK ** -0.5 * LOG2E), "a_k": p["a_kn"],
        "c_q": p["c_qn"] * (C_HD ** -0.5 * LOG2E), "c_k": p["c_kn"],
        "d_q": p["d_qn"] * (D_HD ** -0.5 * LOG2E), "d_k": p["d_kn"],
        "m_q": p["m_qn"] * (M_HD ** -0.5),
    }
    n = len(_MAIN_GAINS)
    pieces, k = [], 0
    while k < n:
        key = _MAIN_GAINS[k]
        k2 = k
        while k2 < n and _MAIN_GAINS[k2] == key:
            k2 += 1
        if key is None:
            pieces.append(jnp.ones((k2 - k,), F32))
        else:
            gvec = per_key[key].astype(F32)
            pieces.append(jnp.tile(gvec, (k2 - k) // gvec.shape[0]))
        k = k2
    return jnp.concatenate(pieces).reshape(1, n)


def _mla_prep_kernel(b_ref, wq_ref, wk_ref, wv_ref, place_ref, mseg_ref, cqg_ref, ckvg_ref,
                     qg_ref, kg_ref, ct_ref, s1_ref, s2_ref, q_out, k_out, v_out):
    braw = b_ref[...]
    cq = braw[:, :B_Q_LORA].astype(F32)
    cqn = ((cq * lax.rsqrt(jnp.mean(cq * cq, axis=-1, keepdims=True) + EPS)) * cqg_ref[...]).astype(BF16)
    ckv = braw[:, B_Q_LORA:B_Q_LORA + B_KV_LORA].astype(F32)
    ckvn = ((ckv * lax.rsqrt(jnp.mean(ckv * ckv, axis=-1, keepdims=True) + EPS)) * ckvg_ref[...]).astype(BF16)
    qb = jnp.dot(cqn, wq_ref[...], preferred_element_type=F32)
    kb = jnp.dot(ckvn, wk_ref[...], preferred_element_type=F32)
    kb = kb + jnp.dot(braw[:, B_Q_LORA + B_KV_LORA:], place_ref[...], preferred_element_type=F32)
    v_out[...] = jnp.dot(ckvn, wv_ref[...], preferred_element_type=F32).astype(BF16)
    ct, s1, s2 = ct_ref[...], s1_ref[...], s2_ref[...]

    def finish(t, gain_ref, out):
        for h in range(B_HEADS):
            y = t[:, h * LANES:(h + 1) * LANES]
            segms = jnp.dot((y * y).astype(BF16), mseg_ref[...], preferred_element_type=F32)
            yn = (y * lax.rsqrt(segms + EPS)) * gain_ref[...]
            half = B_ROPE // 2
            r = yn * ct + pltpu.roll(yn, half, 1) * s1 + pltpu.roll(yn, LANES - half, 1) * s2
            out[:, h * LANES:(h + 1) * LANES] = r.astype(BF16)

    finish(qb, qg_ref, q_out)
    finish(kb, kg_ref, k_out)


def _rope_tables(s):
    half = B_ROPE // 2
    inv = ROPE_THETA ** (-jnp.arange(half, dtype=F32) / half)
    ang = jnp.arange(s, dtype=F32)[:, None] * inv[None, :]
    cos, sin = jnp.cos(ang), jnp.sin(ang)
    one = jnp.ones((s, B_NOPE), F32)
    zpad = jnp.zeros((s, LANES - B_NOPE - B_ROPE), F32)
    z64 = jnp.zeros((s, B_NOPE), F32)
    zh = jnp.zeros((s, half), F32)
    ct = jnp.concatenate([one, cos, cos, zpad], axis=1)
    s1 = jnp.concatenate([z64, zh, sin, zpad], axis=1)
    s2 = jnp.concatenate([z64, -sin, zh, zpad], axis=1)
    return ct, s1, s2


def _mla_prep(braw, p, bsz, s, tm):
    t = braw.shape[0]
    tm = min(tm, s)
    pad = LANES - B_NOPE - B_ROPE
    wq = p["w_qb"].reshape(B_Q_LORA, B_HEADS, B_NOPE + B_ROPE)
    wq = jnp.pad(wq, ((0, 0), (0, 0), (0, pad))).reshape(B_Q_LORA, B_HEADS * LANES).astype(BF16)
    wkv = p["w_kvb"].reshape(B_KV_LORA, B_HEADS, B_NOPE + B_DV)
    wk = jnp.pad(wkv[:, :, :B_NOPE], ((0, 0), (0, 0), (0, LANES - B_NOPE))).reshape(B_KV_LORA, B_HEADS * LANES).astype(BF16)
    wv = wkv[:, :, B_NOPE:].reshape(B_KV_LORA, B_HEADS * B_DV).astype(BF16)
    place = np.zeros((LANES, B_HEADS * LANES), np.float32)
    for h in range(B_HEADS):
        for d in range(B_ROPE):
            place[d, h * LANES + B_NOPE + d] = 1.0
    mseg = _seg_matrix(LANES, [(0, B_NOPE), (B_NOPE, B_ROPE)])
    scale = (B_NOPE + B_ROPE) ** -0.5 * LOG2E
    zp = jnp.zeros((pad,), F32)
    qg = jnp.concatenate([p["b_qn"].astype(F32) * scale, zp]).reshape(1, LANES)
    kg = jnp.concatenate([p["b_kn"].astype(F32), zp]).reshape(1, LANES)
    ct, s1, s2 = _rope_tables(s)
    nst = s // tm
    full = lambda shape: _resident(shape, lambda b, i: tuple(0 for _ in shape))
    row = lambda wd: pl.BlockSpec((tm, wd), lambda b, i: (b * nst + i, 0))
    tab = pl.BlockSpec((tm, LANES), lambda b, i: (i, 0))
    return pl.pallas_call(
        _mla_prep_kernel,
        grid=(bsz, nst),
        in_specs=[row(512), full((B_Q_LORA, 512)), full((B_KV_LORA, 512)), full((B_KV_LORA, 512)),
                  full((LANES, 512)), full((LANES, LANES)), full((1, B_Q_LORA)), full((1, B_KV_LORA)),
                  full((1, LANES)), full((1, LANES)), tab, tab, tab],
        out_specs=[row(512), row(512), row(512)],
        out_shape=[jax.ShapeDtypeStruct((t, 512), BF16)] * 3,
        compiler_params=_cparams(("parallel", "parallel")),
        name="mla_prep",
    )(braw, wq, wk, wv, jnp.asarray(place, BF16), jnp.asarray(mseg, BF16),
      p["b_cqn"].astype(F32).reshape(1, -1), p["b_ckvn"].astype(F32).reshape(1, -1), qg, kg, ct, s1, s2)


def _flash_kernel(slopes_ref, lam_ref, hn_ref, q_ref, k_ref, v_ref, o_ref, m_sc, l_sc, acc_sc,
                  *, nmaps, alibi, tq, tk, seq, lam_init):
    h = pl.program_id(1)
    q0 = pl.program_id(2) * tq
    q = q_ref[...]
    if nmaps == 2:
        lane = lax.broadcasted_iota(jnp.int32, q.shape, 1)
        qs = [jnp.where(lane < A_DK, q, jnp.zeros_like(q)), jnp.where(lane >= A_DK, q, jnp.zeros_like(q))]
    else:
        qs = [q]
    m_sc[...] = jnp.full(m_sc.shape, NEG_INF, F32)
    l_sc[...] = jnp.zeros(l_sc.shape, F32)
    acc_sc[...] = jnp.zeros(acc_sc.shape, F32)
    if alibi:
        slope2 = slopes_ref[h] * LOG2E
        dmat = (lax.broadcasted_iota(jnp.int32, (tq, tk), 0)
                - lax.broadcasted_iota(jnp.int32, (tq, tk), 1)).astype(F32)

    def body(j, carry):
        k0 = pl.multiple_of(j * tk, tk)
        k = k_ref[pl.ds(k0, tk), :]
        v = v_ref[pl.ds(k0, tk), :]
        if alibi:
            bias = -slope2 * jnp.abs(dmat + (q0 - k0).astype(F32))
        for m in range(nmaps):
            s = lax.dot_general(qs[m], k, (((1,), (1,)), ((), ())), preferred_element_type=F32)
            if alibi:
                s = s + bias
            m_old = m_sc[m]
            m_new = jnp.maximum(m_old, jnp.max(s, axis=1, keepdims=True))
            alpha = jnp.exp2(m_old - m_new)
            pr = jnp.exp2(s - m_new)
            l_sc[m] = alpha * l_sc[m] + jnp.sum(pr, axis=1, keepdims=True)
            acc_sc[m] = alpha * acc_sc[m] + jnp.dot(pr.astype(BF16), v, preferred_element_type=F32)
            m_sc[m] = m_new
        return carry

    lax.fori_loop(0, seq // tk, body, 0)
    outs = [acc_sc[m] / l_sc[m] for m in range(nmaps)]
    o_ref[...] = _flash_epilogue(outs, lam_ref, hn_ref, lam_init).astype(BF16)


def _flash_epilogue(outs, lam_ref, hn_ref, lam_init):
    if len(outs) == 1:
        return outs[0]
    lf = lam_ref[...]
    e1 = jnp.exp(jnp.sum(lf[0:1] * lf[1:2], axis=1, keepdims=True))
    e2 = jnp.exp(jnp.sum(lf[2:3] * lf[3:4], axis=1, keepdims=True))
    lam = e1 - e2 + lam_init
    o = outs[0] - lam * outs[1]
    o = (o * lax.rsqrt(jnp.mean(o * o, axis=-1, keepdims=True) + EPS)) * hn_ref[...]
    return o * (1.0 - lam_init)


def _flash_fixed_kernel(shift_ref, slopes_ref, lam_ref, hn_ref, q_ref, k_ref, v_ref, o_ref, v1_sc, acc_sc,
                        *, nmaps, alibi, tq, tk, seq, lam_init, slopes_static):
    h = pl.program_id(1)
    i = pl.program_id(2)
    q0 = i * tq

    @pl.when(i == 0)
    def _():
        v1_sc[:, :LANES] = v_ref[...]
        v1_sc[:, LANES:] = jnp.ones((seq, LANES), BF16)

    q = q_ref[...]
    if nmaps == 2:
        lane = lax.broadcasted_iota(jnp.int32, q.shape, 1)
        qs = [jnp.where(lane < A_DK, q, jnp.zeros_like(q)), jnp.where(lane >= A_DK, q, jnp.zeros_like(q))]
    else:
        qs = [q]
    acc_sc[...] = jnp.zeros(acc_sc.shape, F32)
    shift = shift_ref[0]
    if alibi:
        slope2 = slopes_ref[h] * LOG2E
        t0 = (lax.broadcasted_iota(jnp.int32, (tq, tk), 0)
              - lax.broadcasted_iota(jnp.int32, (tq, tk), 1)).astype(F32) * slope2

    def step(j):
        k0 = j * tk if isinstance(j, int) else pl.multiple_of(j * tk, tk)
        k = k_ref[pl.ds(k0, tk), :]
        v1 = v1_sc[pl.ds(k0, tk), :]
        if alibi:
            sub = jnp.abs(t0 + slope2 * (q0 - k0).astype(F32)) + shift
        ps = []
        for m in range(nmaps):
            s = lax.dot_general(qs[m], k, (((1,), (1,)), ((), ())), preferred_element_type=F32)
            ps.append(jnp.exp2(s - sub if alibi else s - shift).astype(BF16))
        pr = ps[0] if nmaps == 1 else jnp.concatenate(ps, axis=0)
        acc_sc[...] += jnp.dot(pr, v1, preferred_element_type=F32)

    nkv = seq // tk
    if alibi:
        assert tq == tk
        for hd, slope in enumerate(slopes_static):
            reach = int(math.floor((ZERO_EXP2 / (slope * LOG2E) - 1.0) / tk)) + 1
            cnt = min(nkv, 2 * reach + 1)

            @pl.when(h == hd)
            def _(reach=reach, cnt=cnt):
                start = jnp.clip(i - reach, 0, nkv - cnt) if cnt < nkv else 0
                for jj in range(cnt):
                    step(start + jj)
    else:
        for jj in range(nkv):
            step(jj)
    outs = [acc_sc[m * tq:(m + 1) * tq, :LANES] / acc_sc[m * tq:(m + 1) * tq, LANES:] for m in range(nmaps)]
    o_ref[...] = _flash_epilogue(outs, lam_ref, hn_ref, lam_init).astype(BF16)


def _flash(qa, ka, va, qcb, kcb, vcb, bsz, s, heads, nmaps, alibi, slopes, lam, hn, lam_init, bound2):
    t = qa.shape[0]
    name = "flash_diff" if nmaps == 2 else "flash_mla"
    shift = bound2.reshape(1).astype(F32)

    def call(kern, tq, tk, scratch, extra_in, extra_args, suffix):
        tq_, tk_ = min(tq, s), min(tk, s)
        nq = s // tq_
        return pl.pallas_call(
            functools.partial(kern, nmaps=nmaps, alibi=alibi, tq=tq_, tk=tk_, seq=s, lam_init=lam_init),
            grid=(bsz, heads, nq),
            in_specs=extra_in + [
                pl.BlockSpec(memory_space=pltpu.SMEM),
                _resident(lam.shape, lambda b, h, i: (0, 0)),
                _resident(hn.shape, lambda b, h, i: (0, 0)),
                pl.BlockSpec((tq_, LANES), lambda b, h, i: (b * nq + i, qcb + h)),
                pl.BlockSpec((s, LANES), lambda b, h, i: (b, kcb + h)),
                pl.BlockSpec((s, LANES), lambda b, h, i: (b, vcb + h)),
            ],
            out_specs=pl.BlockSpec((tq_, LANES), lambda b, h, i: (b * nq + i, h)),
            out_shape=jax.ShapeDtypeStruct((t, heads * LANES), BF16),
            scratch_shapes=scratch(tq_),
            compiler_params=_cparams(("parallel", "parallel", "arbitrary")),
            name=name + suffix,
        )(*extra_args, slopes, lam, hn, qa, ka, va)

    def fixed():
        return call(functools.partial(_flash_fixed_kernel, slopes_static=_alibi_slopes_static(heads)),
                    512 if alibi else 1024, 512,
                    lambda tq_: [pltpu.VMEM((s, 2 * LANES), BF16), pltpu.VMEM((nmaps * tq_, 2 * LANES), F32)],
                    [pl.BlockSpec(memory_space=pltpu.SMEM)], [shift], "_fixed")

    def online():
        return call(_flash_kernel, 256, 512,
                    lambda tq_: [pltpu.VMEM((nmaps, tq_, 1), F32), pltpu.VMEM((nmaps, tq_, 1), F32),
                                 pltpu.VMEM((nmaps, tq_, LANES), F32)],
                    [], [], "_online")

    return lax.cond(bound2 <= MAX_FIXED_SHIFT, fixed, online)


def _band_kernel(slopes_ref, sink_ref, q_ref, k_ref, v_ref, *outs, window, qb, kw, ut, u_len, dil, use_sink, emit_lse):
    o_ref = outs[0]
    slab = pl.program_id(0)
    ubase = pl.program_id(3) * ut
    lane = lax.broadcasted_iota(jnp.int32, (kw, LANES), 1)
    qlane = lax.broadcasted_iota(jnp.int32, (qb, LANES), 1)
    dmat = (lax.broadcasted_iota(jnp.int32, (qb, kw), 0) - lax.broadcasted_iota(jnp.int32, (qb, kw), 1))
    for sb in range(ut // qb):
        u0 = ubase + sb * qb
        start = pl.multiple_of(jnp.clip(u0 - window, 0, u_len - kw), 16)
        q = q_ref[sb * qb:(sb + 1) * qb, :]
        k = k_ref[pl.ds(start, kw), :]
        v = v_ref[pl.ds(start, kw), :]
        rel = dmat + (u0 - start)
        valid = jnp.abs(rel) <= window
        dist = jnp.abs(rel).astype(F32) * float(dil)
        o_acc = jnp.zeros((qb, LANES), F32)
        lse_acc = jnp.zeros((qb, LANES), F32)
        for hh in range(2):
            head = slab * 2 + hh
            sel_q = (qlane < 64) if hh == 0 else (qlane >= 64)
            sel_v = (lane < 64) if hh == 0 else (lane >= 64)
            qh = jnp.where(sel_q, q, jnp.zeros_like(q))
            vh = jnp.where(sel_v, v, jnp.zeros_like(v))
            s = lax.dot_general(qh, k, (((1,), (1,)), ((), ())), preferred_element_type=F32)
            logits = jnp.where(valid, s - (slopes_ref[head] * LOG2E) * dist, NEG_INF)
            mx = jnp.max(logits, axis=1, keepdims=True)
            if use_sink:
                sk = sink_ref[head]
                mx = jnp.maximum(mx, sk)
            e = jnp.exp2(logits - mx)
            den = jnp.sum(e, axis=1, keepdims=True)
            if use_sink:
                den = den + jnp.exp2(sk - mx)
            pv = jnp.dot(e.astype(BF16), vh, preferred_element_type=F32)
            o_acc = o_acc + pv / den
            if emit_lse:
                lse_acc = lse_acc + jnp.where(sel_q, mx + jnp.log2(den), 0.0)
        o_ref[sb * qb:(sb + 1) * qb, :] = o_acc.astype(BF16)
        if emit_lse:
            outs[1][sb * qb:(sb + 1) * qb, :] = lse_acc


def _band_fixed_kernel(shift_ref, slopes_ref, sink_ref, q_ref, k_ref, v_ref, *outs,
                       window, qb, kw, ut, u_len, dil, use_sink, emit_lse, sps):
    o_ref, sub_sc = outs[0], outs[-1]
    slab0 = pl.program_id(0) * sps
    ubase = pl.program_id(3) * ut
    shift = shift_ref[0]
    qlane = lax.broadcasted_iota(jnp.int32, (2 * qb, LANES), 1)
    qrow = lax.broadcasted_iota(jnp.int32, (2 * qb, LANES), 0)
    keep = (qlane < 64) == (qrow < qb)
    first = lax.broadcasted_iota(jnp.int32, (qb, LANES), 1) < 64
    ones = jnp.ones((kw, LANES), BF16)

    @pl.when((pl.program_id(1) == 0) & (pl.program_id(2) == 0) & (pl.program_id(3) == 0))
    def _():
        dmat = (lax.broadcasted_iota(jnp.int32, (qb, kw), 0) - lax.broadcasted_iota(jnp.int32, (qb, kw), 1))
        for case in range(3):
            absrel = jnp.abs(dmat + case * window)
            for hh in range(2 * sps):
                slope2 = slopes_ref[slab0 * 2 + hh] * (LOG2E * dil)
                sub_sc[case, hh // 2, (hh % 2) * qb:(hh % 2 + 1) * qb] = jnp.where(
                    absrel <= window, absrel.astype(F32) * slope2 + shift, -NEG_INF)

    for sb in range(ut // qb):
        u0 = ubase + sb * qb
        start = pl.multiple_of(jnp.clip(u0 - window, 0, u_len - kw), 16)
        case = (u0 - start) // window
        for sl in range(sps):
            lanes = slice(sl * LANES, (sl + 1) * LANES)
            q = q_ref[sb * qb:(sb + 1) * qb, lanes]
            k = k_ref[pl.ds(start, kw), lanes]
            v1 = jnp.concatenate([v_ref[pl.ds(start, kw), lanes], ones], axis=1)
            q2 = jnp.concatenate([q, q], axis=0)
            q2 = jnp.where(keep, q2, jnp.zeros_like(q2))
            s2 = lax.dot_general(q2, k, (((1,), (1,)), ((), ())), preferred_element_type=F32)
            r = jnp.dot(jnp.exp2(s2 - sub_sc[case, sl]).astype(BF16), v1, preferred_element_type=F32)
            za, zb = r[:qb, LANES:], r[qb:, LANES:]
            if use_sink:
                head = (slab0 + sl) * 2
                za = za + jnp.exp2(jnp.full((1, 1), sink_ref[head] - shift, F32))
                zb = zb + jnp.exp2(jnp.full((1, 1), sink_ref[head + 1] - shift, F32))
            o_ref[sb * qb:(sb + 1) * qb, lanes] = jnp.where(first, r[:qb, :LANES] / za, r[qb:, :LANES] / zb).astype(BF16)
            if emit_lse:
                outs[1][sb * qb:(sb + 1) * qb, lanes] = jnp.where(first, jnp.log2(za), jnp.log2(zb)) + shift


def _band(srcv, qcb, kcb, vcb, bsz, s, dil, ncb_src, window, slopes, sink, use_sink, emit_lse, ut, bound2):
    u_len = s // dil
    assert srcv.shape[0] == bsz * u_len
    qb = min(LANES, u_len)
    kw = min(qb + 2 * window, u_len)
    nslab = 4
    out_shapes = [jax.ShapeDtypeStruct((bsz * u_len, dil * 512), BF16)]
    if emit_lse:
        out_shapes.append(jax.ShapeDtypeStruct((bsz * u_len, dil * 512), F32))
    smem = pl.BlockSpec(memory_space=pltpu.SMEM)

    def call(kern, extra_in, extra_args, suffix, sps, rows, scratch=()):
        wd = sps * LANES
        ut_ = min(rows, u_len)
        nu = u_len // ut_
        assert all(f(sl) == f(0) + sl for f in (qcb, kcb, vcb) for sl in range(sps)) and ncb_src % sps == 0
        out_spec = pl.BlockSpec((ut_, wd), lambda g, b, r, i: (b * nu + i, r * (nslab // sps) + g))
        return pl.pallas_call(
            functools.partial(kern, window=window, qb=qb, kw=kw, ut=ut_, u_len=u_len, dil=dil,
                              use_sink=use_sink, emit_lse=emit_lse),
            grid=(nslab // sps, bsz, dil, nu),
            scratch_shapes=list(scratch),
            in_specs=extra_in + [
                smem, smem,
                pl.BlockSpec((ut_, wd), lambda g, b, r, i: (b * nu + i, (r * ncb_src + qcb(g * sps)) // sps)),
                pl.BlockSpec((u_len, wd), lambda g, b, r, i: (b, (r * ncb_src + kcb(g * sps)) // sps)),
                pl.BlockSpec((u_len, wd), lambda g, b, r, i: (b, (r * ncb_src + vcb(g * sps)) // sps)),
            ],
            out_specs=[out_spec] * len(out_shapes),
            out_shape=out_shapes,
            compiler_params=_cparams(("arbitrary", "arbitrary", "arbitrary", "arbitrary")),
            name="band_w%d_d%d%s" % (window, dil, suffix),
        )(*extra_args, slopes, sink, srcv, srcv, srcv)

    sps = nslab if (dil > 1 and kcb(0) % nslab == 0 and vcb(0) % nslab == 0 and qcb(0) % nslab == 0) else 1
    res = lax.cond(bound2 <= MAX_FIXED_SHIFT,
                   lambda: call(functools.partial(_band_fixed_kernel, sps=sps), [smem],
                                [bound2.reshape(1).astype(F32)], "_fixed", sps, ut // sps,
                                [pltpu.VMEM((3, sps, 2 * qb, kw), F32)]),
                   lambda: call(_band_kernel, [], [], "_online", 1, ut))
    return list(res)


def _memattn_kernel(q_ref, k_ref, v_ref, o_ref):
    for h in range(M_HEADS):
        sl = slice(h * M_HD, (h + 1) * M_HD)
        s = lax.dot_general(q_ref[:, sl], k_ref[:, sl], (((1,), (1,)), ((), ())), preferred_element_type=F32)
        mx = jnp.max(s, axis=1, keepdims=True)
        e = jnp.exp(s - mx)
        den = jnp.sum(e, axis=1, keepdims=True)
        pv = jnp.dot(e.astype(BF16), v_ref[:, sl], preferred_element_type=F32)
        o_ref[:, sl] = (pv / den).astype(BF16)


def _memattn(qm, mk, mv, bsz, s, tq):
    t = qm.shape[0]
    tq = min(tq, s)
    nq = s // tq
    wd = M_HEADS * M_HD
    return pl.pallas_call(
        _memattn_kernel,
        grid=(bsz, nq),
        in_specs=[pl.BlockSpec((tq, wd), lambda b, i: (b * nq + i, 0)),
                  pl.BlockSpec((N_MEM, wd), lambda b, i: (b, 0)),
                  pl.BlockSpec((N_MEM, wd), lambda b, i: (b, 0))],
        out_specs=pl.BlockSpec((tq, wd), lambda b, i: (b * nq + i, 0)),
        out_shape=jax.ShapeDtypeStruct((t, wd), BF16),
        compiler_params=_cparams(("parallel", "parallel")),
        name="memattn",
    )(qm, mk, mv)


def _merge_kernel(x_ref, ng_ref, wz_ref, wg_ref, bg_ref, wbr_ref, wout_ref,
                  oa_ref, ob_ref, oc_ref, om_ref, od0_ref, od1_ref, od2_ref, l0_ref, l1_ref, l2_ref,
                  y_ref, *stage, dils):
    tm = x_ref.shape[0]
    nsl = BRANCH_W // LANES

    def token_major(ref, dil, slabs):
        if dil == 1:
            return ref[...].astype(F32)
        rows = tm // dil
        for r in range(dil):
            for c in range(nsl):
                col = r * BRANCH_W + c * LANES
                slabs[c, pl.ds(r, rows, stride=dil), :] = ref[:, col:col + LANES].astype(F32)
        return jnp.concatenate([slabs[c] for c in range(nsl)], axis=1)

    def dilated_mixture():
        st = iter(stage)
        ls = [token_major(ref, dl, next(st) if dl > 1 else None) for ref, dl in zip((l0_ref, l1_ref, l2_ref), dils)]
        os_ = [token_major(ref, dl, next(st) if dl > 1 else None) for ref, dl in zip((od0_ref, od1_ref, od2_ref), dils)]
        mx = jnp.maximum(jnp.maximum(ls[0], ls[1]), ls[2])
        e0, e1, e2 = jnp.exp2(ls[0] - mx), jnp.exp2(ls[1] - mx), jnp.exp2(ls[2] - mx)
        den = e0 + e1 + e2
        return (e0 / den) * os_[0] + (e1 / den) * os_[1] + (e2 / den) * os_[2]

    x = x_ref[...]
    d = x.shape[1]
    ms = jnp.mean(x * x, axis=-1, keepdims=True)
    h = ((x * lax.rsqrt(ms + EPS)) * ng_ref[...]).astype(BF16)
    branch_in = (lambda: oa_ref[...].astype(F32), lambda: ob_ref[...].astype(F32), lambda: oc_ref[...].astype(F32),
                 dilated_mixture, lambda: om_ref[...].astype(F32))
    acc = jnp.zeros((tm, d), F32)
    for br in range(N_BRANCH):
        z = jnp.dot(h, wz_ref[:, br * BRANCH_W:(br + 1) * BRANCH_W], preferred_element_type=F32)
        g = jnp.dot(h, wg_ref[:, br * d:(br + 1) * d], preferred_element_type=F32) + bg_ref[br:br + 1, :]
        u = (branch_in[br]() * (z / (1.0 + jnp.exp(-z)))).astype(BF16)
        tbr = jnp.dot(u, wbr_ref[br], preferred_element_type=F32)
        acc = acc + tbr / (1.0 + jnp.exp(-g))
    y_ref[...] = x + jnp.dot(acc.astype(BF16), wout_ref[...], preferred_element_type=F32)


def _merge(x2d, p, wz, wg, oa, ob, oc, om, ods, lses, tm):
    t, d = x2d.shape
    tm = min(tm, t)
    rowf = lambda wd: pl.BlockSpec((tm, wd), lambda i: (i, 0))
    dils = tuple(dl for _, dl in DIL_PAIRS)
    assert all(tm % (16 * dl) == 0 for dl in dils)
    classf = [pl.BlockSpec((tm // dl, dl * BRANCH_W), lambda i: (i, 0)) for dl in dils]
    nstage = 2 * sum(1 for dl in dils if dl > 1)
    full = lambda shape: _resident(shape, lambda i: tuple(0 for _ in shape))
    return pl.pallas_call(
        functools.partial(_merge_kernel, dils=dils),
        grid=(t // tm,),
        in_specs=[rowf(d), full((1, d)), full((d, N_BRANCH * BRANCH_W)), full((d, N_BRANCH * d)),
                  full((N_BRANCH, d)), full((N_BRANCH, BRANCH_W, d)), full((d, d))]
        + [rowf(BRANCH_W)] * 4 + classf + classf,
        out_specs=rowf(d),
        out_shape=jax.ShapeDtypeStruct((t, d), F32),
        scratch_shapes=[pltpu.VMEM((BRANCH_W // LANES, tm, LANES), F32)] * nstage,
        compiler_params=_cparams(("parallel",)),
        name="merge",
    )(x2d, p["norm_g"].astype(F32).reshape(1, d), wz, wg, p["b_gate"].astype(F32),
      p["w_br_bf16"], p["w_out_bf16"], oa, ob, oc, om, *ods, *lses)


def _alibi_slopes_static(n):
    return tuple(float(v) for v in 2.0 ** (-8.0 * np.arange(1, n + 1, dtype=np.float64) / n))


def _alibi_slopes(n):
    return jnp.asarray(_alibi_slopes_static(n), F32)


def _encoder_layer(x, mem, layer_idx, p):
    bsz, s, d = x.shape
    t = bsz * s
    x2d = x.reshape(t, d)

    pa, pb, pc, pd0, pd1, pd2, pm = _proj(x2d, p["norm_g"].astype(F32), p["w_main"], p["gain_main"],
                                          _MAIN_TILES, _MAIN_WIDTHS, tm=512,
                                          out_dils=[1, 1, 1] + [dl for _, dl in DIL_PAIRS] + [1])

    lam_init = 0.8 - 0.6 * math.exp(-0.3 * layer_idx)
    bound_a = (BOUND_MARGIN * A_DK * (A_DK ** -0.5 * LOG2E)
               * jnp.max(jnp.abs(p["a_qn"].astype(F32))) * jnp.max(jnp.abs(p["a_kn"].astype(F32))))
    oa = _flash(pa, pa, pa, 0, 4, 8, bsz, s, A_HEADS, 2, True, _alibi_slopes(A_HEADS),
                p["a_lam"].astype(F32), p["a_hn"].astype(F32).reshape(1, A_DV), lam_init, bound_a)

    qb, kb, vb = _mla_prep(pb, p, bsz, s, tm=512)
    dummy_lam = jnp.zeros((4, A_DK), F32)
    dummy_hn = jnp.ones((1, A_DV), F32)
    dqk = B_NOPE + B_ROPE
    bound_b = (BOUND_MARGIN * dqk * (dqk ** -0.5 * LOG2E)
               * jnp.max(jnp.abs(p["b_qn"].astype(F32))) * jnp.max(jnp.abs(p["b_kn"].astype(F32))))
    ob = _flash(qb, kb, vb, 0, 0, 0, bsz, s, B_HEADS, 1, False, _alibi_slopes(B_HEADS),
                dummy_lam, dummy_hn, 0.0, bound_b)

    sink2 = p["c_sink"].astype(F32) * LOG2E
    bound_c = jnp.maximum(
        BOUND_MARGIN * C_HD * (C_HD ** -0.5 * LOG2E)
        * jnp.max(jnp.abs(p["c_qn"].astype(F32))) * jnp.max(jnp.abs(p["c_kn"].astype(F32))),
        jnp.max(sink2))
    oc, = _band(pc, lambda sl: sl, lambda sl: 4 + sl // 2, lambda sl: 6 + sl // 2, bsz, s, 1, 8,
                C_WINDOW, _alibi_slopes(C_QH), sink2, True, False, BAND_ROWS, bound_c)

    ods, lses = [], []
    zero_sink = jnp.zeros((D_HEADS,), F32)
    bound_d = (BOUND_MARGIN * D_HD * (D_HD ** -0.5 * LOG2E)
               * jnp.max(jnp.abs(p["d_qn"].astype(F32))) * jnp.max(jnp.abs(p["d_kn"].astype(F32))))
    for g, (win, dil) in enumerate(DIL_PAIRS):
        og, lg = _band((pd0, pd1, pd2)[g], lambda sl: sl, lambda sl: 4 + sl,
                       lambda sl: 8 + sl, bsz, s, dil, 12, win // (2 * dil),
                       _alibi_slopes(D_HEADS), zero_sink, False, True, BAND_ROWS, bound_d)
        ods.append(og)
        lses.append(lg)

    mtiles = [(0, SEG128, 0, 0), (256, SEG128, 0, 256), (512, SEG_NONE, 1, 0), (768, SEG_NONE, 1, 256)]
    mk, mv = _proj(mem.reshape(bsz * N_MEM, d), p["m_norm"].astype(F32), p["w_mem_kv_bf16"], p["gain_mem"],
                   mtiles, [512, 512], tm=256)
    om = _memattn(pm, mk, mv, bsz, s, tq=512)

    y = _merge(x2d, p, p["wz"], p["wg"], oa, ob, oc, om, ods, lses, tm=256)
    return y.reshape(bsz, s, d)


def _prepare_layer(p):
    w_in = p["w_in"]
    pieces, k, n = [], 0, len(_MAIN_COLS)
    while k < n:
        k2 = k + 1
        if _MAIN_COLS[k] == N_IN:
            while k2 < n and _MAIN_COLS[k2] == N_IN:
                k2 += 1
            pieces.append(jnp.zeros((w_in.shape[0], k2 - k), BF16))
        else:
            while k2 < n and _MAIN_COLS[k2] == _MAIN_COLS[k2 - 1] + 1:
                k2 += 1
            pieces.append(w_in[:, int(_MAIN_COLS[k]):int(_MAIN_COLS[k]) + (k2 - k)].astype(BF16))
        k = k2
    return {
        "w_main": jnp.concatenate(pieces, axis=1),
        "gain_main": _main_gain_vector(p),
        "wz": w_in[:, OFF_Z:OFF_G].astype(BF16),
        "wg": w_in[:, OFF_G:].astype(BF16),
        "w_mem_kv_bf16": p["w_mem_kv"].astype(BF16),
        "gain_mem": jnp.concatenate([jnp.tile(p["m_kn"].astype(F32), M_HEADS),
                                     jnp.ones((M_HEADS * M_HD,), F32)]).reshape(1, -1),
        "w_br_bf16": p["w_br"].astype(BF16),
        "w_out_bf16": p["w_out"].astype(BF16),
    }


def kernel(x_prompt, x_sample, mem_prompt, mem_sample, norm_g, w_in, a_qn, a_kn, a_lam, a_hn, b_cqn, b_ckvn, w_qb, w_kvb, b_qn, b_kn, c_qn, c_kn, c_sink, d_qn, d_kn, m_norm, w_mem_kv, m_qn, m_kn, b_gate, w_br, w_out):
    depth = norm_g.shape[0]
    y_prompt, y_sample = x_prompt, x_sample
    for l in range(depth):
        p = {
            "norm_g": norm_g[l], "w_in": w_in[l],
            "a_qn": a_qn[l], "a_kn": a_kn[l], "a_lam": a_lam[l], "a_hn": a_hn[l],
            "b_cqn": b_cqn[l], "b_ckvn": b_ckvn[l], "w_qb": w_qb[l], "w_kvb": w_kvb[l],
            "b_qn": b_qn[l], "b_kn": b_kn[l],
            "c_qn": c_qn[l], "c_kn": c_kn[l], "c_sink": c_sink[l],
            "d_qn": d_qn[l], "d_kn": d_kn[l],
            "m_norm": m_norm[l], "w_mem_kv": w_mem_kv[l], "m_qn": m_qn[l], "m_kn": m_kn[l],
            "b_gate": b_gate[l], "w_br": w_br[l], "w_out": w_out[l],
        }
        p.update(_prepare_layer(p))
        y_prompt = _encoder_layer(y_prompt, mem_prompt, l, p)
        y_sample = _encoder_layer(y_sample, mem_sample, l, p)
    return (y_prompt, y_sample)
```

```python
import functools
import math

import numpy as np
import jax
import jax.numpy as jnp
from jax import lax
from jax.experimental import pallas as pl
from jax.experimental.pallas import tpu as pltpu

F32 = jnp.float32
BF16 = jnp.bfloat16

D_MODEL = 1024
N_MEM = 256
BRANCH_W = 512
N_BRANCH = 5
NEG_INF = -1e30
EPS = 1e-6
A_HEADS, A_DK, A_DV = 4, 64, 128
B_HEADS, B_Q_LORA, B_KV_LORA, B_NOPE, B_ROPE, B_DV = 4, 256, 128, 64, 32, 128
ROPE_THETA = 10000.0
C_QH, C_KVH, C_HD, C_WINDOW = 8, 2, 64, 128
D_HEADS, D_HD = 8, 64
DIL_PAIRS = ((128, 1), (512, 4), (2048, 16))
N_DIL = 3
M_HEADS, M_HD = 4, 128

OFF_A_Q = 0
OFF_A_K = OFF_A_Q + 2 * A_HEADS * A_DK
OFF_A_V = OFF_A_K + 2 * A_HEADS * A_DK
OFF_B_CQ = OFF_A_V + A_HEADS * A_DV
OFF_B_CKV = OFF_B_CQ + B_Q_LORA
OFF_B_KR = OFF_B_CKV + B_KV_LORA
OFF_C_Q = OFF_B_KR + B_ROPE
OFF_C_K = OFF_C_Q + C_QH * C_HD
OFF_C_V = OFF_C_K + C_KVH * C_HD
OFF_D_Q = OFF_C_V + C_KVH * C_HD
OFF_D_K = OFF_D_Q + N_DIL * D_HEADS * D_HD
OFF_D_V = OFF_D_K + N_DIL * D_HEADS * D_HD
OFF_M_Q = OFF_D_V + N_DIL * D_HEADS * D_HD
OFF_Z = OFF_M_Q + M_HEADS * M_HD
OFF_G = OFF_Z + N_BRANCH * BRANCH_W
N_IN = OFF_G + N_BRANCH * D_MODEL

LANES = 128
COL_TILE = 256
VMEM_LIMIT = 56 * 1024 * 1024

SEG_NONE, SEG64, SEG128 = -1, 0, 1

LOG2E = 1.4426950408889634
MAX_FIXED_SHIFT = 30.0 * LOG2E
BOUND_MARGIN = 1.02
PROJ_STAGE_SLABS = 4
ZERO_EXP2 = 151.0
BAND_ROWS = 2048


def _cparams(sem):
    return pltpu.CompilerParams(dimension_semantics=sem, vmem_limit_bytes=VMEM_LIMIT)


def _resident(shape, index_map):
    return pl.BlockSpec(shape, index_map, pipeline_mode=pl.Buffered(1))


def _seg_matrix(width, segs):
    m = np.zeros((width, width), np.float32)
    for s, n in segs:
        m[s:s + n, s:s + n] = 1.0 / n
    return m


def _proj_kernel(x_ref, ng_ref, w_ref, gain_ref, mseg_ref, *refs, tiles, out_widths, out_dils):
    out_refs = refs[:len(out_widths)]
    ybuf = refs[len(out_widths)] if max(out_dils) > 1 else None
    nbuf_used = 0
    x = x_ref[...]
    ms = jnp.mean(x * x, axis=-1, keepdims=True)
    h = ((x * lax.rsqrt(ms + EPS)) * ng_ref[...]).astype(BF16)
    assert len(tiles) % 2 == 0
    for pair in range(0, len(tiles), 2):
        base = tiles[pair][0]
        assert tiles[pair + 1][0] == base + COL_TILE
        y2 = jnp.dot(h, w_ref[:, base:base + 2 * COL_TILE], preferred_element_type=F32)
        for half, (c0, seg, oi, oc) in enumerate(tiles[pair:pair + 2]):
            y = y2[:, half * COL_TILE:(half + 1) * COL_TILE]
            if seg != SEG_NONE:
                sq = (y * y).astype(BF16)
                segms = jnp.dot(sq, mseg_ref[seg], preferred_element_type=F32)
                y = (y * lax.rsqrt(segms + EPS)) * gain_ref[:, c0:c0 + COL_TILE]
            dil = out_dils[oi]
            if dil == 1:
                out_refs[oi][:, oc:oc + COL_TILE] = y.astype(BF16)
                continue
            rows = y.shape[0] // dil
            for lh in range(COL_TILE // LANES):
                buf = ybuf.at[nbuf_used % PROJ_STAGE_SLABS]
                nbuf_used += 1
                buf[...] = y[:, lh * LANES:(lh + 1) * LANES]
                s1 = 4 if (dil > 4 and dil % 4 == 0) else 1
                s2 = dil // s1
                if s1 > 1:
                    mid = ybuf.at[nbuf_used % PROJ_STAGE_SLABS]
                    nbuf_used += 1
                    part = y.shape[0] // s1
                    for r1 in range(s1):
                        mid[r1 * part:(r1 + 1) * part, :] = buf[pl.ds(r1, part, stride=s1), :]
                else:
                    mid, part = buf, y.shape[0]
                for r1 in range(s1):
                    for r2 in range(s2):
                        col = (r1 + s1 * r2) * out_widths[oi] + oc + lh * LANES
                        out_refs[oi][:, col:col + LANES] = mid[pl.ds(r1 * part + r2, rows, stride=s2), :].astype(BF16)


def _proj(x2d, norm_gain, w, gain, tiles, out_widths, tm, out_dils=None):
    t, d = x2d.shape
    n = w.shape[1]
    tm = min(tm, t)
    out_dils = tuple(out_dils) if out_dils is not None else (1,) * len(out_widths)
    assert all(tm % (16 * dl) == 0 for dl in out_dils)
    mseg = jnp.asarray(
        np.stack([_seg_matrix(COL_TILE, [(s, 64) for s in range(0, COL_TILE, 64)]),
                  _seg_matrix(COL_TILE, [(s, 128) for s in range(0, COL_TILE, 128)])]), BF16)
    return pl.pallas_call(
        functools.partial(_proj_kernel, tiles=tuple(tiles), out_widths=tuple(out_widths), out_dils=out_dils),
        grid=(t // tm,),
        in_specs=[
            pl.BlockSpec((tm, d), lambda i: (i, 0)),
            _resident((1, d), lambda i: (0, 0)),
            _resident((d, n), lambda i: (0, 0)),
            _resident((1, n), lambda i: (0, 0)),
            _resident((2, COL_TILE, COL_TILE), lambda i: (0, 0, 0)),
        ],
        out_specs=[pl.BlockSpec((tm // dl, dl * wd), lambda i: (i, 0)) for wd, dl in zip(out_widths, out_dils)],
        out_shape=[jax.ShapeDtypeStruct((t // dl, dl * wd), BF16) for wd, dl in zip(out_widths, out_dils)],
        scratch_shapes=[pltpu.VMEM((PROJ_STAGE_SLABS, tm, LANES), F32)] if max(out_dils) > 1 else [],
        compiler_params=_cparams(("parallel",)),
        name="proj",
    )(x2d, norm_gain.reshape(1, d), w, gain, mseg)


def _main_plan():
    zero = N_IN
    cols, gains, tiles = [], [], []
    widths = [3 * 512, 512, 1024] + [3 * 512] * N_DIL + [512]

    def add(out_idx, out_col, src_cols, seg, gain_key):
        assert len(src_cols) % COL_TILE == 0
        c0 = len(cols)
        cols.extend(src_cols)
        gains.extend([gain_key] * len(src_cols))
        for k in range(len(src_cols) // COL_TILE):
            tiles.append((c0 + k * COL_TILE, seg, out_idx, out_col + k * COL_TILE))

    hm = [m * A_HEADS * A_DK + h * A_DK + d for h in range(A_HEADS) for m in range(2) for d in range(A_DK)]
    add(0, 0, [OFF_A_Q + c for c in hm], SEG64, "a_q")
    add(0, 512, [OFF_A_K + c for c in hm], SEG64, "a_k")
    add(0, 1024, [OFF_A_V + c for c in range(A_HEADS * A_DV)], SEG_NONE, None)
    braw = ([OFF_B_CQ + c for c in range(B_Q_LORA)] + [OFF_B_CKV + c for c in range(B_KV_LORA)]
            + [zero] * B_NOPE + [OFF_B_KR + c for c in range(B_ROPE)] + [zero] * (LANES - B_NOPE - B_ROPE))
    add(1, 0, braw, SEG_NONE, None)
    add(2, 0, [OFF_C_Q + c for c in range(C_QH * C_HD)], SEG64, "c_q")
    dup = [kv * C_HD + d for kv in range(C_KVH) for _ in range(2) for d in range(C_HD)]
    add(2, 512, [OFF_C_K + c for c in dup], SEG64, "c_k")
    add(2, 768, [OFF_C_V + c for c in dup], SEG_NONE, None)
    for g in range(N_DIL):
        gsl = [g * D_HEADS * D_HD + c for c in range(D_HEADS * D_HD)]
        add(3 + g, 0, [OFF_D_Q + c for c in gsl], SEG64, "d_q")
        add(3 + g, 512, [OFF_D_K + c for c in gsl], SEG64, "d_k")
        add(3 + g, 1024, [OFF_D_V + c for c in gsl], SEG_NONE, None)
    add(3 + N_DIL, 0, [OFF_M_Q + c for c in range(M_HEADS * M_HD)], SEG128, "m_q")
    return np.asarray(cols, np.int32), gains, tiles, widths


_MAIN_COLS, _MAIN_GAINS, _MAIN_TILES, _MAIN_WIDTHS = _main_plan()


def _main_gain_vector(p):
    per_key = {
        "a_q": p["a_qn"] * (A_DK ** -0.5 * LOG2E), "a_k": p["a_kn"],
        "c_q": p["c_qn"] * (C_HD ** -0.5 * LOG2E), "c_k": p["c_kn"],
        "d_q": p["d_qn"] * (D_HD ** -0.5 * LOG2E), "d_k": p["d_kn"],
        "m_q": p["m_qn"] * (M_HD ** -0.5),
    }
    n = len(_MAIN_GAINS)
    pieces, k = [], 0
    while k < n:
        key = _MAIN_GAINS[k]
        k2 = k
        while k2 < n and _MAIN_GAINS[k2] == key:
            k2 += 1
        if key is None:
            pieces.append(jnp.ones((k2 - k,), F32))
        else:
            gvec = per_key[key].astype(F32)
            pieces.append(jnp.tile(gvec, (k2 - k) // gvec.shape[0]))
        k = k2
    return jnp.concatenate(pieces).reshape(1, n)


def _mla_prep_kernel(b_ref, wq_ref, wk_ref, wv_ref, mseg_ref, cqg_ref, ckvg_ref,
                     qg_ref, kg_ref, ct_ref, s1_ref, s2_ref, q_out, k_out, v_out):
    braw = b_ref[...]
    cq = braw[:, :B_Q_LORA].astype(F32)
    cqn = ((cq * lax.rsqrt(jnp.mean(cq * cq, axis=-1, keepdims=True) + EPS)) * cqg_ref[...]).astype(BF16)
    ckv = braw[:, B_Q_LORA:B_Q_LORA + B_KV_LORA].astype(F32)
    ckvn = ((ckv * lax.rsqrt(jnp.mean(ckv * ckv, axis=-1, keepdims=True) + EPS)) * ckvg_ref[...]).astype(BF16)
    qb = jnp.dot(cqn, wq_ref[...], preferred_element_type=F32)
    kb = jnp.dot(ckvn, wk_ref[...], preferred_element_type=F32)
    v_out[...] = jnp.dot(ckvn, wv_ref[...], preferred_element_type=F32).astype(BF16)
    ct, s1, s2 = ct_ref[...], s1_ref[...], s2_ref[...]
    half = B_ROPE // 2

    def norm_slab(y, gain_ref):
        segms = jnp.dot((y * y).astype(BF16), mseg_ref[...], preferred_element_type=F32)
        return (y * lax.rsqrt(segms + EPS)) * gain_ref[...]

    def rope(yn):
        return yn * ct + pltpu.roll(yn, half, 1) * s1 + pltpu.roll(yn, LANES - half, 1) * s2

    kr = rope(norm_slab(braw[:, B_Q_LORA + B_KV_LORA:].astype(F32), kg_ref))
    for h in range(B_HEADS):
        sl = slice(h * LANES, (h + 1) * LANES)
        k_out[:, sl] = (norm_slab(kb[:, sl], kg_ref) + kr).astype(BF16)
        q_out[:, sl] = rope(norm_slab(qb[:, sl], qg_ref)).astype(BF16)


def _rope_tables(s):
    half = B_ROPE // 2
    inv = ROPE_THETA ** (-jnp.arange(half, dtype=F32) / half)
    ang = jnp.arange(s, dtype=F32)[:, None] * inv[None, :]
    cos, sin = jnp.cos(ang), jnp.sin(ang)
    one = jnp.ones((s, B_NOPE), F32)
    zpad = jnp.zeros((s, LANES - B_NOPE - B_ROPE), F32)
    z64 = jnp.zeros((s, B_NOPE), F32)
    zh = jnp.zeros((s, half), F32)
    ct = jnp.concatenate([one, cos, cos, zpad], axis=1)
    s1 = jnp.concatenate([z64, zh, sin, zpad], axis=1)
    s2 = jnp.concatenate([z64, -sin, zh, zpad], axis=1)
    return ct, s1, s2


def _mla_prep(braw, p, bsz, s, tm):
    t = braw.shape[0]
    tm = min(tm, s)
    pad = LANES - B_NOPE - B_ROPE
    wq = p["w_qb"].reshape(B_Q_LORA, B_HEADS, B_NOPE + B_ROPE)
    wq = jnp.pad(wq, ((0, 0), (0, 0), (0, pad))).reshape(B_Q_LORA, B_HEADS * LANES).astype(BF16)
    wkv = p["w_kvb"].reshape(B_KV_LORA, B_HEADS, B_NOPE + B_DV)
    wk = jnp.pad(wkv[:, :, :B_NOPE], ((0, 0), (0, 0), (0, LANES - B_NOPE))).reshape(B_KV_LORA, B_HEADS * LANES).astype(BF16)
    wv = wkv[:, :, B_NOPE:].reshape(B_KV_LORA, B_HEADS * B_DV).astype(BF16)
    mseg =_seg_matrix(LANES, [(0, B_NOPE), (B_NOPE, B_ROPE)])
    scale = (B_NOPE + B_ROPE) ** -0.5 * LOG2E
    zp = jnp.zeros((pad,), F32)
    qg = jnp.concatenate([p["b_qn"].astype(F32) * scale, zp]).reshape(1, LANES)
    kg = jnp.concatenate([p["b_kn"].astype(F32), zp]).reshape(1, LANES)
    ct, s1, s2 = _rope_tables(s)
    nst = s // tm
    full = lambda shape: _resident(shape, lambda b, i: tuple(0 for _ in shape))
    row = lambda wd: pl.BlockSpec((tm, wd), lambda b, i: (b * nst + i, 0))
    tab = pl.BlockSpec((tm, LANES), lambda b, i: (i, 0))
    return pl.pallas_call(
        _mla_prep_kernel,
        grid=(bsz, nst),
        in_specs=[row(512), full((B_Q_LORA, 512)), full((B_KV_LORA, 512)), full((B_KV_LORA, 512)),
                  full((LANES, LANES)), full((1, B_Q_LORA)), full((1, B_KV_LORA)),
                  full((1, LANES)), full((1, LANES)), tab, tab, tab],
        out_specs=[row(512), row(512), row(512)],
        out_shape=[jax.ShapeDtypeStruct((t, 512), BF16)] * 3,
        compiler_params=_cparams(("parallel", "parallel")),
        name="mla_prep",
    )(braw, wq, wk, wv, jnp.asarray(mseg, BF16),
      p["b_cqn"].astype(F32).reshape(1, -1), p["b_ckvn"].astype(F32).reshape(1, -1), qg, kg, ct, s1, s2)


def _flash_kernel(slopes_ref, lam_ref, hn_ref, q_ref, k_ref, v_ref, o_ref, m_sc, l_sc, acc_sc,
                  *, nmaps, alibi, tq, tk, seq, lam_init):
    h = pl.program_id(1)
    q0 = pl.program_id(2) * tq
    q = q_ref[...]
    if nmaps == 2:
        lane = lax.broadcasted_iota(jnp.int32, q.shape, 1)
        qs = [jnp.where(lane < A_DK, q, jnp.zeros_like(q)), jnp.where(lane >= A_DK, q, jnp.zeros_like(q))]
    else:
        qs = [q]
    m_sc[...] = jnp.full(m_sc.shape, NEG_INF, F32)
    l_sc[...] = jnp.zeros(l_sc.shape, F32)
    acc_sc[...] = jnp.zeros(acc_sc.shape, F32)
    if alibi:
        slope2 = slopes_ref[h] * LOG2E
        dmat = (lax.broadcasted_iota(jnp.int32, (tq, tk), 0)
                - lax.broadcasted_iota(jnp.int32, (tq, tk), 1)).astype(F32)

    def body(j, carry):
        k0 = pl.multiple_of(j * tk, tk)
        k = k_ref[pl.ds(k0, tk), :]
        v = v_ref[pl.ds(k0, tk), :]
        if alibi:
            bias = -slope2 * jnp.abs(dmat + (q0 - k0).astype(F32))
        for m in range(nmaps):
            s = lax.dot_general(qs[m], k, (((1,), (1,)), ((), ())), preferred_element_type=F32)
            if alibi:
                s = s + bias
            m_old = m_sc[m]
            m_new = jnp.maximum(m_old, jnp.max(s, axis=1, keepdims=True))
            alpha = jnp.exp2(m_old - m_new)
            pr = jnp.exp2(s - m_new)
            l_sc[m] = alpha * l_sc[m] + jnp.sum(pr, axis=1, keepdims=True)
            acc_sc[m] = alpha * acc_sc[m] + jnp.dot(pr.astype(BF16), v, preferred_element_type=F32)
            m_sc[m] = m_new
        return carry

    lax.fori_loop(0, seq // tk, body, 0)
    outs = [acc_sc[m] / l_sc[m] for m in range(nmaps)]
    o_ref[...] = _flash_epilogue(outs, lam_ref, hn_ref, lam_init).astype(BF16)


def _flash_epilogue(outs, lam_ref, hn_ref, lam_init):
    if len(outs) == 1:
        return outs[0]
    lf = lam_ref[...]
    e1 = jnp.exp(jnp.sum(lf[0:1] * lf[1:2], axis=1, keepdims=True))
    e2 = jnp.exp(jnp.sum(lf[2:3] * lf[3:4], axis=1, keepdims=True))
    lam = e1 - e2 + lam_init
    o = outs[0] - lam * outs[1]
    o = (o * lax.rsqrt(jnp.mean(o * o, axis=-1, keepdims=True) + EPS)) * hn_ref[...]
    return o * (1.0 - lam_init)


def _flash_fixed_kernel(shift_ref, slopes_ref, lam_ref, hn_ref, q_ref, k_ref, v_ref, o_ref, v1_sc, acc_sc,
                        *bias_sc, nmaps, alibi, tq, tk, seq, lam_init, slopes_static):
    h = pl.program_id(1)
    i = pl.program_id(2)
    q0 = i * tq

    @pl.when(i == 0)
    def _():
        v1_sc[:, :LANES] = v_ref[...]
        v1_sc[:, LANES:] = jnp.ones((seq, LANES), BF16)
        if alibi:
            bias_sc[0][...] = (lax.broadcasted_iota(jnp.int32, (tq, tk), 0)
                               - lax.broadcasted_iota(jnp.int32, (tq, tk), 1)).astype(F32) * (slopes_ref[h] * LOG2E)

    q = q_ref[...]
    if nmaps == 2:
        lane = lax.broadcasted_iota(jnp.int32, q.shape, 1)
        qs = [jnp.where(lane < A_DK, q, jnp.zeros_like(q)), jnp.where(lane >= A_DK, q, jnp.zeros_like(q))]
    else:
        qs = [q]
    acc_sc[...] = jnp.zeros(acc_sc.shape, F32)
    shift = shift_ref[0]
    if alibi:
        slope2 = slopes_ref[h] * LOG2E
        t0_sc = bias_sc[0]

    def step(j):
        k0 = j * tk if isinstance(j, int) else pl.multiple_of(j * tk, tk)
        k = k_ref[pl.ds(k0, tk), :]
        v1 = v1_sc[pl.ds(k0, tk), :]
        if alibi:
            sub = jnp.abs(t0_sc[...] + slope2 * (q0 - k0).astype(F32)) + shift
        ps = []
        for m in range(nmaps):
            s = lax.dot_general(qs[m], k, (((1,), (1,)), ((), ())), preferred_element_type=F32)
            ps.append(jnp.exp2(s - sub if alibi else s - shift).astype(BF16))
        pr = ps[0] if nmaps == 1 else jnp.concatenate(ps, axis=0)
        acc_sc[...] += jnp.dot(pr, v1, preferred_element_type=F32)

    nkv = seq // tk
    if alibi:
        assert tq == tk
        for hd, slope in enumerate(slopes_static):
            reach = int(math.floor((ZERO_EXP2 / (slope * LOG2E) - 1.0) / tk)) + 1
            cnt = min(nkv, 2 * reach + 1)

            @pl.when(h == hd)
            def _(reach=reach, cnt=cnt):
                start = jnp.clip(i - reach, 0, nkv - cnt) if cnt < nkv else 0
                for jj in range(cnt):
                    step(start + jj)
    else:
        for jj in range(nkv):
            step(jj)
    outs = [acc_sc[m * tq:(m + 1) * tq, :LANES] / acc_sc[m * tq:(m + 1) * tq, LANES:] for m in range(nmaps)]
    o_ref[...] = _flash_epilogue(outs, lam_ref, hn_ref, lam_init).astype(BF16)


def _flash(qa, ka, va, qcb, kcb, vcb, bsz, s, heads, nmaps, alibi, slopes, lam, hn, lam_init, bound2):
    t = qa.shape[0]
    name = "flash_diff" if nmaps == 2 else "flash_mla"
    shift = bound2.reshape(1).astype(F32)

    def call(kern, tq, tk, scratch, extra_in, extra_args, suffix):
        tq_, tk_ = min(tq, s), min(tk, s)
        nq = s // tq_
        return pl.pallas_call(
            functools.partial(kern, nmaps=nmaps, alibi=alibi, tq=tq_, tk=tk_, seq=s, lam_init=lam_init),
            grid=(bsz, heads, nq),
            in_specs=extra_in + [
                pl.BlockSpec(memory_space=pltpu.SMEM),
                _resident(lam.shape, lambda b, h, i: (0, 0)),
                _resident(hn.shape, lambda b, h, i: (0, 0)),
                pl.BlockSpec((tq_, LANES), lambda b, h, i: (b * nq + i, qcb + h)),
                pl.BlockSpec((s, LANES), lambda b, h, i: (b, kcb + h)),
                pl.BlockSpec((s, LANES), lambda b, h, i: (b, vcb + h)),
            ],
            out_specs=pl.BlockSpec((tq_, LANES), lambda b, h, i: (b * nq + i, h)),
            out_shape=jax.ShapeDtypeStruct((t, heads * LANES), BF16),
            scratch_shapes=scratch(tq_),
            compiler_params=_cparams(("parallel", "parallel", "arbitrary")),
            name=name + suffix,
        )(*extra_args, slopes, lam, hn, qa, ka, va)

    def fixed():
        return call(functools.partial(_flash_fixed_kernel, slopes_static=_alibi_slopes_static(heads)),
                    512 if alibi else 1024, 512,
                    lambda tq_: [pltpu.VMEM((s, 2 * LANES), BF16), pltpu.VMEM((nmaps * tq_, 2 * LANES), F32)]
                    + ([pltpu.VMEM((tq_, min(512, s)), F32)] if alibi else []),
                    [pl.BlockSpec(memory_space=pltpu.SMEM)], [shift], "_fixed")

    def online():
        return call(_flash_kernel, 256, 512,
                    lambda tq_: [pltpu.VMEM((nmaps, tq_, 1), F32), pltpu.VMEM((nmaps, tq_, 1), F32),
                                 pltpu.VMEM((nmaps, tq_, LANES), F32)],
                    [], [], "_online")

    return lax.cond(bound2 <= MAX_FIXED_SHIFT, fixed, online)


def _band_kernel(slopes_ref, sink_ref, q_ref, k_ref, v_ref, *outs, window, qb, kw, ut, u_len, dil, use_sink, emit_lse):
    o_ref = outs[0]
    slab = pl.program_id(0)
    ubase = pl.program_id(3) * ut
    lane = lax.broadcasted_iota(jnp.int32, (kw, LANES), 1)
    qlane = lax.broadcasted_iota(jnp.int32, (qb, LANES), 1)
    dmat = (lax.broadcasted_iota(jnp.int32, (qb, kw), 0) - lax.broadcasted_iota(jnp.int32, (qb, kw), 1))
    for sb in range(ut // qb):
        u0 = ubase + sb * qb
        start = pl.multiple_of(jnp.clip(u0 - window, 0, u_len - kw), 16)
        q = q_ref[sb * qb:(sb + 1) * qb, :]
        k = k_ref[pl.ds(start, kw), :]
        v = v_ref[pl.ds(start, kw), :]
        rel = dmat + (u0 - start)
        valid = jnp.abs(rel) <= window
        dist = jnp.abs(rel).astype(F32) * float(dil)
        o_acc = jnp.zeros((qb, LANES), F32)
        lse_acc = jnp.zeros((qb, LANES), F32)
        for hh in range(2):
            head = slab * 2 + hh
            sel_q = (qlane < 64) if hh == 0 else (qlane >= 64)
            sel_v = (lane < 64) if hh == 0 else (lane >= 64)
            qh = jnp.where(sel_q, q, jnp.zeros_like(q))
            vh = jnp.where(sel_v, v, jnp.zeros_like(v))
            s = lax.dot_general(qh, k, (((1,), (1,)), ((), ())), preferred_element_type=F32)
            logits = jnp.where(valid, s - (slopes_ref[head] * LOG2E) * dist, NEG_INF)
            mx = jnp.max(logits, axis=1, keepdims=True)
            if use_sink:
                sk = sink_ref[head]
                mx = jnp.maximum(mx, sk)
            e = jnp.exp2(logits - mx)
            den = jnp.sum(e, axis=1, keepdims=True)
            if use_sink:
                den = den + jnp.exp2(sk - mx)
            pv = jnp.dot(e.astype(BF16), vh, preferred_element_type=F32)
            o_acc = o_acc + pv / den
            if emit_lse:
                lse_acc = lse_acc + jnp.where(sel_q, mx + jnp.log2(den), 0.0)
        o_ref[sb * qb:(sb + 1) * qb, :] = o_acc.astype(BF16)
        if emit_lse:
            outs[1][sb * qb:(sb + 1) * qb, :] = lse_acc


def _band_fixed_kernel(shift_ref, slopes_ref, sink_ref, q_ref, k_ref, v_ref, *outs,
                       window, qb, kw, ut, u_len, dil, use_sink, emit_lse, sps):
    o_ref, sub_sc = outs[0], outs[-1]
    slab0 = pl.program_id(0) * sps
    ubase = pl.program_id(3) * ut
    shift = shift_ref[0]
    qlane = lax.broadcasted_iota(jnp.int32, (2 * qb, LANES), 1)
    qrow = lax.broadcasted_iota(jnp.int32, (2 * qb, LANES), 0)
    keep = (qlane < 64) == (qrow < qb)
    first = lax.broadcasted_iota(jnp.int32, (qb, LANES), 1) < 64
    ones = jnp.ones((kw, LANES), BF16)

    @pl.when((pl.program_id(1) == 0) & (pl.program_id(2) == 0) & (pl.program_id(3) == 0))
    def _():
        dmat = (lax.broadcasted_iota(jnp.int32, (qb, kw), 0) - lax.broadcasted_iota(jnp.int32, (qb, kw), 1))
        for case in range(3):
            absrel = jnp.abs(dmat + case * window)
            for hh in range(2 * sps):
                slope2 = slopes_ref[slab0 * 2 + hh] * (LOG2E * dil)
                sub_sc[case, hh // 2, (hh % 2) * qb:(hh % 2 + 1) * qb] = jnp.where(
                    absrel <= window, absrel.astype(F32) * slope2 + shift, -NEG_INF)

    for sb in range(ut // qb):
        u0 = ubase + sb * qb
        start = pl.multiple_of(jnp.clip(u0 - window, 0, u_len - kw), 16)
        case = (u0 - start) // window
        for sl in range(sps):
            lanes = slice(sl * LANES, (sl + 1) * LANES)
            q = q_ref[sb * qb:(sb + 1) * qb, lanes]
            k = k_ref[pl.ds(start, kw), lanes]
            v1 = jnp.concatenate([v_ref[pl.ds(start, kw), lanes], ones], axis=1)
            q2 = jnp.concatenate([q, q], axis=0)
            q2 = jnp.where(keep, q2, jnp.zeros_like(q2))
            s2 = lax.dot_general(q2, k, (((1,), (1,)), ((), ())), preferred_element_type=F32)
            r = jnp.dot(jnp.exp2(s2 - sub_sc[case, sl]).astype(BF16), v1, preferred_element_type=F32)
            za, zb = r[:qb, LANES:], r[qb:, LANES:]
            if use_sink:
                head = (slab0 + sl) * 2
                za = za + jnp.exp2(jnp.full((1, 1), sink_ref[head] - shift, F32))
                zb = zb + jnp.exp2(jnp.full((1, 1), sink_ref[head + 1] - shift, F32))
            o_ref[sb * qb:(sb + 1) * qb, lanes] = jnp.where(first, r[:qb, :LANES] / za, r[qb:, :LANES] / zb).astype(BF16)
            if emit_lse:
                outs[1][sb * qb:(sb + 1) * qb, lanes] = jnp.where(first, jnp.log2(za), jnp.log2(zb)) + shift


def _band(srcv, qcb, kcb, vcb, bsz, s, dil, ncb_src, window, slopes, sink, use_sink, emit_lse, ut, bound2):
    u_len = s // dil
    assert srcv.shape[0] == bsz * u_len
    qb = min(LANES, u_len)
    kw = min(qb + 2 * window, u_len)
    nslab = 4
    out_shapes = [jax.ShapeDtypeStruct((bsz * u_len, dil * 512), BF16)]
    if emit_lse:
        out_shapes.append(jax.ShapeDtypeStruct((bsz * u_len, dil * 512), F32))
    smem = pl.BlockSpec(memory_space=pltpu.SMEM)

    def call(kern, extra_in, extra_args, suffix, sps, rows, scratch=()):
        wd = sps * LANES
        ut_ = min(rows, u_len)
        nu = u_len // ut_
        assert all(f(sl) == f(0) + sl for f in (qcb, kcb, vcb) for sl in range(sps)) and ncb_src % sps == 0
        out_spec = pl.BlockSpec((ut_, wd), lambda g, b, r, i: (b * nu + i, r * (nslab // sps) + g))
        return pl.pallas_call(
            functools.partial(kern, window=window, qb=qb, kw=kw, ut=ut_, u_len=u_len, dil=dil,
                              use_sink=use_sink, emit_lse=emit_lse),
            grid=(nslab // sps, bsz, dil, nu),
            scratch_shapes=list(scratch),
            in_specs=extra_in + [
                smem, smem,
                pl.BlockSpec((ut_, wd), lambda g, b, r, i: (b * nu + i, (r * ncb_src + qcb(g * sps)) // sps)),
                pl.BlockSpec((u_len, wd), lambda g, b, r, i: (b, (r * ncb_src + kcb(g * sps)) // sps)),
                pl.BlockSpec((u_len, wd), lambda g, b, r, i: (b, (r * ncb_src + vcb(g * sps)) // sps)),
            ],
            out_specs=[out_spec] * len(out_shapes),
            out_shape=out_shapes,
            compiler_params=_cparams(("arbitrary", "arbitrary", "arbitrary", "arbitrary")),
            name="band_w%d_d%d%s" % (window, dil, suffix),
        )(*extra_args, slopes, sink, srcv, srcv, srcv)

    sps = nslab if (dil > 1 and kcb(0) % nslab == 0 and vcb(0) % nslab == 0 and qcb(0) % nslab == 0) else 1
    res = lax.cond(bound2 <= MAX_FIXED_SHIFT,
                   lambda: call(functools.partial(_band_fixed_kernel, sps=sps), [smem],
                                [bound2.reshape(1).astype(F32)], "_fixed", sps, ut // sps,
                                [pltpu.VMEM((3, sps, 2 * qb, kw), F32)]),
                   lambda: call(_band_kernel, [], [], "_online", 1, ut))
    return list(res)


def _memattn_kernel(q_ref, k_ref, v_ref, o_ref):
    for h in range(M_HEADS):
        sl = slice(h * M_HD, (h + 1) * M_HD)
        s = lax.dot_general(q_ref[:, sl], k_ref[:, sl], (((1,), (1,)), ((), ())), preferred_element_type=F32)
        mx = jnp.max(s, axis=1, keepdims=True)
        e = jnp.exp(s - mx)
        den = jnp.sum(e, axis=1, keepdims=True)
        pv = jnp.dot(e.astype(BF16), v_ref[:, sl], preferred_element_type=F32)
        o_ref[:, sl] = (pv / den).astype(BF16)


def _memattn(qm, mk, mv, bsz, s, tq):
    t = qm.shape[0]
    tq = min(tq, s)
    nq = s // tq
    wd = M_HEADS * M_HD
    return pl.pallas_call(
        _memattn_kernel,
        grid=(bsz, nq),
        in_specs=[pl.BlockSpec((tq, wd), lambda b, i: (b * nq + i, 0)),
                  pl.BlockSpec((N_MEM, wd), lambda b, i: (b, 0)),
                  pl.BlockSpec((N_MEM, wd), lambda b, i: (b, 0))],
        out_specs=pl.BlockSpec((tq, wd), lambda b, i: (b * nq + i, 0)),
        out_shape=jax.ShapeDtypeStruct((t, wd), BF16),
        compiler_params=_cparams(("parallel", "parallel")),
        name="memattn",
    )(qm, mk, mv)


def _merge_kernel(x_ref, ng_ref, wz_ref, wg_ref, bg_ref, wbr_ref, wout_ref,
                  oa_ref, ob_ref, oc_ref, om_ref, od0_ref, od1_ref, od2_ref, l0_ref, l1_ref, l2_ref,
                  y_ref, *stage, dils):
    tm = x_ref.shape[0]
    nsl = BRANCH_W // LANES

    def token_major(ref, dil, slabs):
        if dil == 1:
            return ref[...].astype(F32)
        rows = tm // dil
        for r in range(dil):
            for c in range(nsl):
                col = r * BRANCH_W + c * LANES
                slabs[c, pl.ds(r, rows, stride=dil), :] = ref[:, col:col + LANES].astype(F32)
        return jnp.concatenate([slabs[c] for c in range(nsl)], axis=1)

    def dilated_mixture():
        st = iter(stage)
        ls = [token_major(ref, dl, next(st) if dl > 1 else None) for ref, dl in zip((l0_ref, l1_ref, l2_ref), dils)]
        os_ = [token_major(ref, dl, next(st) if dl > 1 else None) for ref, dl in zip((od0_ref, od1_ref, od2_ref), dils)]
        mx = jnp.maximum(jnp.maximum(ls[0], ls[1]), ls[2])
        e0, e1, e2 = jnp.exp2(ls[0] - mx), jnp.exp2(ls[1] - mx), jnp.exp2(ls[2] - mx)
        den = e0 + e1 + e2
        return (e0 / den) * os_[0] + (e1 / den) * os_[1] + (e2 / den) * os_[2]

    x = x_ref[...]
    d = x.shape[1]
    ms = jnp.mean(x * x, axis=-1, keepdims=True)
    h = ((x * lax.rsqrt(ms + EPS)) * ng_ref[...]).astype(BF16)
    branch_in = (lambda: oa_ref[...].astype(F32), lambda: ob_ref[...].astype(F32), lambda: oc_ref[...].astype(F32),
                 dilated_mixture, lambda: om_ref[...].astype(F32))
    acc = jnp.zeros((tm, d), F32)
    for br in range(N_BRANCH):
        z = jnp.dot(h, wz_ref[:, br * BRANCH_W:(br + 1) * BRANCH_W], preferred_element_type=F32)
        g = jnp.dot(h, wg_ref[:, br * d:(br + 1) * d], preferred_element_type=F32) + bg_ref[br:br + 1, :]
        u = (branch_in[br]() * (z / (1.0 + jnp.exp(-z)))).astype(BF16)
        tbr = jnp.dot(u, wbr_ref[br], preferred_element_type=F32)
        acc = acc + tbr / (1.0 + jnp.exp(-g))
    y_ref[...] = x + jnp.dot(acc.astype(BF16), wout_ref[...], preferred_element_type=F32)


def _merge(x2d, p, wz, wg, oa, ob, oc, om, ods, lses, tm):
    t, d = x2d.shape
    tm = min(tm, t)
    rowf = lambda wd: pl.BlockSpec((tm, wd), lambda i: (i, 0))
    dils = tuple(dl for _, dl in DIL_PAIRS)
    assert all(tm % (16 * dl) == 0 for dl in dils)
    classf = [pl.BlockSpec((tm // dl, dl * BRANCH_W), lambda i: (i, 0)) for dl in dils]
    nstage = 2 * sum(1 for dl in dils if dl > 1)
    full = lambda shape: _resident(shape, lambda i: tuple(0 for _ in shape))
    return pl.pallas_call(
        functools.partial(_merge_kernel, dils=dils),
        grid=(t // tm,),
        in_specs=[rowf(d), full((1, d)), full((d, N_BRANCH * BRANCH_W)), full((d, N_BRANCH * d)),
                  full((N_BRANCH, d)), full((N_BRANCH, BRANCH_W, d)), full((d, d))]
        + [rowf(BRANCH_W)] * 4 + classf + classf,
        out_specs=rowf(d),
        out_shape=jax.ShapeDtypeStruct((t, d), F32),
        scratch_shapes=[pltpu.VMEM((BRANCH_W // LANES, tm, LANES), F32)] * nstage,
        compiler_params=_cparams(("parallel",)),
        name="merge",
    )(x2d, p["norm_g"].astype(F32).reshape(1, d), wz, wg, p["b_gate"].astype(F32),
      p["w_br_bf16"], p["w_out_bf16"], oa, ob, oc, om, *ods, *lses)


def _alibi_slopes_static(n):
    return tuple(float(v) for v in 2.0 ** (-8.0 * np.arange(1, n + 1, dtype=np.float64) / n))


def _alibi_slopes(n):
    return jnp.asarray(_alibi_slopes_static(n), F32)


def _encoder_layer(x, mem, layer_idx, p):
    bsz, s, d = x.shape
    t = bsz * s
    x2d = x.reshape(t, d)

    pa, pb, pc, pd0, pd1, pd2, pm = _proj(x2d, p["norm_g"].astype(F32), p["w_main"], p["gain_main"],
                                          _MAIN_TILES, _MAIN_WIDTHS, tm=512,
                                          out_dils=[1, 1, 1] + [dl for _, dl in DIL_PAIRS] + [1])

    lam_init = 0.8 - 0.6 * math.exp(-0.3 * layer_idx)
    bound_a = (BOUND_MARGIN * A_DK * (A_DK ** -0.5 * LOG2E)
               * jnp.max(jnp.abs(p["a_qn"].astype(F32))) * jnp.max(jnp.abs(p["a_kn"].astype(F32))))
    oa = _flash(pa, pa, pa, 0, 4, 8, bsz, s, A_HEADS, 2, True, _alibi_slopes(A_HEADS),
                p["a_lam"].astype(F32), p["a_hn"].astype(F32).reshape(1, A_DV), lam_init, bound_a)

    qb, kb, vb = _mla_prep(pb, p, bsz, s, tm=512)
    dummy_lam = jnp.zeros((4, A_DK), F32)
    dummy_hn = jnp.ones((1, A_DV), F32)
    dqk = B_NOPE + B_ROPE
    bound_b = (BOUND_MARGIN * dqk * (dqk ** -0.5 * LOG2E)
               * jnp.max(jnp.abs(p["b_qn"].astype(F32))) * jnp.max(jnp.abs(p["b_kn"].astype(F32))))
    ob = _flash(qb, kb, vb, 0, 0, 0, bsz, s, B_HEADS, 1, False, _alibi_slopes(B_HEADS),
                dummy_lam, dummy_hn, 0.0, bound_b)

    sink2 = p["c_sink"].astype(F32) * LOG2E
    bound_c = jnp.maximum(
        BOUND_MARGIN * C_HD * (C_HD ** -0.5 * LOG2E)
        * jnp.max(jnp.abs(p["c_qn"].astype(F32))) * jnp.max(jnp.abs(p["c_kn"].astype(F32))),
        jnp.max(sink2))
    oc, = _band(pc, lambda sl: sl, lambda sl: 4 + sl // 2, lambda sl: 6 + sl // 2, bsz, s, 1, 8,
                C_WINDOW, _alibi_slopes(C_QH), sink2, True, False, BAND_ROWS, bound_c)

    ods, lses = [], []
    zero_sink = jnp.zeros((D_HEADS,), F32)
    bound_d = (BOUND_MARGIN * D_HD * (D_HD ** -0.5 * LOG2E)
               * jnp.max(jnp.abs(p["d_qn"].astype(F32))) * jnp.max(jnp.abs(p["d_kn"].astype(F32))))
    for g, (win, dil) in enumerate(DIL_PAIRS):
        og, lg = _band((pd0, pd1, pd2)[g], lambda sl: sl, lambda sl: 4 + sl,
                       lambda sl: 8 + sl, bsz, s, dil, 12, win // (2 * dil),
                       _alibi_slopes(D_HEADS), zero_sink, False, True, BAND_ROWS, bound_d)
        ods.append(og)
        lses.append(lg)

    mtiles = [(0, SEG128, 0, 0), (256, SEG128, 0, 256), (512, SEG_NONE, 1, 0), (768, SEG_NONE, 1, 256)]
    mk, mv = _proj(mem.reshape(bsz * N_MEM, d), p["m_norm"].astype(F32), p["w_mem_kv_bf16"], p["gain_mem"],
                   mtiles, [512, 512], tm=256)
    om = _memattn(pm, mk, mv, bsz, s, tq=512)

    y = _merge(x2d, p, p["wz"], p["wg"], oa, ob, oc, om, ods, lses, tm=256)
    return y.reshape(bsz, s, d)


def _prepare_layer(p):
    w_in = p["w_in"]
    pieces, k, n = [], 0, len(_MAIN_COLS)
    while k < n:
        k2 = k + 1
        if _MAIN_COLS[k] == N_IN:
            while k2 < n and _MAIN_COLS[k2] == N_IN:
                k2 += 1
            pieces.append(jnp.zeros((w_in.shape[0], k2 - k), BF16))
        else:
            while k2 < n and _MAIN_COLS[k2] == _MAIN_COLS[k2 - 1] + 1:
                k2 += 1
            pieces.append(w_in[:, int(_MAIN_COLS[k]):int(_MAIN_COLS[k]) + (k2 - k)].astype(BF16))
        k = k2
    return {
        "w_main": jnp.concatenate(pieces, axis=1),
        "gain_main": _main_gain_vector(p),
        "wz": w_in[:, OFF_Z:OFF_G].astype(BF16),
        "wg": w_in[:, OFF_G:].astype(BF16),
        "w_mem_kv_bf16": p["w_mem_kv"].astype(BF16),
        "gain_mem": jnp.concatenate([jnp.tile(p["m_kn"].astype(F32), M_HEADS),
                                     jnp.ones((M_HEADS * M_HD,), F32)]).reshape(1, -1),
        "w_br_bf16": p["w_br"].astype(BF16),
        "w_out_bf16": p["w_out"].astype(BF16),
    }


def kernel(x_prompt, x_sample, mem_prompt, mem_sample, norm_g, w_in, a_qn, a_kn, a_lam, a_hn, b_cqn, b_ckvn, w_qb, w_kvb, b_qn, b_kn, c_qn, c_kn, c_sink, d_qn, d_kn, m_norm, w_mem_kv, m_qn, m_kn, b_gate, w_br, w_out):
    depth = norm_g.shape[0]
    y_prompt, y_sample = x_prompt, x_sample
    for l in range(depth):
        p = {
            "norm_g": norm_g[l], "w_in": w_in[l],
            "a_qn": a_qn[l], "a_kn": a_kn[l], "a_lam": a_lam[l], "a_hn": a_hn[l],
            "b_cqn": b_cqn[l], "b_ckvn": b_ckvn[l], "w_qb": w_qb[l], "w_kvb": w_kvb[l],
            "b_qn": b_qn[l], "b_kn": b_kn[l],
            "c_qn": c_qn[l], "c_kn": c_kn[l], "c_sink": c_sink[l],
            "d_qn": d_qn[l], "d_kn": d_kn[l],
            "m_norm": m_norm[l], "w_mem_kv": w_mem_kv[l], "m_qn": m_qn[l], "m_kn": m_kn[l],
            "b_gate": b_gate[l], "w_br": w_br[l], "w_out": w_out[l],
        }
        p.update(_prepare_layer(p))
        y_prompt = _encoder_layer(y_prompt, mem_prompt, l, p)
        y_sample = _encoder_layer(y_sample, mem_sample, l, p)
    return (y_prompt, y_sample)
```

```python
import functools
import math

import numpy as np
import jax
import jax.numpy as jnp
from jax import lax
from jax.experimental import pallas as pl
from jax.experimental.pallas import tpu as pltpu

F32 = jnp.float32
BF16 = jnp.bfloat16

D_MODEL = 1024
N_MEM = 256
BRANCH_W = 512
N_BRANCH = 5
NEG_INF = -1e30
EPS = 1e-6
A_HEADS, A_DK, A_DV = 4, 64, 128
B_HEADS, B_Q_LORA, B_KV_LORA, B_NOPE, B_ROPE, B_DV = 4, 256, 128, 64, 32, 128
ROPE_THETA = 10000.0
C_QH, C_KVH, C_HD, C_WINDOW = 8, 2, 64, 128
D_HEADS, D_HD = 8, 64
DIL_PAIRS = ((128, 1), (512, 4), (2048, 16))
N_DIL = 3
M_HEADS, M_HD = 4, 128

OFF_A_Q = 0
OFF_A_K = OFF_A_Q + 2 * A_HEADS * A_DK
OFF_A_V = OFF_A_K + 2 * A_HEADS * A_DK
OFF_B_CQ = OFF_A_V + A_HEADS * A_DV
OFF_B_CKV = OFF_B_CQ + B_Q_LORA
OFF_B_KR = OFF_B_CKV + B_KV_LORA
OFF_C_Q = OFF_B_KR + B_ROPE
OFF_C_K = OFF_C_Q + C_QH * C_HD
OFF_C_V = OFF_C_K + C_KVH * C_HD
OFF_D_Q = OFF_C_V + C_KVH * C_HD
OFF_D_K = OFF_D_Q + N_DIL * D_HEADS * D_HD
OFF_D_V = OFF_D_K + N_DIL * D_HEADS * D_HD
OFF_M_Q = OFF_D_V + N_DIL * D_HEADS * D_HD
OFF_Z = OFF_M_Q + M_HEADS * M_HD
OFF_G = OFF_Z + N_BRANCH * BRANCH_W
N_IN = OFF_G + N_BRANCH * D_MODEL

LANES = 128
COL_TILE = 256
VMEM_LIMIT = 56 * 1024 * 1024

SEG_NONE, SEG64, SEG128 = -1, 0, 1

LOG2E = 1.4426950408889634
MAX_FIXED_SHIFT = 30.0 * LOG2E
BOUND_MARGIN = 1.02
PROJ_STAGE_SLABS = 4
ZERO_EXP2 = 151.0
BAND_ROWS = 8192
MERGE_SUB_ROWS = 256


def _cparams(sem):
    return pltpu.CompilerParams(dimension_semantics=sem, vmem_limit_bytes=VMEM_LIMIT)


def _resident(shape, index_map):
    return pl.BlockSpec(shape, index_map, pipeline_mode=pl.Buffered(1))


def _seg_matrix(width, segs):
    m = np.zeros((width, width), np.float32)
    for s, n in segs:
        m[s:s + n, s:s + n] = 1.0 / n
    return m


def _proj_kernel(x_ref, ng_ref, w_ref, gain_ref, mseg_ref, *refs, tiles, out_widths, out_dils):
    out_refs = refs[:len(out_widths)]
    ybuf = refs[len(out_widths)] if max(out_dils) > 1 else None
    nbuf_used = 0
    x = x_ref[...]
    ms = jnp.mean(x * x, axis=-1, keepdims=True)
    h = ((x * lax.rsqrt(ms + EPS)) * ng_ref[...]).astype(BF16)
    assert len(tiles) % 2 == 0
    for pair in range(0, len(tiles), 2):
        base = tiles[pair][0]
        assert tiles[pair + 1][0] == base + COL_TILE
        y2 = jnp.dot(h, w_ref[:, base:base + 2 * COL_TILE], preferred_element_type=F32)
        for half, (c0, seg, oi, oc) in enumerate(tiles[pair:pair + 2]):
            y = y2[:, half * COL_TILE:(half + 1) * COL_TILE]
            if seg != SEG_NONE:
                sq = (y * y).astype(BF16)
                segms = jnp.dot(sq, mseg_ref[seg], preferred_element_type=F32)
                y = (y * lax.rsqrt(segms + EPS)) * gain_ref[:, c0:c0 + COL_TILE]
            dil = out_dils[oi]
            if dil == 1:
                out_refs[oi][:, oc:oc + COL_TILE] = y.astype(BF16)
                continue
            rows = y.shape[0] // dil
            for lh in range(COL_TILE // LANES):
                buf = ybuf.at[nbuf_used % PROJ_STAGE_SLABS]
                nbuf_used += 1
                buf[...] = y[:, lh * LANES:(lh + 1) * LANES]
                s1 = 4 if (dil > 4 and dil % 4 == 0) else 1
                s2 = dil // s1
                if s1 > 1:
                    mid = ybuf.at[nbuf_used % PROJ_STAGE_SLABS]
                    nbuf_used += 1
                    part = y.shape[0] // s1
                    for r1 in range(s1):
                        mid[r1 * part:(r1 + 1) * part, :] = buf[pl.ds(r1, part, stride=s1), :]
                else:
                    mid, part = buf, y.shape[0]
                for r1 in range(s1):
                    for r2 in range(s2):
                        col = (r1 + s1 * r2) * out_widths[oi] + oc + lh * LANES
                        out_refs[oi][:, col:col + LANES] = mid[pl.ds(r1 * part + r2, rows, stride=s2), :].astype(BF16)


def _proj(x2d, norm_gain, w, gain, tiles, out_widths, tm, out_dils=None):
    t, d = x2d.shape
    n = w.shape[1]
    tm = min(tm, t)
    out_dils = tuple(out_dils) if out_dils is not None else (1,) * len(out_widths)
    assert all(tm % (16 * dl) == 0 for dl in out_dils)
    mseg = jnp.asarray(
        np.stack([_seg_matrix(COL_TILE, [(s, 64) for s in range(0, COL_TILE, 64)]),
                  _seg_matrix(COL_TILE, [(s, 128) for s in range(0, COL_TILE, 128)])]), BF16)
    return pl.pallas_call(
        functools.partial(_proj_kernel, tiles=tuple(tiles), out_widths=tuple(out_widths), out_dils=out_dils),
        grid=(t // tm,),
        in_specs=[
            pl.BlockSpec((tm, d), lambda i: (i, 0)),
            _resident((1, d), lambda i: (0, 0)),
            _resident((d, n), lambda i: (0, 0)),
            _resident((1, n), lambda i: (0, 0)),
            _resident((2, COL_TILE, COL_TILE), lambda i: (0, 0, 0)),
        ],
        out_specs=[pl.BlockSpec((tm // dl, dl * wd), lambda i: (i, 0)) for wd, dl in zip(out_widths, out_dils)],
        out_shape=[jax.ShapeDtypeStruct((t // dl, dl * wd), BF16) for wd, dl in zip(out_widths, out_dils)],
        scratch_shapes=[pltpu.VMEM((PROJ_STAGE_SLABS, tm, LANES), F32)] if max(out_dils) > 1 else [],
        compiler_params=_cparams(("parallel",)),
        name="proj",
    )(x2d, norm_gain.reshape(1, d), w, gain, mseg)


def _main_plan():
    zero = N_IN
    cols, gains, tiles = [], [], []
    widths = [3 * 512, 512, 1024] + [3 * 512] * N_DIL + [512]

    def add(out_idx, out_col, src_cols, seg, gain_key):
        assert len(src_cols) % COL_TILE == 0
        c0 = len(cols)
        cols.extend(src_cols)
        gains.extend([gain_key] * len(src_cols))
        for k in range(len(src_cols) // COL_TILE):
            tiles.append((c0 + k * COL_TILE, seg, out_idx, out_col + k * COL_TILE))

    hm = [m * A_HEADS * A_DK + h * A_DK + d for h in range(A_HEADS) for m in range(2) for d in range(A_DK)]
    add(0, 0, [OFF_A_Q + c for c in hm], SEG64, "a_q")
    add(0, 512, [OFF_A_K + c for c in hm], SEG64, "a_k")
    add(0, 1024, [OFF_A_V + c for c in range(A_HEADS * A_DV)], SEG_NONE, None)
    braw = ([OFF_B_CQ + c for c in range(B_Q_LORA)] + [OFF_B_CKV + c for c in range(B_KV_LORA)]
            + [zero] * B_NOPE + [OFF_B_KR + c for c in range(B_ROPE)] + [zero] * (LANES - B_NOPE - B_ROPE))
    add(1, 0, braw, SEG_NONE, None)
    add(2, 0, [OFF_C_Q + c for c in range(C_QH * C_HD)], SEG64, "c_q")
    dup = [kv * C_HD + d for kv in range(C_KVH) for _ in range(2) for d in range(C_HD)]
    add(2, 512, [OFF_C_K + c for c in dup], SEG64, "c_k")
    add(2, 768, [OFF_C_V + c for c in dup], SEG_NONE, None)
    for g in range(N_DIL):
        gsl = [g * D_HEADS * D_HD + c for c in range(D_HEADS * D_HD)]
        add(3 + g, 0, [OFF_D_Q + c for c in gsl], SEG64, "d_q")
        add(3 + g, 512, [OFF_D_K + c for c in gsl], SEG64, "d_k")
        add(3 + g, 1024, [OFF_D_V + c for c in gsl], SEG_NONE, None)
    add(3 + N_DIL, 0, [OFF_M_Q + c for c in range(M_HEADS * M_HD)], SEG128, "m_q")
    return np.asarray(cols, np.int32), gains, tiles, widths


_MAIN_COLS, _MAIN_GAINS, _MAIN_TILES, _MAIN_WIDTHS = _main_plan()


def _main_gain_vector(p):
    per_key = {
        "a_q": p["a_qn"] * (A_DK ** -0.5 * LOG2E), "a_k": p["a_kn"],
        "c_q": p["c_qn"] * (C_HD ** -0.5 * LOG2E), "c_k": p["c_kn"],
        "d_q": p["d_qn"] * (D_HD ** -0.5 * LOG2E), "d_k": p["d_kn"],
        "m_q": p["m_qn"] * (M_HD ** -0.5),
    }
    n = len(_MAIN_GAINS)
    pieces, k = [], 0
    while k < n:
        key = _MAIN_GAINS[k]
        k2 = k
        while k2 < n and _MAIN_GAINS[k2] == key:
            k2 += 1
        if key is None:
            pieces.append(jnp.ones((k2 - k,), F32))
        else:
            gvec = per_key[key].astype(F32)
            pieces.append(jnp.tile(gvec, (k2 - k) // gvec.shape[0]))
        k = k2
    return jnp.concatenate(pieces).reshape(1, n)


def _mla_prep_kernel(b_ref, wq_ref, wk_ref, wv_ref, mseg_ref, cqg_ref, ckvg_ref,
                     qg_ref, kg_ref, ct_ref, s1_ref, s2_ref, q_out, k_out, v_out):
    braw = b_ref[...]
    cq = braw[:, :B_Q_LORA].astype(F32)
    cqn = ((cq * lax.rsqrt(jnp.mean(cq * cq, axis=-1, keepdims=True) + EPS)) * cqg_ref[...]).astype(BF16)
    ckv = braw[:, B_Q_LORA:B_Q_LORA + B_KV_LORA].astype(F32)
    ckvn = ((ckv * lax.rsqrt(jnp.mean(ckv * ckv, axis=-1, keepdims=True) + EPS)) * ckvg_ref[...]).astype(BF16)
    qb = jnp.dot(cqn, wq_ref[...], preferred_element_type=F32)
    kb = jnp.dot(ckvn, wk_ref[...], preferred_element_type=F32)
    v_out[...] = jnp.dot(ckvn, wv_ref[...], preferred_element_type=F32).astype(BF16)
    ct, s1, s2 = ct_ref[...], s1_ref[...], s2_ref[...]
    half = B_ROPE // 2

    def norm_slab(y, gain_ref):
        segms = jnp.dot((y * y).astype(BF16), mseg_ref[...], preferred_element_type=F32)
        return (y * lax.rsqrt(segms + EPS)) * gain_ref[...]

    def rope(yn):
        return yn * ct + pltpu.roll(yn, half, 1) * s1 + pltpu.roll(yn, LANES - half, 1) * s2

    kr = rope(norm_slab(braw[:, B_Q_LORA + B_KV_LORA:].astype(F32), kg_ref))
    for h in range(B_HEADS):
        sl = slice(h * LANES, (h + 1) * LANES)
        k_out[:, sl] = (norm_slab(kb[:, sl], kg_ref) + kr).astype(BF16)
        q_out[:, sl] = rope(norm_slab(qb[:, sl], qg_ref)).astype(BF16)


def _rope_tables(s):
    half = B_ROPE // 2
    inv = ROPE_THETA ** (-jnp.arange(half, dtype=F32) / half)
    ang = jnp.arange(s, dtype=F32)[:, None] * inv[None, :]
    cos, sin = jnp.cos(ang), jnp.sin(ang)
    one = jnp.ones((s, B_NOPE), F32)
    zpad = jnp.zeros((s, LANES - B_NOPE - B_ROPE), F32)
    z64 = jnp.zeros((s, B_NOPE), F32)
    zh = jnp.zeros((s, half), F32)
    ct = jnp.concatenate([one, cos, cos, zpad], axis=1)
    s1 = jnp.concatenate([z64, zh, sin, zpad], axis=1)
    s2 = jnp.concatenate([z64, -sin, zh, zpad], axis=1)
    return ct, s1, s2


def _mla_prep(braw, p, bsz, s, tm):
    t = braw.shape[0]
    tm = min(tm, s)
    pad = LANES - B_NOPE - B_ROPE
    wq = p["w_qb"].reshape(B_Q_LORA, B_HEADS, B_NOPE + B_ROPE)
    wq = jnp.pad(wq, ((0, 0), (0, 0), (0, pad))).reshape(B_Q_LORA, B_HEADS * LANES).astype(BF16)
    wkv = p["w_kvb"].reshape(B_KV_LORA, B_HEADS, B_NOPE + B_DV)
    wk = jnp.pad(wkv[:, :, :B_NOPE], ((0, 0), (0, 0), (0, LANES - B_NOPE))).reshape(B_KV_LORA, B_HEADS * LANES).astype(BF16)
    wv = wkv[:, :, B_NOPE:].reshape(B_KV_LORA, B_HEADS * B_DV).astype(BF16)
    mseg =_seg_matrix(LANES, [(0, B_NOPE), (B_NOPE, B_ROPE)])
    scale = (B_NOPE + B_ROPE) ** -0.5 * LOG2E
    zp = jnp.zeros((pad,), F32)
    qg = jnp.concatenate([p["b_qn"].astype(F32) * scale, zp]).reshape(1, LANES)
    kg = jnp.concatenate([p["b_kn"].astype(F32), zp]).reshape(1, LANES)
    ct, s1, s2 = _rope_tables(s)
    nst = s // tm
    full = lambda shape: _resident(shape, lambda b, i: tuple(0 for _ in shape))
    row = lambda wd: pl.BlockSpec((tm, wd), lambda b, i: (b * nst + i, 0))
    tab = pl.BlockSpec((tm, LANES), lambda b, i: (i, 0))
    return pl.pallas_call(
        _mla_prep_kernel,
        grid=(bsz, nst),
        in_specs=[row(512), full((B_Q_LORA, 512)), full((B_KV_LORA, 512)), full((B_KV_LORA, 512)),
                  full((LANES, LANES)), full((1, B_Q_LORA)), full((1, B_KV_LORA)),
                  full((1, LANES)), full((1, LANES)), tab, tab, tab],
        out_specs=[row(512), row(512), row(512)],
        out_shape=[jax.ShapeDtypeStruct((t, 512), BF16)] * 3,
        compiler_params=_cparams(("parallel", "parallel")),
        name="mla_prep",
    )(braw, wq, wk, wv, jnp.asarray(mseg, BF16),
      p["b_cqn"].astype(F32).reshape(1, -1), p["b_ckvn"].astype(F32).reshape(1, -1), qg, kg, ct, s1, s2)


def _flash_kernel(slopes_ref, lam_ref, hn_ref, q_ref, k_ref, v_ref, o_ref, m_sc, l_sc, acc_sc,
                  *, nmaps, alibi, tq, tk, seq, lam_init):
    h = pl.program_id(1)
    q0 = pl.program_id(2) * tq
    q = q_ref[...]
    if nmaps == 2:
        lane = lax.broadcasted_iota(jnp.int32, q.shape, 1)
        qs = [jnp.where(lane < A_DK, q, jnp.zeros_like(q)), jnp.where(lane >= A_DK, q, jnp.zeros_like(q))]
    else:
        qs = [q]
    m_sc[...] = jnp.full(m_sc.shape, NEG_INF, F32)
    l_sc[...] = jnp.zeros(l_sc.shape, F32)
    acc_sc[...] = jnp.zeros(acc_sc.shape, F32)
    if alibi:
        slope2 = slopes_ref[h] * LOG2E
        dmat = (lax.broadcasted_iota(jnp.int32, (tq, tk), 0)
                - lax.broadcasted_iota(jnp.int32, (tq, tk), 1)).astype(F32)

    def body(j, carry):
        k0 = pl.multiple_of(j * tk, tk)
        k = k_ref[pl.ds(k0, tk), :]
        v = v_ref[pl.ds(k0, tk), :]
        if alibi:
            bias = -slope2 * jnp.abs(dmat + (q0 - k0).astype(F32))
        for m in range(nmaps):
            s = lax.dot_general(qs[m], k, (((1,), (1,)), ((), ())), preferred_element_type=F32)
            if alibi:
                s = s + bias
            m_old = m_sc[m]
            m_new = jnp.maximum(m_old, jnp.max(s, axis=1, keepdims=True))
            alpha = jnp.exp2(m_old - m_new)
            pr = jnp.exp2(s - m_new)
            l_sc[m] = alpha * l_sc[m] + jnp.sum(pr, axis=1, keepdims=True)
            acc_sc[m] = alpha * acc_sc[m] + jnp.dot(pr.astype(BF16), v, preferred_element_type=F32)
            m_sc[m] = m_new
        return carry

    lax.fori_loop(0, seq // tk, body, 0)
    outs = [acc_sc[m] / l_sc[m] for m in range(nmaps)]
    o_ref[...] = _flash_epilogue(outs, lam_ref, hn_ref, lam_init).astype(BF16)


def _flash_epilogue(outs, lam_ref, hn_ref, lam_init):
    if len(outs) == 1:
        return outs[0]
    lf = lam_ref[...]
    e1 = jnp.exp(jnp.sum(lf[0:1] * lf[1:2], axis=1, keepdims=True))
    e2 = jnp.exp(jnp.sum(lf[2:3] * lf[3:4], axis=1, keepdims=True))
    lam = e1 - e2 + lam_init
    o = outs[0] - lam * outs[1]
    o = (o * lax.rsqrt(jnp.mean(o * o, axis=-1, keepdims=True) + EPS)) * hn_ref[...]
    return o * (1.0 - lam_init)


def _flash_fixed_kernel(shift_ref, slopes_ref, lam_ref, hn_ref, q_ref, k_ref, v_ref, o_ref, v1_sc, acc_sc,
                        *bias_sc, nmaps, alibi, tq, tk, seq, lam_init, slopes_static):
    h = pl.program_id(1)
    i = pl.program_id(2)
    q0 = i * tq

    @pl.when(i == 0)
    def _():
        v1_sc[:, :LANES] = v_ref[...]
        v1_sc[:, LANES:] = jnp.ones((seq, LANES), BF16)
        if alibi:
            bias_sc[0][...] = (lax.broadcasted_iota(jnp.int32, (tq, tk), 0)
                               - lax.broadcasted_iota(jnp.int32, (tq, tk), 1)).astype(F32) * (slopes_ref[h] * LOG2E)

    q = q_ref[...]
    if nmaps == 2:
        lane = lax.broadcasted_iota(jnp.int32, q.shape, 1)
        qs = [jnp.where(lane < A_DK, q, jnp.zeros_like(q)), jnp.where(lane >= A_DK, q, jnp.zeros_like(q))]
    else:
        qs = [q]
    acc_sc[...] = jnp.zeros(acc_sc.shape, F32)
    shift = shift_ref[0]
    if alibi:
        slope2 = slopes_ref[h] * LOG2E
        t0_sc = bias_sc[0]

    def step(j):
        k0 = j * tk if isinstance(j, int) else pl.multiple_of(j * tk, tk)
        k = k_ref[pl.ds(k0, tk), :]
        v1 = v1_sc[pl.ds(k0, tk), :]
        if alibi:
            sub = jnp.abs(t0_sc[...] + slope2 * (q0 - k0).astype(F32)) + shift
        ps = []
        for m in range(nmaps):
            s = lax.dot_general(qs[m], k, (((1,), (1,)), ((), ())), preferred_element_type=F32)
            ps.append(jnp.exp2(s - sub if alibi else s - shift).astype(BF16))
        pr = ps[0] if nmaps == 1 else jnp.concatenate(ps, axis=0)
        acc_sc[...] += jnp.dot(pr, v1, preferred_element_type=F32)

    nkv = seq // tk
    if alibi:
        assert tq == tk
        for hd, slope in enumerate(slopes_static):
            reach = int(math.floor((ZERO_EXP2 / (slope * LOG2E) - 1.0) / tk)) + 1
            cnt = min(nkv, 2 * reach + 1)

            @pl.when(h == hd)
            def _(reach=reach, cnt=cnt):
                start = jnp.clip(i - reach, 0, nkv - cnt) if cnt < nkv else 0
                for jj in range(cnt):
                    step(start + jj)
    else:
        for jj in range(nkv):
            step(jj)
    outs = [acc_sc[m * tq:(m + 1) * tq, :LANES] / acc_sc[m * tq:(m + 1) * tq, LANES:] for m in range(nmaps)]
    o_ref[...] = _flash_epilogue(outs, lam_ref, hn_ref, lam_init).astype(BF16)


def _flash(qa, ka, va, qcb, kcb, vcb, bsz, s, heads, nmaps, alibi, slopes, lam, hn, lam_init, bound2):
    t = qa.shape[0]
    name = "flash_diff" if nmaps == 2 else "flash_mla"
    shift = bound2.reshape(1).astype(F32)

    def call(kern, tq, tk, scratch, extra_in, extra_args, suffix):
        tq_, tk_ = min(tq, s), min(tk, s)
        nq = s // tq_
        return pl.pallas_call(
            functools.partial(kern, nmaps=nmaps, alibi=alibi, tq=tq_, tk=tk_, seq=s, lam_init=lam_init),
            grid=(bsz, heads, nq),
            in_specs=extra_in + [
                pl.BlockSpec(memory_space=pltpu.SMEM),
                _resident(lam.shape, lambda b, h, i: (0, 0)),
                _resident(hn.shape, lambda b, h, i: (0, 0)),
                pl.BlockSpec((tq_, LANES), lambda b, h, i: (b * nq + i, qcb + h)),
                pl.BlockSpec((s, LANES), lambda b, h, i: (b, kcb + h)),
                pl.BlockSpec((s, LANES), lambda b, h, i: (b, vcb + h)),
            ],
            out_specs=pl.BlockSpec((tq_, LANES), lambda b, h, i: (b * nq + i, h)),
            out_shape=jax.ShapeDtypeStruct((t, heads * LANES), BF16),
            scratch_shapes=scratch(tq_),
            compiler_params=_cparams(("parallel", "parallel", "arbitrary")),
            name=name + suffix,
        )(*extra_args, slopes, lam, hn, qa, ka, va)

    def fixed():
        return call(functools.partial(_flash_fixed_kernel, slopes_static=_alibi_slopes_static(heads)),
                    512 if alibi else 1024, 512,
                    lambda tq_: [pltpu.VMEM((s, 2 * LANES), BF16), pltpu.VMEM((nmaps * tq_, 2 * LANES), F32)]
                    + ([pltpu.VMEM((tq_, min(512, s)), F32)] if alibi else []),
                    [pl.BlockSpec(memory_space=pltpu.SMEM)], [shift], "_fixed")

    def online():
        return call(_flash_kernel, 256, 512,
                    lambda tq_: [pltpu.VMEM((nmaps, tq_, 1), F32), pltpu.VMEM((nmaps, tq_, 1), F32),
                                 pltpu.VMEM((nmaps, tq_, LANES), F32)],
                    [], [], "_online")

    return lax.cond(bound2 <= MAX_FIXED_SHIFT, fixed, online)


def _band_kernel(slopes_ref, sink_ref, q_ref, k_ref, v_ref, *outs, window, qb, kw, ut, u_len, dil, use_sink, emit_lse):
    o_ref = outs[0]
    slab = pl.program_id(0)
    ubase = pl.program_id(3) * ut
    lane = lax.broadcasted_iota(jnp.int32, (kw, LANES), 1)
    qlane = lax.broadcasted_iota(jnp.int32, (qb, LANES), 1)
    dmat = (lax.broadcasted_iota(jnp.int32, (qb, kw), 0) - lax.broadcasted_iota(jnp.int32, (qb, kw), 1))
    for sb in range(ut // qb):
        u0 = ubase + sb * qb
        start = pl.multiple_of(jnp.clip(u0 - window, 0, u_len - kw), 16)
        q = q_ref[sb * qb:(sb + 1) * qb, :]
        k = k_ref[pl.ds(start, kw), :]
        v = v_ref[pl.ds(start, kw), :]
        rel = dmat + (u0 - start)
        valid = jnp.abs(rel) <= window
        dist = jnp.abs(rel).astype(F32) * float(dil)
        o_acc = jnp.zeros((qb, LANES), F32)
        lse_acc = jnp.zeros((qb, LANES), F32)
        for hh in range(2):
            head = slab * 2 + hh
            sel_q = (qlane < 64) if hh == 0 else (qlane >= 64)
            sel_v = (lane < 64) if hh == 0 else (lane >= 64)
            qh = jnp.where(sel_q, q, jnp.zeros_like(q))
            vh = jnp.where(sel_v, v, jnp.zeros_like(v))
            s = lax.dot_general(qh, k, (((1,), (1,)), ((), ())), preferred_element_type=F32)
            logits = jnp.where(valid, s - (slopes_ref[head] * LOG2E) * dist, NEG_INF)
            mx = jnp.max(logits, axis=1, keepdims=True)
            if use_sink:
                sk = sink_ref[head]
                mx = jnp.maximum(mx, sk)
            e = jnp.exp2(logits - mx)
            den = jnp.sum(e, axis=1, keepdims=True)
            if use_sink:
                den = den + jnp.exp2(sk - mx)
            pv = jnp.dot(e.astype(BF16), vh, preferred_element_type=F32)
            o_acc = o_acc + pv / den
            if emit_lse:
                lse_acc = lse_acc + jnp.where(sel_q, mx + jnp.log2(den), 0.0)
        o_ref[sb * qb:(sb + 1) * qb, :] = o_acc.astype(BF16)
        if emit_lse:
            outs[1][sb * qb:(sb + 1) * qb, :] = lse_acc


def _band_fixed_kernel(shift_ref, slopes_ref, sink_ref, q_ref, k_ref, v_ref, *outs,
                       window, qb, kw, ut, u_len, dil, use_sink, emit_lse, sps):
    o_ref, sub_sc = outs[0], outs[-1]
    slab0 = pl.program_id(0) * sps
    ubase = pl.program_id(3) * ut
    shift = shift_ref[0]
    qlane = lax.broadcasted_iota(jnp.int32, (2 * qb, LANES), 1)
    qrow = lax.broadcasted_iota(jnp.int32, (2 * qb, LANES), 0)
    keep = (qlane < 64) == (qrow < qb)
    first = lax.broadcasted_iota(jnp.int32, (qb, LANES), 1) < 64
    ones = jnp.ones((kw, LANES), BF16)

    @pl.when((pl.program_id(1) == 0) & (pl.program_id(2) == 0) & (pl.program_id(3) == 0))
    def _():
        dmat = (lax.broadcasted_iota(jnp.int32, (qb, kw), 0) - lax.broadcasted_iota(jnp.int32, (qb, kw), 1))
        for case in range(3):
            absrel = jnp.abs(dmat + case * window)
            for hh in range(2 * sps):
                slope2 = slopes_ref[slab0 * 2 + hh] * (LOG2E * dil)
                sub_sc[case, hh // 2, (hh % 2) * qb:(hh % 2 + 1) * qb] = jnp.where(
                    absrel <= window, absrel.astype(F32) * slope2 + shift, -NEG_INF)

    for sb in range(ut // qb):
        u0 = ubase + sb * qb
        start = pl.multiple_of(jnp.clip(u0 - window, 0, u_len - kw), 16)
        case = (u0 - start) // window
        for sl in range(sps):
            lanes = slice(sl * LANES, (sl + 1) * LANES)
            q = q_ref[sb * qb:(sb + 1) * qb, lanes]
            k = k_ref[pl.ds(start, kw), lanes]
            v1 = jnp.concatenate([v_ref[pl.ds(start, kw), lanes], ones], axis=1)
            q2 = jnp.concatenate([q, q], axis=0)
            q2 = jnp.where(keep, q2, jnp.zeros_like(q2))
            s2 = lax.dot_general(q2, k, (((1,), (1,)), ((), ())), preferred_element_type=F32)
            r = jnp.dot(jnp.exp2(s2 - sub_sc[case, sl]).astype(BF16), v1, preferred_element_type=F32)
            za, zb = r[:qb, LANES:], r[qb:, LANES:]
            if use_sink:
                head = (slab0 + sl) * 2
                za = za + jnp.exp2(jnp.full((1, 1), sink_ref[head] - shift, F32))
                zb = zb + jnp.exp2(jnp.full((1, 1), sink_ref[head + 1] - shift, F32))
            o_ref[sb * qb:(sb + 1) * qb, lanes] = jnp.where(first, r[:qb, :LANES] / za, r[qb:, :LANES] / zb).astype(BF16)
            if emit_lse:
                outs[1][sb * qb:(sb + 1) * qb, lanes] = jnp.where(first, jnp.log2(za), jnp.log2(zb)) + shift


def _band(srcv, qcb, kcb, vcb, bsz, s, dil, ncb_src, window, slopes, sink, use_sink, emit_lse, ut, bound2):
    u_len = s // dil
    assert srcv.shape[0] == bsz * u_len
    qb = min(LANES, u_len)
    kw = min(qb + 2 * window, u_len)
    nslab = 4
    out_shapes = [jax.ShapeDtypeStruct((bsz * u_len, dil * 512), BF16)]
    if emit_lse:
        out_shapes.append(jax.ShapeDtypeStruct((bsz * u_len, dil * 512), F32))
    smem = pl.BlockSpec(memory_space=pltpu.SMEM)

    def call(kern, extra_in, extra_args, suffix, sps, rows, scratch=()):
        wd = sps * LANES
        ut_ = min(rows, u_len)
        nu = u_len // ut_
        assert all(f(sl) == f(0) + sl for f in (qcb, kcb, vcb) for sl in range(sps)) and ncb_src % sps == 0
        out_spec = pl.BlockSpec((ut_, wd), lambda g, b, r, i: (b * nu + i, r * (nslab // sps) + g))
        return pl.pallas_call(
            functools.partial(kern, window=window, qb=qb, kw=kw, ut=ut_, u_len=u_len, dil=dil,
                              use_sink=use_sink, emit_lse=emit_lse),
            grid=(nslab // sps, bsz, dil, nu),
            scratch_shapes=list(scratch),
            in_specs=extra_in + [
                smem, smem,
                pl.BlockSpec((ut_, wd), lambda g, b, r, i: (b * nu + i, (r * ncb_src + qcb(g * sps)) // sps)),
                pl.BlockSpec((u_len, wd), lambda g, b, r, i: (b, (r * ncb_src + kcb(g * sps)) // sps)),
                pl.BlockSpec((u_len, wd), lambda g, b, r, i: (b, (r * ncb_src + vcb(g * sps)) // sps)),
            ],
            out_specs=[out_spec] * len(out_shapes),
            out_shape=out_shapes,
            compiler_params=_cparams(("arbitrary", "arbitrary", "arbitrary", "arbitrary")),
            name="band_w%d_d%d%s" % (window, dil, suffix),
        )(*extra_args, slopes, sink, srcv, srcv, srcv)

    sps = nslab if (dil > 1 and kcb(0) % nslab == 0 and vcb(0) % nslab == 0 and qcb(0) % nslab == 0) else 1
    res = lax.cond(bound2 <= MAX_FIXED_SHIFT,
                   lambda: call(functools.partial(_band_fixed_kernel, sps=sps), [smem],
                                [bound2.reshape(1).astype(F32)], "_fixed", sps, ut // sps,
                                [pltpu.VMEM((3, sps, 2 * qb, kw), F32)]),
                   lambda: call(_band_kernel, [], [], "_online", 1, ut))
    return list(res)


def _memattn_kernel(q_ref, k_ref, v_ref, o_ref):
    for h in range(M_HEADS):
        sl = slice(h * M_HD, (h + 1) * M_HD)
        s = lax.dot_general(q_ref[:, sl], k_ref[:, sl], (((1,), (1,)), ((), ())), preferred_element_type=F32)
        mx = jnp.max(s, axis=1, keepdims=True)
        e = jnp.exp(s - mx)
        den = jnp.sum(e, axis=1, keepdims=True)
        pv = jnp.dot(e.astype(BF16), v_ref[:, sl], preferred_element_type=F32)
        o_ref[:, sl] = (pv / den).astype(BF16)


def _memattn(qm, mk, mv, bsz, s, tq):
    t = qm.shape[0]
    tq = min(tq, s)
    nq = s // tq
    wd = M_HEADS * M_HD
    return pl.pallas_call(
        _memattn_kernel,
        grid=(bsz, nq),
        in_specs=[pl.BlockSpec((tq, wd), lambda b, i: (b * nq + i, 0)),
                  pl.BlockSpec((N_MEM, wd), lambda b, i: (b, 0)),
                  pl.BlockSpec((N_MEM, wd), lambda b, i: (b, 0))],
        out_specs=pl.BlockSpec((tq, wd), lambda b, i: (b * nq + i, 0)),
        out_shape=jax.ShapeDtypeStruct((t, wd), BF16),
        compiler_params=_cparams(("parallel", "parallel")),
        name="memattn",
    )(qm, mk, mv)


def _merge_kernel(x_ref, ng_ref, wz_ref, wg_ref, bg_ref, wbr_ref, wout_ref,
                  oa_ref, ob_ref, oc_ref, om_ref, od0_ref, od1_ref, od2_ref, l0_ref, l1_ref, l2_ref,
                  y_ref, *stage, dils, sub_rows):
    tm = x_ref.shape[0]
    nsl = BRANCH_W // LANES

    def token_major(ref, dil, slabs):
        if dil == 1:
            return ref[...].astype(F32)
        rows = tm // dil
        for r in range(dil):
            for c in range(nsl):
                col = r * BRANCH_W + c * LANES
                slabs[c, pl.ds(r, rows, stride=dil), :] = ref[:, col:col + LANES].astype(F32)
        return jnp.concatenate([slabs[c] for c in range(nsl)], axis=1)

    def dilated_mixture():
        st = iter(stage)
        ls = [token_major(ref, dl, next(st) if dl > 1 else None) for ref, dl in zip((l0_ref, l1_ref, l2_ref), dils)]
        os_ = [token_major(ref, dl, next(st) if dl > 1 else None) for ref, dl in zip((od0_ref, od1_ref, od2_ref), dils)]
        mx = jnp.maximum(jnp.maximum(ls[0], ls[1]), ls[2])
        e0, e1, e2 = jnp.exp2(ls[0] - mx), jnp.exp2(ls[1] - mx), jnp.exp2(ls[2] - mx)
        den = e0 + e1 + e2
        return (e0 / den) * os_[0] + (e1 / den) * os_[1] + (e2 / den) * os_[2]

    d = x_ref.shape[1]
    od = dilated_mixture()
    for rg in range(tm // sub_rows):
        rs = slice(rg * sub_rows, (rg + 1) * sub_rows)
        x = x_ref[rs, :]
        ms = jnp.mean(x * x, axis=-1, keepdims=True)
        h = ((x * lax.rsqrt(ms + EPS)) * ng_ref[...]).astype(BF16)
        branch_in = (lambda: oa_ref[rs, :].astype(F32), lambda: ob_ref[rs, :].astype(F32),
                     lambda: oc_ref[rs, :].astype(F32), lambda: od[rs], lambda: om_ref[rs, :].astype(F32))
        acc = jnp.zeros((sub_rows, d), F32)
        for br in range(N_BRANCH):
            z = jnp.dot(h, wz_ref[:, br * BRANCH_W:(br + 1) * BRANCH_W], preferred_element_type=F32)
            g = jnp.dot(h, wg_ref[:, br * d:(br + 1) * d], preferred_element_type=F32) + bg_ref[br:br + 1, :]
            u = (branch_in[br]() * (z / (1.0 + jnp.exp(-z)))).astype(BF16)
            tbr = jnp.dot(u, wbr_ref[br], preferred_element_type=F32)
            acc = acc + tbr / (1.0 + jnp.exp(-g))
        y_ref[rs, :] = x + jnp.dot(acc.astype(BF16), wout_ref[...], preferred_element_type=F32)


def _merge(x2d, p, wz, wg, oa, ob, oc, om, ods, lses, tm):
    t, d = x2d.shape
    tm = min(tm, t)
    rowf = lambda wd: pl.BlockSpec((tm, wd), lambda i: (i, 0))
    dils = tuple(dl for _, dl in DIL_PAIRS)
    assert all(tm % (16 * dl) == 0 for dl in dils)
    classf = [pl.BlockSpec((tm // dl, dl * BRANCH_W), lambda i: (i, 0)) for dl in dils]
    nstage = 2 * sum(1 for dl in dils if dl > 1)
    full = lambda shape: _resident(shape, lambda i: tuple(0 for _ in shape))
    return pl.pallas_call(
        functools.partial(_merge_kernel, dils=dils, sub_rows=min(MERGE_SUB_ROWS, tm)),
        grid=(t // tm,),
        in_specs=[rowf(d), full((1, d)), full((d, N_BRANCH * BRANCH_W)), full((d, N_BRANCH * d)),
                  full((N_BRANCH, d)), full((N_BRANCH, BRANCH_W, d)), full((d, d))]
        + [rowf(BRANCH_W)] * 4 + classf + classf,
        out_specs=rowf(d),
        out_shape=jax.ShapeDtypeStruct((t, d), F32),
        scratch_shapes=[pltpu.VMEM((BRANCH_W // LANES, tm, LANES), F32)] * nstage,
        compiler_params=_cparams(("parallel",)),
        name="merge",
    )(x2d, p["norm_g"].astype(F32).reshape(1, d), wz, wg, p["b_gate"].astype(F32),
      p["w_br_bf16"], p["w_out_bf16"], oa, ob, oc, om, *ods, *lses)


def _alibi_slopes_static(n):
    return tuple(float(v) for v in 2.0 ** (-8.0 * np.arange(1, n + 1, dtype=np.float64) / n))


def _alibi_slopes(n):
    return jnp.asarray(_alibi_slopes_static(n), F32)


def _encoder_layer(x, mem, layer_idx, p):
    bsz, s, d = x.shape
    t = bsz * s
    x2d = x.reshape(t, d)

    pa, pb, pc, pd0, pd1, pd2, pm = _proj(x2d, p["norm_g"].astype(F32), p["w_main"], p["gain_main"],
                                          _MAIN_TILES, _MAIN_WIDTHS, tm=512,
                                          out_dils=[1, 1, 1] + [dl for _, dl in DIL_PAIRS] + [1])

    lam_init = 0.8 - 0.6 * math.exp(-0.3 * layer_idx)
    bound_a = (BOUND_MARGIN * A_DK * (A_DK ** -0.5 * LOG2E)
               * jnp.max(jnp.abs(p["a_qn"].astype(F32))) * jnp.max(jnp.abs(p["a_kn"].astype(F32))))
    oa = _flash(pa, pa, pa, 0, 4, 8, bsz, s, A_HEADS, 2, True, _alibi_slopes(A_HEADS),
                p["a_lam"].astype(F32), p["a_hn"].astype(F32).reshape(1, A_DV), lam_init, bound_a)

    qb, kb, vb = _mla_prep(pb, p, bsz, s, tm=512)
    dummy_lam = jnp.zeros((4, A_DK), F32)
    dummy_hn = jnp.ones((1, A_DV), F32)
    dqk = B_NOPE + B_ROPE
    bound_b = (BOUND_MARGIN * dqk * (dqk ** -0.5 * LOG2E)
               * jnp.max(jnp.abs(p["b_qn"].astype(F32))) * jnp.max(jnp.abs(p["b_kn"].astype(F32))))
    ob = _flash(qb, kb, vb, 0, 0, 0, bsz, s, B_HEADS, 1, False, _alibi_slopes(B_HEADS),
                dummy_lam, dummy_hn, 0.0, bound_b)

    sink2 = p["c_sink"].astype(F32) * LOG2E
    bound_c = jnp.maximum(
        BOUND_MARGIN * C_HD * (C_HD ** -0.5 * LOG2E)
        * jnp.max(jnp.abs(p["c_qn"].astype(F32))) * jnp.max(jnp.abs(p["c_kn"].astype(F32))),
        jnp.max(sink2))
    oc, = _band(pc, lambda sl: sl, lambda sl: 4 + sl // 2, lambda sl: 6 + sl // 2, bsz, s, 1, 8,
                C_WINDOW, _alibi_slopes(C_QH), sink2, True, False, BAND_ROWS, bound_c)

    ods, lses = [], []
    zero_sink = jnp.zeros((D_HEADS,), F32)
    bound_d = (BOUND_MARGIN * D_HD * (D_HD ** -0.5 * LOG2E)
               * jnp.max(jnp.abs(p["d_qn"].astype(F32))) * jnp.max(jnp.abs(p["d_kn"].astype(F32))))
    for g, (win, dil) in enumerate(DIL_PAIRS):
        og, lg = _band((pd0, pd1, pd2)[g], lambda sl: sl, lambda sl: 4 + sl,
                       lambda sl: 8 + sl, bsz, s, dil, 12, win // (2 * dil),
                       _alibi_slopes(D_HEADS), zero_sink, False, True, BAND_ROWS, bound_d)
        ods.append(og)
        lses.append(lg)

    mtiles = [(0, SEG128, 0, 0), (256, SEG128, 0, 256), (512, SEG_NONE, 1, 0), (768, SEG_NONE, 1, 256)]
    mk, mv = _proj(mem.reshape(bsz * N_MEM, d), p["m_norm"].astype(F32), p["w_mem_kv_bf16"], p["gain_mem"],
                   mtiles, [512, 512], tm=256)
    om = _memattn(pm, mk, mv, bsz, s, tq=2048)

    y = _merge(x2d, p, p["wz"], p["wg"], oa, ob, oc, om, ods, lses, tm=512)
    return y.reshape(bsz, s, d)


def _prepare_layer(p):
    w_in = p["w_in"]
    pieces, k, n = [], 0, len(_MAIN_COLS)
    while k < n:
        k2 = k + 1
        if _MAIN_COLS[k] == N_IN:
            while k2 < n and _MAIN_COLS[k2] == N_IN:
                k2 += 1
            pieces.append(jnp.zeros((w_in.shape[0], k2 - k), BF16))
        else:
            while k2 < n and _MAIN_COLS[k2] == _MAIN_COLS[k2 - 1] + 1:
                k2 += 1
            pieces.append(w_in[:, int(_MAIN_COLS[k]):int(_MAIN_COLS[k]) + (k2 - k)].astype(BF16))
        k = k2
    return {
        "w_main": jnp.concatenate(pieces, axis=1),
        "gain_main": _main_gain_vector(p),
        "wz": w_in[:, OFF_Z:OFF_G].astype(BF16),
        "wg": w_in[:, OFF_G:].astype(BF16),
        "w_mem_kv_bf16": p["w_mem_kv"].astype(BF16),
        "gain_mem": jnp.concatenate([jnp.tile(p["m_kn"].astype(F32), M_HEADS),
                                     jnp.ones((M_HEADS * M_HD,), F32)]).reshape(1, -1),
        "w_br_bf16": p["w_br"].astype(BF16),
        "w_out_bf16": p["w_out"].astype(BF16),
    }


def kernel(x_prompt, x_sample, mem_prompt, mem_sample, norm_g, w_in, a_qn, a_kn, a_lam, a_hn, b_cqn, b_ckvn, w_qb, w_kvb, b_qn, b_kn, c_qn, c_kn, c_sink, d_qn, d_kn, m_norm, w_mem_kv, m_qn, m_kn, b_gate, w_br, w_out):
    depth = norm_g.shape[0]
    y_prompt, y_sample = x_prompt, x_sample
    for l in range(depth):
        p = {
            "norm_g": norm_g[l], "w_in": w_in[l],
            "a_qn": a_qn[l], "a_kn": a_kn[l], "a_lam": a_lam[l], "a_hn": a_hn[l],
            "b_cqn": b_cqn[l], "b_ckvn": b_ckvn[l], "w_qb": w_qb[l], "w_kvb": w_kvb[l],
            "b_qn": b_qn[l], "b_kn": b_kn[l],
            "c_qn": c_qn[l], "c_kn": c_kn[l], "c_sink": c_sink[l],
            "d_qn": d_qn[l], "d_kn": d_kn[l],
            "m_norm": m_norm[l], "w_mem_kv": w_mem_kv[l], "m_qn": m_qn[l], "m_kn": m_kn[l],
            "b_gate": b_gate[l], "w_br": w_br[l], "w_out": w_out[l],
        }
        p.update(_prepare_layer(p))
        y_prompt = _encoder_layer(y_prompt, mem_prompt, l, p)
        y_sample = _encoder_layer(y_sample, mem_sample, l, p)
    return (y_prompt, y_sample)
```

```python
import functools
import math

import numpy as np
import jax
import jax.numpy as jnp
from jax import lax
from jax.experimental import pallas as pl
from jax.experimental.pallas import tpu as pltpu

F32 = jnp.float32
BF16 = jnp.bfloat16

D_MODEL = 1024
N_MEM = 256
BRANCH_W = 512
N_BRANCH = 5
NEG_INF = -1e30
EPS = 1e-6
A_HEADS, A_DK, A_DV = 4, 64, 128
B_HEADS, B_Q_LORA, B_KV_LORA, B_NOPE, B_ROPE, B_DV = 4, 256, 128, 64, 32, 128
ROPE_THETA = 10000.0
C_QH, C_KVH, C_HD, C_WINDOW = 8, 2, 64, 128
D_HEADS, D_HD = 8, 64
DIL_PAIRS = ((128, 1), (512, 4), (2048, 16))
N_DIL = 3
M_HEADS, M_HD = 4, 128

OFF_A_Q = 0
OFF_A_K = OFF_A_Q + 2 * A_HEADS * A_DK
OFF_A_V = OFF_A_K + 2 * A_HEADS * A_DK
OFF_B_CQ = OFF_A_V + A_HEADS * A_DV
OFF_B_CKV = OFF_B_CQ + B_Q_LORA
OFF_B_KR = OFF_B_CKV + B_KV_LORA
OFF_C_Q = OFF_B_KR + B_ROPE
OFF_C_K = OFF_C_Q + C_QH * C_HD
OFF_C_V = OFF_C_K + C_KVH * C_HD
OFF_D_Q = OFF_C_V + C_KVH * C_HD
OFF_D_K = OFF_D_Q + N_DIL * D_HEADS * D_HD
OFF_D_V = OFF_D_K + N_DIL * D_HEADS * D_HD
OFF_M_Q = OFF_D_V + N_DIL * D_HEADS * D_HD
OFF_Z = OFF_M_Q + M_HEADS * M_HD
OFF_G = OFF_Z + N_BRANCH * BRANCH_W
N_IN = OFF_G + N_BRANCH * D_MODEL

LANES = 128
COL_TILE = 256
VMEM_LIMIT = 56 * 1024 * 1024

SEG_NONE, SEG64, SEG128 = -1, 0, 1

LOG2E = 1.4426950408889634
MAX_FIXED_SHIFT = 30.0 * LOG2E
BOUND_MARGIN = 1.02
PROJ_STAGE_SLABS = 4
ZERO_EXP2 = 151.0

PROJ_ROWS = 512
MEM_PROJ_ROWS = 256
MLA_PREP_ROWS = 512
FLASH_TQ, FLASH_TK = 512, 512
FLASH_TQ_PLAIN = 1024
FLASH_ONLINE_TQ = 256
BAND_ROWS = 8192
MEMATTN_ROWS = 2048
MERGE_ROWS = 512
MERGE_SUB_ROWS = 256


def _cparams(sem):
    return pltpu.CompilerParams(dimension_semantics=sem, vmem_limit_bytes=VMEM_LIMIT)


def _resident(shape, index_map):
    return pl.BlockSpec(shape, index_map, pipeline_mode=pl.Buffered(1))


def _seg_matrix(width, segs):
    m = np.zeros((width, width), np.float32)
    for s, n in segs:
        m[s:s + n, s:s + n] = 1.0 / n
    return m


def _proj_kernel(x_ref, ng_ref, w_ref, gain_ref, mseg_ref, *refs, tiles, out_widths, out_dils):
    out_refs = refs[:len(out_widths)]
    ybuf = refs[len(out_widths)] if max(out_dils) > 1 else None
    nbuf_used = 0
    x = x_ref[...]
    ms = jnp.mean(x * x, axis=-1, keepdims=True)
    h = ((x * lax.rsqrt(ms + EPS)) * ng_ref[...]).astype(BF16)
    assert len(tiles) % 2 == 0
    for pair in range(0, len(tiles), 2):
        base = tiles[pair][0]
        assert tiles[pair + 1][0] == base + COL_TILE
        y2 = jnp.dot(h, w_ref[:, base:base + 2 * COL_TILE], preferred_element_type=F32)
        for half, (c0, seg, oi, oc) in enumerate(tiles[pair:pair + 2]):
            y = y2[:, half * COL_TILE:(half + 1) * COL_TILE]
            if seg != SEG_NONE:
                sq = (y * y).astype(BF16)
                segms = jnp.dot(sq, mseg_ref[seg], preferred_element_type=F32)
                y = (y * lax.rsqrt(segms + EPS)) * gain_ref[:, c0:c0 + COL_TILE]
            dil = out_dils[oi]
            if dil == 1:
                out_refs[oi][:, oc:oc + COL_TILE] = y.astype(BF16)
                continue
            rows = y.shape[0] // dil
            for lh in range(COL_TILE // LANES):
                buf = ybuf.at[nbuf_used % PROJ_STAGE_SLABS]
                nbuf_used += 1
                buf[...] = y[:, lh * LANES:(lh + 1) * LANES]
                s1 = 4 if (dil > 4 and dil % 4 == 0) else 1
                s2 = dil // s1
                if s1 > 1:
                    mid = ybuf.at[nbuf_used % PROJ_STAGE_SLABS]
                    nbuf_used += 1
                    part = y.shape[0] // s1
                    for r1 in range(s1):
                        mid[r1 * part:(r1 + 1) * part, :] = buf[pl.ds(r1, part, stride=s1), :]
                else:
                    mid, part = buf, y.shape[0]
                for r1 in range(s1):
                    for r2 in range(s2):
                        col = (r1 + s1 * r2) * out_widths[oi] + oc + lh * LANES
                        out_refs[oi][:, col:col + LANES] = mid[pl.ds(r1 * part + r2, rows, stride=s2), :].astype(BF16)


def _proj(x2d, norm_gain, w, gain, tiles, out_widths, tm, out_dils=None):
    t, d = x2d.shape
    n = w.shape[1]
    tm = min(tm, t)
    out_dils = tuple(out_dils) if out_dils is not None else (1,) * len(out_widths)
    assert all(tm % (16 * dl) == 0 for dl in out_dils)
    mseg = jnp.asarray(
        np.stack([_seg_matrix(COL_TILE, [(s, 64) for s in range(0, COL_TILE, 64)]),
                  _seg_matrix(COL_TILE, [(s, 128) for s in range(0, COL_TILE, 128)])]), BF16)
    return pl.pallas_call(
        functools.partial(_proj_kernel, tiles=tuple(tiles), out_widths=tuple(out_widths), out_dils=out_dils),
        grid=(t // tm,),
        in_specs=[
            pl.BlockSpec((tm, d), lambda i: (i, 0)),
            _resident((1, d), lambda i: (0, 0)),
            _resident((d, n), lambda i: (0, 0)),
            _resident((1, n), lambda i: (0, 0)),
            _resident((2, COL_TILE, COL_TILE), lambda i: (0, 0, 0)),
        ],
        out_specs=[pl.BlockSpec((tm // dl, dl * wd), lambda i: (i, 0)) for wd, dl in zip(out_widths, out_dils)],
        out_shape=[jax.ShapeDtypeStruct((t // dl, dl * wd), BF16) for wd, dl in zip(out_widths, out_dils)],
        scratch_shapes=[pltpu.VMEM((PROJ_STAGE_SLABS, tm, LANES), F32)] if max(out_dils) > 1 else [],
        compiler_params=_cparams(("parallel",)),
        name="proj",
    )(x2d, norm_gain.reshape(1, d), w, gain, mseg)


def _main_plan():
    zero = N_IN
    cols, gains, tiles = [], [], []
    widths = [3 * 512, 512, 1024] + [3 * 512] * N_DIL + [512]

    def add(out_idx, out_col, src_cols, seg, gain_key):
        assert len(src_cols) % COL_TILE == 0
        c0 = len(cols)
        cols.extend(src_cols)
        gains.extend([gain_key] * len(src_cols))
        for k in range(len(src_cols) // COL_TILE):
            tiles.append((c0 + k * COL_TILE, seg, out_idx, out_col + k * COL_TILE))

    hm = [m * A_HEADS * A_DK + h * A_DK + d for h in range(A_HEADS) for m in range(2) for d in range(A_DK)]
    add(0, 0, [OFF_A_Q + c for c in hm], SEG64, "a_q")
    add(0, 512, [OFF_A_K + c for c in hm], SEG64, "a_k")
    add(0, 1024, [OFF_A_V + c for c in range(A_HEADS * A_DV)], SEG_NONE, None)
    braw = ([OFF_B_CQ + c for c in range(B_Q_LORA)] + [OFF_B_CKV + c for c in range(B_KV_LORA)]
            + [zero] * B_NOPE + [OFF_B_KR + c for c in range(B_ROPE)] + [zero] * (LANES - B_NOPE - B_ROPE))
    add(1, 0, braw, SEG_NONE, None)
    add(2, 0, [OFF_C_Q + c for c in range(C_QH * C_HD)], SEG64, "c_q")
    dup = [kv * C_HD + d for kv in range(C_KVH) for _ in range(2) for d in range(C_HD)]
    add(2, 512, [OFF_C_K + c for c in dup], SEG64, "c_k")
    add(2, 768, [OFF_C_V + c for c in dup], SEG_NONE, None)
    for g in range(N_DIL):
        gsl = [g * D_HEADS * D_HD + c for c in range(D_HEADS * D_HD)]
        add(3 + g, 0, [OFF_D_Q + c for c in gsl], SEG64, "d_q")
        add(3 + g, 512, [OFF_D_K + c for c in gsl], SEG64, "d_k")
        add(3 + g, 1024, [OFF_D_V + c for c in gsl], SEG_NONE, None)
    add(3 + N_DIL, 0, [OFF_M_Q + c for c in range(M_HEADS * M_HD)], SEG128, "m_q")
    return np.asarray(cols, np.int32), gains, tiles, widths


_MAIN_COLS, _MAIN_GAINS, _MAIN_TILES, _MAIN_WIDTHS = _main_plan()


def _main_gain_vector(p):
    per_key = {
        "a_q": p["a_qn"] * (A_DK ** -0.5 * LOG2E), "a_k": p["a_kn"],
        "c_q": p["c_qn"] * (C_HD ** -0.5 * LOG2E), "c_k": p["c_kn"],
        "d_q": p["d_qn"] * (D_HD ** -0.5 * LOG2E), "d_k": p["d_kn"],
        "m_q": p["m_qn"] * (M_HD ** -0.5),
    }
    n = len(_MAIN_GAINS)
    pieces, k = [], 0
    while k < n:
        key = _MAIN_GAINS[k]
        k2 = k
        while k2 < n and _MAIN_GAINS[k2] == key:
            k2 += 1
        if key is None:
            pieces.append(jnp.ones((k2 - k,), F32))
        else:
            gvec = per_key[key].astype(F32)
            pieces.append(jnp.tile(gvec, (k2 - k) // gvec.shape[0]))
        k = k2
    return jnp.concatenate(pieces).reshape(1, n)


def _mla_prep_kernel(b_ref, wq_ref, wk_ref, wv_ref, mseg_ref, cqg_ref, ckvg_ref,
                     qg_ref, kg_ref, ct_ref, s1_ref, s2_ref, q_out, k_out, v_out):
    braw = b_ref[...]
    cq = braw[:, :B_Q_LORA].astype(F32)
    cqn = ((cq * lax.rsqrt(jnp.mean(cq * cq, axis=-1, keepdims=True) + EPS)) * cqg_ref[...]).astype(BF16)
    ckv = braw[:, B_Q_LORA:B_Q_LORA + B_KV_LORA].astype(F32)
    ckvn = ((ckv * lax.rsqrt(jnp.mean(ckv * ckv, axis=-1, keepdims=True) + EPS)) * ckvg_ref[...]).astype(BF16)
    qb = jnp.dot(cqn, wq_ref[...], preferred_element_type=F32)
    kb = jnp.dot(ckvn, wk_ref[...], preferred_element_type=F32)
    v_out[...] = jnp.dot(ckvn, wv_ref[...], preferred_element_type=F32).astype(BF16)
    ct, s1, s2 = ct_ref[...], s1_ref[...], s2_ref[...]
    half = B_ROPE // 2

    def norm_slab(y, gain_ref):
        segms = jnp.dot((y * y).astype(BF16), mseg_ref[...], preferred_element_type=F32)
        return (y * lax.rsqrt(segms + EPS)) * gain_ref[...]

    def rope(yn):
        return yn * ct + pltpu.roll(yn, half, 1) * s1 + pltpu.roll(yn, LANES - half, 1) * s2

    kr = rope(norm_slab(braw[:, B_Q_LORA + B_KV_LORA:].astype(F32), kg_ref))
    for h in range(B_HEADS):
        sl = slice(h * LANES, (h + 1) * LANES)
        k_out[:, sl] = (norm_slab(kb[:, sl], kg_ref) + kr).astype(BF16)
        q_out[:, sl] = rope(norm_slab(qb[:, sl], qg_ref)).astype(BF16)


def _rope_tables(s):
    half = B_ROPE // 2
    inv = ROPE_THETA ** (-jnp.arange(half, dtype=F32) / half)
    ang = jnp.arange(s, dtype=F32)[:, None] * inv[None, :]
    cos, sin = jnp.cos(ang), jnp.sin(ang)
    one = jnp.ones((s, B_NOPE), F32)
    zpad = jnp.zeros((s, LANES - B_NOPE - B_ROPE), F32)
    z64 = jnp.zeros((s, B_NOPE), F32)
    zh = jnp.zeros((s, half), F32)
    ct = jnp.concatenate([one, cos, cos, zpad], axis=1)
    s1 = jnp.concatenate([z64, zh, sin, zpad], axis=1)
    s2 = jnp.concatenate([z64, -sin, zh, zpad], axis=1)
    return ct, s1, s2


def _mla_prep(braw, p, bsz, s, tm):
    t = braw.shape[0]
    tm = min(tm, s)
    pad = LANES - B_NOPE - B_ROPE
    wq = p["w_qb"].reshape(B_Q_LORA, B_HEADS, B_NOPE + B_ROPE)
    wq = jnp.pad(wq, ((0, 0), (0, 0), (0, pad))).reshape(B_Q_LORA, B_HEADS * LANES).astype(BF16)
    wkv = p["w_kvb"].reshape(B_KV_LORA, B_HEADS, B_NOPE + B_DV)
    wk = jnp.pad(wkv[:, :, :B_NOPE], ((0, 0), (0, 0), (0, LANES - B_NOPE))).reshape(B_KV_LORA, B_HEADS * LANES).astype(BF16)
    wv = wkv[:, :, B_NOPE:].reshape(B_KV_LORA, B_HEADS * B_DV).astype(BF16)
    mseg =_seg_matrix(LANES, [(0, B_NOPE), (B_NOPE, B_ROPE)])
    scale = (B_NOPE + B_ROPE) ** -0.5 * LOG2E
    zp = jnp.zeros((pad,), F32)
    qg = jnp.concatenate([p["b_qn"].astype(F32) * scale, zp]).reshape(1, LANES)
    kg = jnp.concatenate([p["b_kn"].astype(F32), zp]).reshape(1, LANES)
    ct, s1, s2 = _rope_tables(s)
    nst = s // tm
    full = lambda shape: _resident(shape, lambda b, i: tuple(0 for _ in shape))
    row = lambda wd: pl.BlockSpec((tm, wd), lambda b, i: (b * nst + i, 0))
    tab = pl.BlockSpec((tm, LANES), lambda b, i: (i, 0))
    return pl.pallas_call(
        _mla_prep_kernel,
        grid=(bsz, nst),
        in_specs=[row(512), full((B_Q_LORA, 512)), full((B_KV_LORA, 512)), full((B_KV_LORA, 512)),
                  full((LANES, LANES)), full((1, B_Q_LORA)), full((1, B_KV_LORA)),
                  full((1, LANES)), full((1, LANES)), tab, tab, tab],
        out_specs=[row(512), row(512), row(512)],
        out_shape=[jax.ShapeDtypeStruct((t, 512), BF16)] * 3,
        compiler_params=_cparams(("parallel", "parallel")),
        name="mla_prep",
    )(braw, wq, wk, wv, jnp.asarray(mseg, BF16),
      p["b_cqn"].astype(F32).reshape(1, -1), p["b_ckvn"].astype(F32).reshape(1, -1), qg, kg, ct, s1, s2)


def _flash_kernel(slopes_ref, lam_ref, hn_ref, q_ref, k_ref, v_ref, o_ref, m_sc, l_sc, acc_sc,
                  *, nmaps, alibi, tq, tk, seq, lam_init):
    h = pl.program_id(1)
    q0 = pl.program_id(2) * tq
    q = q_ref[...]
    if nmaps == 2:
        lane = lax.broadcasted_iota(jnp.int32, q.shape, 1)
        qs = [jnp.where(lane < A_DK, q, jnp.zeros_like(q)), jnp.where(lane >= A_DK, q, jnp.zeros_like(q))]
    else:
        qs = [q]
    m_sc[...] = jnp.full(m_sc.shape, NEG_INF, F32)
    l_sc[...] = jnp.zeros(l_sc.shape, F32)
    acc_sc[...] = jnp.zeros(acc_sc.shape, F32)
    if alibi:
        slope2 = slopes_ref[h] * LOG2E
        dmat = (lax.broadcasted_iota(jnp.int32, (tq, tk), 0)
                - lax.broadcasted_iota(jnp.int32, (tq, tk), 1)).astype(F32)

    def body(j, carry):
        k0 = pl.multiple_of(j * tk, tk)
        k = k_ref[pl.ds(k0, tk), :]
        v = v_ref[pl.ds(k0, tk), :]
        if alibi:
            bias = -slope2 * jnp.abs(dmat + (q0 - k0).astype(F32))
        for m in range(nmaps):
            s = lax.dot_general(qs[m], k, (((1,), (1,)), ((), ())), preferred_element_type=F32)
            if alibi:
                s = s + bias
            m_old = m_sc[m]
            m_new = jnp.maximum(m_old, jnp.max(s, axis=1, keepdims=True))
            alpha = jnp.exp2(m_old - m_new)
            pr = jnp.exp2(s - m_new)
            l_sc[m] = alpha * l_sc[m] + jnp.sum(pr, axis=1, keepdims=True)
            acc_sc[m] = alpha * acc_sc[m] + jnp.dot(pr.astype(BF16), v, preferred_element_type=F32)
            m_sc[m] = m_new
        return carry

    lax.fori_loop(0, seq // tk, body, 0)
    outs = [acc_sc[m] / l_sc[m] for m in range(nmaps)]
    o_ref[...] = _flash_epilogue(outs, lam_ref, hn_ref, lam_init).astype(BF16)


def _flash_epilogue(outs, lam_ref, hn_ref, lam_init):
    if len(outs) == 1:
        return outs[0]
    lf = lam_ref[...]
    e1 = jnp.exp(jnp.sum(lf[0:1] * lf[1:2], axis=1, keepdims=True))
    e2 = jnp.exp(jnp.sum(lf[2:3] * lf[3:4], axis=1, keepdims=True))
    lam = e1 - e2 + lam_init
    o = outs[0] - lam * outs[1]
    o = (o * lax.rsqrt(jnp.mean(o * o, axis=-1, keepdims=True) + EPS)) * hn_ref[...]
    return o * (1.0 - lam_init)


def _flash_fixed_kernel(shift_ref, slopes_ref, lam_ref, hn_ref, q_ref, k_ref, v_ref, o_ref, v1_sc, acc_sc,
                        *bias_sc, nmaps, alibi, tq, tk, seq, lam_init, slopes_static):
    h = pl.program_id(1)
    i = pl.program_id(2)
    q0 = i * tq

    @pl.when(i == 0)
    def _():
        v1_sc[:, :LANES] = v_ref[...]
        v1_sc[:, LANES:] = jnp.ones((seq, LANES), BF16)
        if alibi:
            bias_sc[0][...] = (lax.broadcasted_iota(jnp.int32, (tq, tk), 0)
                               - lax.broadcasted_iota(jnp.int32, (tq, tk), 1)).astype(F32) * (slopes_ref[h] * LOG2E)

    q = q_ref[...]
    if nmaps == 2:
        lane = lax.broadcasted_iota(jnp.int32, q.shape, 1)
        q_all = jnp.concatenate([jnp.where(lane < A_DK, q, jnp.zeros_like(q)),
                                 jnp.where(lane >= A_DK, q, jnp.zeros_like(q))], axis=0)
    else:
        q_all = q
    shift = shift_ref[0]
    if alibi:
        slope2 = slopes_ref[h] * LOG2E
        t0_sc = bias_sc[0]

    def step(j, first):
        k0 = j * tk if isinstance(j, int) else pl.multiple_of(j * tk, tk)
        k = k_ref[pl.ds(k0, tk), :]
        v1 = v1_sc[pl.ds(k0, tk), :]
        if alibi:
            sub = jnp.abs(t0_sc[...] + slope2 * (q0 - k0).astype(F32)) + shift
        s = lax.dot_general(q_all, k, (((1,), (1,)), ((), ())), preferred_element_type=F32)
        ps = [jnp.exp2(s[m * tq:(m + 1) * tq] - (sub if alibi else shift)).astype(BF16) for m in range(nmaps)]
        pr = ps[0] if nmaps == 1 else jnp.concatenate(ps, axis=0)
        pv = jnp.dot(pr, v1, preferred_element_type=F32)
        if first:
            acc_sc[...] = pv
        else:
            acc_sc[...] += pv

    nkv = seq // tk
    if alibi:
        assert tq == tk
        for hd, slope in enumerate(slopes_static):
            reach = int(math.floor((ZERO_EXP2 / (slope * LOG2E) - 1.0) / tk)) + 1
            cnt = min(nkv, 2 * reach + 1)

            @pl.when(h == hd)
            def _(reach=reach, cnt=cnt):
                start = jnp.clip(i - reach, 0, nkv - cnt) if cnt < nkv else 0
                for jj in range(cnt):
                    step(start + jj, jj == 0)
    else:
        for jj in range(nkv):
            step(jj, jj == 0)
    outs = [acc_sc[m * tq:(m + 1) * tq, :LANES] / acc_sc[m * tq:(m + 1) * tq, LANES:] for m in range(nmaps)]
    o_ref[...] = _flash_epilogue(outs, lam_ref, hn_ref, lam_init).astype(BF16)


def _flash(qa, ka, va, qcb, kcb, vcb, bsz, s, heads, nmaps, alibi, slopes, lam, hn, lam_init, bound2):
    t = qa.shape[0]
    name = "flash_diff" if nmaps == 2 else "flash_mla"
    shift = bound2.reshape(1).astype(F32)

    def call(kern, tq, tk, scratch, extra_in, extra_args, suffix):
        tq_, tk_ = min(tq, s), min(tk, s)
        nq = s // tq_
        return pl.pallas_call(
            functools.partial(kern, nmaps=nmaps, alibi=alibi, tq=tq_, tk=tk_, seq=s, lam_init=lam_init),
            grid=(bsz, heads, nq),
            in_specs=extra_in + [
                pl.BlockSpec(memory_space=pltpu.SMEM),
                _resident(lam.shape, lambda b, h, i: (0, 0)),
                _resident(hn.shape, lambda b, h, i: (0, 0)),
                pl.BlockSpec((tq_, LANES), lambda b, h, i: (b * nq + i, qcb + h)),
                pl.BlockSpec((s, LANES), lambda b, h, i: (b, kcb + h)),
                pl.BlockSpec((s, LANES), lambda b, h, i: (b, vcb + h)),
            ],
            out_specs=pl.BlockSpec((tq_, LANES), lambda b, h, i: (b * nq + i, h)),
            out_shape=jax.ShapeDtypeStruct((t, heads * LANES), BF16),
            scratch_shapes=scratch(tq_),
            compiler_params=_cparams(("parallel", "parallel", "arbitrary")),
            name=name + suffix,
        )(*extra_args, slopes, lam, hn, qa, ka, va)

    def fixed():
        return call(functools.partial(_flash_fixed_kernel, slopes_static=_alibi_slopes_static(heads)),
                    FLASH_TQ if alibi else FLASH_TQ_PLAIN, FLASH_TK,
                    lambda tq_: [pltpu.VMEM((s, 2 * LANES), BF16), pltpu.VMEM((nmaps * tq_, 2 * LANES), F32)]
                    + ([pltpu.VMEM((tq_, min(FLASH_TK, s)), F32)] if alibi else []),
                    [pl.BlockSpec(memory_space=pltpu.SMEM)], [shift], "_fixed")

    def online():
        return call(_flash_kernel, FLASH_ONLINE_TQ, FLASH_TK,
                    lambda tq_: [pltpu.VMEM((nmaps, tq_, 1), F32), pltpu.VMEM((nmaps, tq_, 1), F32),
                                 pltpu.VMEM((nmaps, tq_, LANES), F32)],
                    [], [], "_online")

    return lax.cond(bound2 <= MAX_FIXED_SHIFT, fixed, online)


def _band_kernel(slopes_ref, sink_ref, q_ref, k_ref, v_ref, *outs, window, qb, kw, ut, u_len, dil, use_sink, emit_lse):
    o_ref = outs[0]
    slab = pl.program_id(0)
    ubase = pl.program_id(3) * ut
    lane = lax.broadcasted_iota(jnp.int32, (kw, LANES), 1)
    qlane = lax.broadcasted_iota(jnp.int32, (qb, LANES), 1)
    dmat = (lax.broadcasted_iota(jnp.int32, (qb, kw), 0) - lax.broadcasted_iota(jnp.int32, (qb, kw), 1))
    for sb in range(ut // qb):
        u0 = ubase + sb * qb
        start = pl.multiple_of(jnp.clip(u0 - window, 0, u_len - kw), 16)
        q = q_ref[sb * qb:(sb + 1) * qb, :]
        k = k_ref[pl.ds(start, kw), :]
        v = v_ref[pl.ds(start, kw), :]
        rel = dmat + (u0 - start)
        valid = jnp.abs(rel) <= window
        dist = jnp.abs(rel).astype(F32) * float(dil)
        o_acc = jnp.zeros((qb, LANES), F32)
        lse_acc = jnp.zeros((qb, LANES), F32)
        for hh in range(2):
            head = slab * 2 + hh
            sel_q = (qlane < 64) if hh == 0 else (qlane >= 64)
            sel_v = (lane < 64) if hh == 0 else (lane >= 64)
            qh = jnp.where(sel_q, q, jnp.zeros_like(q))
            vh = jnp.where(sel_v, v, jnp.zeros_like(v))
            s = lax.dot_general(qh, k, (((1,), (1,)), ((), ())), preferred_element_type=F32)
            logits = jnp.where(valid, s - (slopes_ref[head] * LOG2E) * dist, NEG_INF)
            mx = jnp.max(logits, axis=1, keepdims=True)
            if use_sink:
                sk = sink_ref[head]
                mx = jnp.maximum(mx, sk)
            e = jnp.exp2(logits - mx)
            den = jnp.sum(e, axis=1, keepdims=True)
            if use_sink:
                den = den + jnp.exp2(sk - mx)
            pv = jnp.dot(e.astype(BF16), vh, preferred_element_type=F32)
            o_acc = o_acc + pv / den
            if emit_lse:
                lse_acc = lse_acc + jnp.where(sel_q, mx + jnp.log2(den), 0.0)
        o_ref[sb * qb:(sb + 1) * qb, :] = o_acc.astype(BF16)
        if emit_lse:
            outs[1][sb * qb:(sb + 1) * qb, :] = lse_acc


def _band_fixed_kernel(shift_ref, slopes_ref, sink_ref, q_ref, k_ref, v_ref, *outs,
                       window, qb, kw, ut, u_len, dil, use_sink, emit_lse, sps):
    o_ref, sub_sc = outs[0], outs[-1]
    slab0 = pl.program_id(0) * sps
    ubase = pl.program_id(3) * ut
    shift = shift_ref[0]
    qlane = lax.broadcasted_iota(jnp.int32, (2 * qb, LANES), 1)
    qrow = lax.broadcasted_iota(jnp.int32, (2 * qb, LANES), 0)
    keep = (qlane < 64) == (qrow < qb)
    first = lax.broadcasted_iota(jnp.int32, (qb, LANES), 1) < 64
    ones = jnp.ones((kw, LANES), BF16)

    @pl.when((pl.program_id(1) == 0) & (pl.program_id(2) == 0) & (pl.program_id(3) == 0))
    def _():
        dmat = (lax.broadcasted_iota(jnp.int32, (qb, kw), 0) - lax.broadcasted_iota(jnp.int32, (qb, kw), 1))
        for case in range(3):
            absrel = jnp.abs(dmat + case * window)
            for hh in range(2 * sps):
                slope2 = slopes_ref[slab0 * 2 + hh] * (LOG2E * dil)
                sub_sc[case, hh // 2, (hh % 2) * qb:(hh % 2 + 1) * qb] = jnp.where(
                    absrel <= window, absrel.astype(F32) * slope2 + shift, -NEG_INF)

    for sb in range(ut // qb):
        u0 = ubase + sb * qb
        start = pl.multiple_of(jnp.clip(u0 - window, 0, u_len - kw), 16)
        case = (u0 - start) // window
        for sl in range(sps):
            lanes = slice(sl * LANES, (sl + 1) * LANES)
            q = q_ref[sb * qb:(sb + 1) * qb, lanes]
            k = k_ref[pl.ds(start, kw), lanes]
            v1 = jnp.concatenate([v_ref[pl.ds(start, kw), lanes], ones], axis=1)
            q2 = jnp.concatenate([q, q], axis=0)
            q2 = jnp.where(keep, q2, jnp.zeros_like(q2))
            s2 = lax.dot_general(q2, k, (((1,), (1,)), ((), ())), preferred_element_type=F32)
            r = jnp.dot(jnp.exp2(s2 - sub_sc[case, sl]).astype(BF16), v1, preferred_element_type=F32)
            za, zb = r[:qb, LANES:], r[qb:, LANES:]
            if use_sink:
                head = (slab0 + sl) * 2
                za = za + jnp.exp2(jnp.full((1, 1), sink_ref[head] - shift, F32))
                zb = zb + jnp.exp2(jnp.full((1, 1), sink_ref[head + 1] - shift, F32))
            o_ref[sb * qb:(sb + 1) * qb, lanes] = jnp.where(first, r[:qb, :LANES] / za, r[qb:, :LANES] / zb).astype(BF16)
            if emit_lse:
                outs[1][sb * qb:(sb + 1) * qb, lanes] = jnp.where(first, jnp.log2(za), jnp.log2(zb)) + shift


def _band(srcv, qcb, kcb, vcb, bsz, s, dil, ncb_src, window, slopes, sink, use_sink, emit_lse, ut, bound2):
    u_len = s // dil
    assert srcv.shape[0] == bsz * u_len
    qb = min(LANES, u_len)
    kw = min(qb + 2 * window, u_len)
    nslab = 4
    out_shapes = [jax.ShapeDtypeStruct((bsz * u_len, dil * 512), BF16)]
    if emit_lse:
        out_shapes.append(jax.ShapeDtypeStruct((bsz * u_len, dil * 512), F32))
    smem = pl.BlockSpec(memory_space=pltpu.SMEM)

    def call(kern, extra_in, extra_args, suffix, sps, rows, scratch=()):
        wd = sps * LANES
        ut_ = min(rows, u_len)
        nu = u_len // ut_
        assert all(f(sl) == f(0) + sl for f in (qcb, kcb, vcb) for sl in range(sps)) and ncb_src % sps == 0
        out_spec = pl.BlockSpec((ut_, wd), lambda g, b, r, i: (b * nu + i, r * (nslab // sps) + g))
        return pl.pallas_call(
            functools.partial(kern, window=window, qb=qb, kw=kw, ut=ut_, u_len=u_len, dil=dil,
                              use_sink=use_sink, emit_lse=emit_lse),
            grid=(nslab // sps, bsz, dil, nu),
            scratch_shapes=list(scratch),
            in_specs=extra_in + [
                smem, smem,
                pl.BlockSpec((ut_, wd), lambda g, b, r, i: (b * nu + i, (r * ncb_src + qcb(g * sps)) // sps)),
                pl.BlockSpec((u_len, wd), lambda g, b, r, i: (b, (r * ncb_src + kcb(g * sps)) // sps)),
                pl.BlockSpec((u_len, wd), lambda g, b, r, i: (b, (r * ncb_src + vcb(g * sps)) // sps)),
            ],
            out_specs=[out_spec] * len(out_shapes),
            out_shape=out_shapes,
            compiler_params=_cparams(("arbitrary", "arbitrary", "arbitrary", "arbitrary")),
            name="band_w%d_d%d%s" % (window, dil, suffix),
        )(*extra_args, slopes, sink, srcv, srcv, srcv)

    sps = nslab if (dil > 1 and kcb(0) % nslab == 0 and vcb(0) % nslab == 0 and qcb(0) % nslab == 0) else 1
    res = lax.cond(bound2 <= MAX_FIXED_SHIFT,
                   lambda: call(functools.partial(_band_fixed_kernel, sps=sps), [smem],
                                [bound2.reshape(1).astype(F32)], "_fixed", sps, ut // sps,
                                [pltpu.VMEM((3, sps, 2 * qb, kw), F32)]),
                   lambda: call(_band_kernel, [], [], "_online", 1, ut))
    return list(res)


def _memattn_kernel(q_ref, k_ref, v_ref, o_ref):
    for h in range(M_HEADS):
        sl = slice(h * M_HD, (h + 1) * M_HD)
        s = lax.dot_general(q_ref[:, sl], k_ref[:, sl], (((1,), (1,)), ((), ())), preferred_element_type=F32)
        mx = jnp.max(s, axis=1, keepdims=True)
        e = jnp.exp(s - mx)
        den = jnp.sum(e, axis=1, keepdims=True)
        pv = jnp.dot(e.astype(BF16), v_ref[:, sl], preferred_element_type=F32)
        o_ref[:, sl] = (pv / den).astype(BF16)


def _memattn(qm, mk, mv, bsz, s, tq):
    t = qm.shape[0]
    tq = min(tq, s)
    nq = s // tq
    wd = M_HEADS * M_HD
    return pl.pallas_call(
        _memattn_kernel,
        grid=(bsz, nq),
        in_specs=[pl.BlockSpec((tq, wd), lambda b, i: (b * nq + i, 0)),
                  pl.BlockSpec((N_MEM, wd), lambda b, i: (b, 0)),
                  pl.BlockSpec((N_MEM, wd), lambda b, i: (b, 0))],
        out_specs=pl.BlockSpec((tq, wd), lambda b, i: (b * nq + i, 0)),
        out_shape=jax.ShapeDtypeStruct((t, wd), BF16),
        compiler_params=_cparams(("parallel", "parallel")),
        name="memattn",
    )(qm, mk, mv)


def _merge_kernel(x_ref, ng_ref, wz_ref, wg_ref, bg_ref, wbr_ref, wout_ref,
                  oa_ref, ob_ref, oc_ref, om_ref, od0_ref, od1_ref, od2_ref, l0_ref, l1_ref, l2_ref,
                  y_ref, *stage, dils, sub_rows):
    tm = x_ref.shape[0]
    nsl = BRANCH_W // LANES

    def token_major(ref, dil, slabs):
        if dil == 1:
            return ref[...].astype(F32)
        rows = tm // dil
        for r in range(dil):
            for c in range(nsl):
                col = r * BRANCH_W + c * LANES
                slabs[c, pl.ds(r, rows, stride=dil), :] = ref[:, col:col + LANES].astype(F32)
        return jnp.concatenate([slabs[c] for c in range(nsl)], axis=1)

    def dilated_mixture():
        st = iter(stage)
        ls = [token_major(ref, dl, next(st) if dl > 1 else None) for ref, dl in zip((l0_ref, l1_ref, l2_ref), dils)]
        os_ = [token_major(ref, dl, next(st) if dl > 1 else None) for ref, dl in zip((od0_ref, od1_ref, od2_ref), dils)]
        mx = jnp.maximum(jnp.maximum(ls[0], ls[1]), ls[2])
        e0, e1, e2 = jnp.exp2(ls[0] - mx), jnp.exp2(ls[1] - mx), jnp.exp2(ls[2] - mx)
        den = e0 + e1 + e2
        return (e0 / den) * os_[0] + (e1 / den) * os_[1] + (e2 / den) * os_[2]

    d = x_ref.shape[1]
    od = dilated_mixture()
    for rg in range(tm // sub_rows):
        rs = slice(rg * sub_rows, (rg + 1) * sub_rows)
        x = x_ref[rs, :]
        ms = jnp.mean(x * x, axis=-1, keepdims=True)
        h = ((x * lax.rsqrt(ms + EPS)) * ng_ref[...]).astype(BF16)
        branch_in = (lambda: oa_ref[rs, :].astype(F32), lambda: ob_ref[rs, :].astype(F32),
                     lambda: oc_ref[rs, :].astype(F32), lambda: od[rs], lambda: om_ref[rs, :].astype(F32))
        acc = jnp.zeros((sub_rows, d), F32)
        for br in range(N_BRANCH):
            z = jnp.dot(h, wz_ref[:, br * BRANCH_W:(br + 1) * BRANCH_W], preferred_element_type=F32)
            g = jnp.dot(h, wg_ref[:, br * d:(br + 1) * d], preferred_element_type=F32) + bg_ref[br:br + 1, :]
            u = (branch_in[br]() * (z / (1.0 + jnp.exp(-z)))).astype(BF16)
            tbr = jnp.dot(u, wbr_ref[br], preferred_element_type=F32)
            acc = acc + tbr / (1.0 + jnp.exp(-g))
        y_ref[rs, :] = x + jnp.dot(acc.astype(BF16), wout_ref[...], preferred_element_type=F32)


def _merge(x2d, p, wz, wg, oa, ob, oc, om, ods, lses, tm):
    t, d = x2d.shape
    tm = min(tm, t)
    rowf = lambda wd: pl.BlockSpec((tm, wd), lambda i: (i, 0))
    dils = tuple(dl for _, dl in DIL_PAIRS)
    assert all(tm % (16 * dl) == 0 for dl in dils)
    classf = [pl.BlockSpec((tm // dl, dl * BRANCH_W), lambda i: (i, 0)) for dl in dils]
    nstage = 2 * sum(1 for dl in dils if dl > 1)
    full = lambda shape: _resident(shape, lambda i: tuple(0 for _ in shape))
    return pl.pallas_call(
        functools.partial(_merge_kernel, dils=dils, sub_rows=min(MERGE_SUB_ROWS, tm)),
        grid=(t // tm,),
        in_specs=[rowf(d), full((1, d)), full((d, N_BRANCH * BRANCH_W)), full((d, N_BRANCH * d)),
                  full((N_BRANCH, d)), full((N_BRANCH, BRANCH_W, d)), full((d, d))]
        + [rowf(BRANCH_W)] * 4 + classf + classf,
        out_specs=rowf(d),
        out_shape=jax.ShapeDtypeStruct((t, d), F32),
        scratch_shapes=[pltpu.VMEM((BRANCH_W // LANES, tm, LANES), F32)] * nstage,
        compiler_params=_cparams(("parallel",)),
        name="merge",
    )(x2d, p["norm_g"].astype(F32).reshape(1, d), wz, wg, p["b_gate"].astype(F32),
      p["w_br_bf16"], p["w_out_bf16"], oa, ob, oc, om, *ods, *lses)


def _alibi_slopes_static(n):
    return tuple(float(v) for v in 2.0 ** (-8.0 * np.arange(1, n + 1, dtype=np.float64) / n))


def _alibi_slopes(n):
    return jnp.asarray(_alibi_slopes_static(n), F32)


def _encoder_layer(x, mem, layer_idx, p):
    bsz, s, d = x.shape
    t = bsz * s
    x2d = x.reshape(t, d)

    pa, pb, pc, pd0, pd1, pd2, pm = _proj(x2d, p["norm_g"].astype(F32), p["w_main"], p["gain_main"],
                                          _MAIN_TILES, _MAIN_WIDTHS, tm=PROJ_ROWS,
                                          out_dils=[1, 1, 1] + [dl for _, dl in DIL_PAIRS] + [1])

    lam_init = 0.8 - 0.6 * math.exp(-0.3 * layer_idx)
    bound_a = (BOUND_MARGIN * A_DK * (A_DK ** -0.5 * LOG2E)
               * jnp.max(jnp.abs(p["a_qn"].astype(F32))) * jnp.max(jnp.abs(p["a_kn"].astype(F32))))
    oa = _flash(pa, pa, pa, 0, 4, 8, bsz, s, A_HEADS, 2, True, _alibi_slopes(A_HEADS),
                p["a_lam"].astype(F32), p["a_hn"].astype(F32).reshape(1, A_DV), lam_init, bound_a)

    qb, kb, vb = _mla_prep(pb, p, bsz, s, tm=MLA_PREP_ROWS)
    dummy_lam = jnp.zeros((4, A_DK), F32)
    dummy_hn = jnp.ones((1, A_DV), F32)
    dqk = B_NOPE + B_ROPE
    bound_b = (BOUND_MARGIN * dqk * (dqk ** -0.5 * LOG2E)
               * jnp.max(jnp.abs(p["b_qn"].astype(F32))) * jnp.max(jnp.abs(p["b_kn"].astype(F32))))
    ob = _flash(qb, kb, vb, 0, 0, 0, bsz, s, B_HEADS, 1, False, _alibi_slopes(B_HEADS),
                dummy_lam, dummy_hn, 0.0, bound_b)

    sink2 = p["c_sink"].astype(F32) * LOG2E
    bound_c = jnp.maximum(
        BOUND_MARGIN * C_HD * (C_HD ** -0.5 * LOG2E)
        * jnp.max(jnp.abs(p["c_qn"].astype(F32))) * jnp.max(jnp.abs(p["c_kn"].astype(F32))),
        jnp.max(sink2))
    oc, = _band(pc, lambda sl: sl, lambda sl: 4 + sl // 2, lambda sl: 6 + sl // 2, bsz, s, 1, 8,
                C_WINDOW, _alibi_slopes(C_QH), sink2, True, False, BAND_ROWS, bound_c)

    ods, lses = [], []
    zero_sink = jnp.zeros((D_HEADS,), F32)
    bound_d = (BOUND_MARGIN * D_HD * (D_HD ** -0.5 * LOG2E)
               * jnp.max(jnp.abs(p["d_qn"].astype(F32))) * jnp.max(jnp.abs(p["d_kn"].astype(F32))))
    for g, (win, dil) in enumerate(DIL_PAIRS):
        og, lg = _band((pd0, pd1, pd2)[g], lambda sl: sl, lambda sl: 4 + sl,
                       lambda sl: 8 + sl, bsz, s, dil, 12, win // (2 * dil),
                       _alibi_slopes(D_HEADS), zero_sink, False, True, BAND_ROWS, bound_d)
        ods.append(og)
        lses.append(lg)

    mtiles = [(0, SEG128, 0, 0), (256, SEG128, 0, 256), (512, SEG_NONE, 1, 0), (768, SEG_NONE, 1, 256)]
    mk, mv = _proj(mem.reshape(bsz * N_MEM, d), p["m_norm"].astype(F32), p["w_mem_kv_bf16"], p["gain_mem"],
                   mtiles, [512, 512], tm=MEM_PROJ_ROWS)
    om = _memattn(pm, mk, mv, bsz, s, tq=MEMATTN_ROWS)

    y = _merge(x2d, p, p["wz"], p["wg"], oa, ob, oc, om, ods, lses, tm=MERGE_ROWS)
    return y.reshape(bsz, s, d)


def _prepare_layer(p):
    w_in = p["w_in"]
    pieces, k, n = [], 0, len(_MAIN_COLS)
    while k < n:
        k2 = k + 1
        if _MAIN_COLS[k] == N_IN:
            while k2 < n and _MAIN_COLS[k2] == N_IN:
                k2 += 1
            pieces.append(jnp.zeros((w_in.shape[0], k2 - k), BF16))
        else:
            while k2 < n and _MAIN_COLS[k2] == _MAIN_COLS[k2 - 1] + 1:
                k2 += 1
            pieces.append(w_in[:, int(_MAIN_COLS[k]):int(_MAIN_COLS[k]) + (k2 - k)].astype(BF16))
        k = k2
    return {
        "w_main": jnp.concatenate(pieces, axis=1),
        "gain_main": _main_gain_vector(p),
        "wz": w_in[:, OFF_Z:OFF_G].astype(BF16),
        "wg": w_in[:, OFF_G:].astype(BF16),
        "w_mem_kv_bf16": p["w_mem_kv"].astype(BF16),
        "gain_mem": jnp.concatenate([jnp.tile(p["m_kn"].astype(F32), M_HEADS),
                                     jnp.ones((M_HEADS * M_HD,), F32)]).reshape(1, -1),
        "w_br_bf16": p["w_br"].astype(BF16),
        "w_out_bf16": p["w_out"].astype(BF16),
    }


def kernel(x_prompt, x_sample, mem_prompt, mem_sample, norm_g, w_in, a_qn, a_kn, a_lam, a_hn, b_cqn, b_ckvn, w_qb, w_kvb, b_qn, b_kn, c_qn, c_kn, c_sink, d_qn, d_kn, m_norm, w_mem_kv, m_qn, m_kn, b_gate, w_br, w_out):
    depth = norm_g.shape[0]
    y_prompt, y_sample = x_prompt, x_sample
    for l in range(depth):
        p = {
            "norm_g": norm_g[l], "w_in": w_in[l],
            "a_qn": a_qn[l], "a_kn": a_kn[l], "a_lam": a_lam[l], "a_hn": a_hn[l],
            "b_cqn": b_cqn[l], "b_ckvn": b_ckvn[l], "w_qb": w_qb[l], "w_kvb": w_kvb[l],
            "b_qn": b_qn[l], "b_kn": b_kn[l],
            "c_qn": c_qn[l], "c_kn": c_kn[l], "c_sink": c_sink[l],
            "d_qn": d_qn[l], "d_kn": d_kn[l],
            "m_norm": m_norm[l], "w_mem_kv": w_mem_kv[l], "m_qn": m_qn[l], "m_kn": m_kn[l],
            "b_gate": b_gate[l], "w_br": w_br[l], "w_out": w_out[l],
        }
        p.update(_prepare_layer(p))
        y_prompt = _encoder_layer(y_prompt, mem_prompt, l, p)
        y_sample = _encoder_layer(y_sample, mem_sample, l, p)
    return (y_prompt, y_sample)
```

```python
import functools
import math

import numpy as np
import jax
import jax.numpy as jnp
from jax import lax
from jax.experimental import pallas as pl
from jax.experimental.pallas import tpu as pltpu

F32 = jnp.float32
BF16 = jnp.bfloat16

D_MODEL = 1024
N_MEM = 256
BRANCH_W = 512
N_BRANCH = 5
NEG_INF = -1e30
EPS = 1e-6
A_HEADS, A_DK, A_DV = 4, 64, 128
B_HEADS, B_Q_LORA, B_KV_LORA, B_NOPE, B_ROPE, B_DV = 4, 256, 128, 64, 32, 128
ROPE_THETA = 10000.0
C_QH, C_KVH, C_HD, C_WINDOW = 8, 2, 64, 128
D_HEADS, D_HD = 8, 64
DIL_PAIRS = ((128, 1), (512, 4), (2048, 16))
N_DIL = 3
M_HEADS, M_HD = 4, 128

OFF_A_Q = 0
OFF_A_K = OFF_A_Q + 2 * A_HEADS * A_DK
OFF_A_V = OFF_A_K + 2 * A_HEADS * A_DK
OFF_B_CQ = OFF_A_V + A_HEADS * A_DV
OFF_B_CKV = OFF_B_CQ + B_Q_LORA
OFF_B_KR = OFF_B_CKV + B_KV_LORA
OFF_C_Q = OFF_B_KR + B_ROPE
OFF_C_K = OFF_C_Q + C_QH * C_HD
OFF_C_V = OFF_C_K + C_KVH * C_HD
OFF_D_Q = OFF_C_V + C_KVH * C_HD
OFF_D_K = OFF_D_Q + N_DIL * D_HEADS * D_HD
OFF_D_V = OFF_D_K + N_DIL * D_HEADS * D_HD
OFF_M_Q = OFF_D_V + N_DIL * D_HEADS * D_HD
OFF_Z = OFF_M_Q + M_HEADS * M_HD
OFF_G = OFF_Z + N_BRANCH * BRANCH_W
N_IN = OFF_G + N_BRANCH * D_MODEL

LANES = 128
COL_TILE = 256
VMEM_LIMIT = 56 * 1024 * 1024

SEG_NONE, SEG64, SEG128 = -1, 0, 1

LOG2E = 1.4426950408889634
MAX_FIXED_SHIFT = 30.0 * LOG2E
BOUND_MARGIN = 1.02
PROJ_STAGE_SLABS = 4
ZERO_EXP2 = 151.0

PROJ_ROWS = 512
MEM_PROJ_ROWS = 256
MLA_PREP_ROWS = 512
FLASH_TQ, FLASH_TK = 512, 512
FLASH_TQ_PLAIN = 1024
FLASH_ONLINE_TQ = 256
BAND_ROWS = 8192
MEMATTN_ROWS = 2048
MERGE_ROWS = 512
MERGE_SUB_ROWS = 256


def _cparams(sem):
    return pltpu.CompilerParams(dimension_semantics=sem, vmem_limit_bytes=VMEM_LIMIT)


def _resident(shape, index_map):
    return pl.BlockSpec(shape, index_map, pipeline_mode=pl.Buffered(1))


def _seg_matrix(width, segs):
    m = np.zeros((width, width), np.float32)
    for s, n in segs:
        m[s:s + n, s:s + n] = 1.0 / n
    return m


def _proj_kernel(x_ref, ng_ref, w_ref, gain_ref, mseg_ref, *refs, tiles, out_widths, out_dils):
    out_refs = refs[:len(out_widths)]
    ybuf = refs[len(out_widths)] if max(out_dils) > 1 else None
    nbuf_used = 0
    x = x_ref[...]
    ms = jnp.mean(x * x, axis=-1, keepdims=True)
    h = ((x * lax.rsqrt(ms + EPS)) * ng_ref[...]).astype(BF16)
    assert len(tiles) % 2 == 0
    for pair in range(0, len(tiles), 2):
        base = tiles[pair][0]
        assert tiles[pair + 1][0] == base + COL_TILE
        y2 = jnp.dot(h, w_ref[:, base:base + 2 * COL_TILE], preferred_element_type=F32)
        for half, (c0, seg, oi, oc) in enumerate(tiles[pair:pair + 2]):
            y = y2[:, half * COL_TILE:(half + 1) * COL_TILE]
            if seg != SEG_NONE:
                sq = (y * y).astype(BF16)
                segms = jnp.dot(sq, mseg_ref[seg], preferred_element_type=F32)
                y = (y * lax.rsqrt(segms + EPS)) * gain_ref[:, c0:c0 + COL_TILE]
            dil = out_dils[oi]
            if dil == 1:
                out_refs[oi][:, oc:oc + COL_TILE] = y.astype(BF16)
                continue
            rows = y.shape[0] // dil
            for lh in range(COL_TILE // LANES):
                buf = ybuf.at[nbuf_used % PROJ_STAGE_SLABS]
                nbuf_used += 1
                buf[...] = y[:, lh * LANES:(lh + 1) * LANES]
                s1 = 4 if (dil > 4 and dil % 4 == 0) else 1
                s2 = dil // s1
                if s1 > 1:
                    mid = ybuf.at[nbuf_used % PROJ_STAGE_SLABS]
                    nbuf_used += 1
                    part = y.shape[0] // s1
                    for r1 in range(s1):
                        mid[r1 * part:(r1 + 1) * part, :] = buf[pl.ds(r1, part, stride=s1), :]
                else:
                    mid, part = buf, y.shape[0]
                for r1 in range(s1):
                    for r2 in range(s2):
                        col = (r1 + s1 * r2) * out_widths[oi] + oc + lh * LANES
                        out_refs[oi][:, col:col + LANES] = mid[pl.ds(r1 * part + r2, rows, stride=s2), :].astype(BF16)


def _proj(x2d, norm_gain, w, gain, tiles, out_widths, tm, out_dils=None):
    t, d = x2d.shape
    n = w.shape[1]
    tm = min(tm, t)
    out_dils = tuple(out_dils) if out_dils is not None else (1,) * len(out_widths)
    assert all(tm % (16 * dl) == 0 for dl in out_dils)
    mseg = jnp.asarray(
        np.stack([_seg_matrix(COL_TILE, [(s, 64) for s in range(0, COL_TILE, 64)]),
                  _seg_matrix(COL_TILE, [(s, 128) for s in range(0, COL_TILE, 128)])]), BF16)
    return pl.pallas_call(
        functools.partial(_proj_kernel, tiles=tuple(tiles), out_widths=tuple(out_widths), out_dils=out_dils),
        grid=(t // tm,),
        in_specs=[
            pl.BlockSpec((tm, d), lambda i: (i, 0)),
            _resident((1, d), lambda i: (0, 0)),
            _resident((d, n), lambda i: (0, 0)),
            _resident((1, n), lambda i: (0, 0)),
            _resident((2, COL_TILE, COL_TILE), lambda i: (0, 0, 0)),
        ],
        out_specs=[pl.BlockSpec((tm // dl, dl * wd), lambda i: (i, 0)) for wd, dl in zip(out_widths, out_dils)],
        out_shape=[jax.ShapeDtypeStruct((t // dl, dl * wd), BF16) for wd, dl in zip(out_widths, out_dils)],
        scratch_shapes=[pltpu.VMEM((PROJ_STAGE_SLABS, tm, LANES), F32)] if max(out_dils) > 1 else [],
        compiler_params=_cparams(("parallel",)),
        name="proj",
    )(x2d, norm_gain.reshape(1, d), w, gain, mseg)


def _main_plan():
    zero = N_IN
    cols, gains, tiles = [], [], []
    widths = [3 * 512, 512, 1024] + [3 * 512] * N_DIL + [512]

    def add(out_idx, out_col, src_cols, seg, gain_key):
        assert len(src_cols) % COL_TILE == 0
        c0 = len(cols)
        cols.extend(src_cols)
        gains.extend([gain_key] * len(src_cols))
        for k in range(len(src_cols) // COL_TILE):
            tiles.append((c0 + k * COL_TILE, seg, out_idx, out_col + k * COL_TILE))

    hm = [m * A_HEADS * A_DK + h * A_DK + d for h in range(A_HEADS) for m in range(2) for d in range(A_DK)]
    add(0, 0, [OFF_A_Q + c for c in hm], SEG64, "a_q")
    add(0, 512, [OFF_A_K + c for c in hm], SEG64, "a_k")
    add(0, 1024, [OFF_A_V + c for c in range(A_HEADS * A_DV)], SEG_NONE, None)
    braw = ([OFF_B_CQ + c for c in range(B_Q_LORA)] + [OFF_B_CKV + c for c in range(B_KV_LORA)]
            + [zero] * B_NOPE + [OFF_B_KR + c for c in range(B_ROPE)] + [zero] * (LANES - B_NOPE - B_ROPE))
    add(1, 0, braw, SEG_NONE, None)
    add(2, 0, [OFF_C_Q + c for c in range(C_QH * C_HD)], SEG64, "c_q")
    dup = [kv * C_HD + d for kv in range(C_KVH) for _ in range(2) for d in range(C_HD)]
    add(2, 512, [OFF_C_K + c for c in dup], SEG64, "c_k")
    add(2, 768, [OFF_C_V + c for c in dup], SEG_NONE, None)
    for g in range(N_DIL):
        gsl = [g * D_HEADS * D_HD + c for c in range(D_HEADS * D_HD)]
        add(3 + g, 0, [OFF_D_Q + c for c in gsl], SEG64, "d_q")
        add(3 + g, 512, [OFF_D_K + c for c in gsl], SEG64, "d_k")
        add(3 + g, 1024, [OFF_D_V + c for c in gsl], SEG_NONE, None)
    add(3 + N_DIL, 0, [OFF_M_Q + c for c in range(M_HEADS * M_HD)], SEG128, "m_q")
    return np.asarray(cols, np.int32), gains, tiles, widths


_MAIN_COLS, _MAIN_GAINS, _MAIN_TILES, _MAIN_WIDTHS = _main_plan()


def _main_gain_vector(p):
    per_key = {
        "a_q": p["a_qn"] * (A_DK ** -0.5 * LOG2E), "a_k": p["a_kn"],
        "c_q": p["c_qn"] * (C_HD ** -0.5 * LOG2E), "c_k": p["c_kn"],
        "d_q": p["d_qn"] * (D_HD ** -0.5 * LOG2E), "d_k": p["d_kn"],
        "m_q": p["m_qn"] * (M_HD ** -0.5),
    }
    n = len(_MAIN_GAINS)
    pieces, k = [], 0
    while k < n:
        key = _MAIN_GAINS[k]
        k2 = k
        while k2 < n and _MAIN_GAINS[k2] == key:
            k2 += 1
        if key is None:
            pieces.append(jnp.ones((k2 - k,), F32))
        else:
            gvec = per_key[key].astype(F32)
            pieces.append(jnp.tile(gvec, (k2 - k) // gvec.shape[0]))
        k = k2
    return jnp.concatenate(pieces).reshape(1, n)


def _mla_prep_kernel(b_ref, wq_ref, wk_ref, wv_ref, mseg_ref, cqg_ref, ckvg_ref,
                     qg_ref, kg_ref, ct_ref, s1_ref, s2_ref, q_out, k_out, v_out):
    braw = b_ref[...]
    cq = braw[:, :B_Q_LORA].astype(F32)
    cqn = ((cq * lax.rsqrt(jnp.mean(cq * cq, axis=-1, keepdims=True) + EPS)) * cqg_ref[...]).astype(BF16)
    ckv = braw[:, B_Q_LORA:B_Q_LORA + B_KV_LORA].astype(F32)
    ckvn = ((ckv * lax.rsqrt(jnp.mean(ckv * ckv, axis=-1, keepdims=True) + EPS)) * ckvg_ref[...]).astype(BF16)
    qb = jnp.dot(cqn, wq_ref[...], preferred_element_type=F32)
    kb = jnp.dot(ckvn, wk_ref[...], preferred_element_type=F32)
    v_out[...] = jnp.dot(ckvn, wv_ref[...], preferred_element_type=F32).astype(BF16)
    ct, s1, s2 = ct_ref[...], s1_ref[...], s2_ref[...]
    half = B_ROPE // 2

    def norm_slab(y, gain_ref):
        segms = jnp.dot((y * y).astype(BF16), mseg_ref[...], preferred_element_type=F32)
        return (y * lax.rsqrt(segms + EPS)) * gain_ref[...]

    def rope(yn):
        return yn * ct + pltpu.roll(yn, half, 1) * s1 + pltpu.roll(yn, LANES - half, 1) * s2

    kr = rope(norm_slab(braw[:, B_Q_LORA + B_KV_LORA:].astype(F32), kg_ref))
    for h in range(B_HEADS):
        sl = slice(h * LANES, (h + 1) * LANES)
        k_out[:, sl] = (norm_slab(kb[:, sl], kg_ref) + kr).astype(BF16)
        q_out[:, sl] = rope(norm_slab(qb[:, sl], qg_ref)).astype(BF16)


def _rope_tables(s):
    half = B_ROPE // 2
    inv = ROPE_THETA ** (-jnp.arange(half, dtype=F32) / half)
    ang = jnp.arange(s, dtype=F32)[:, None] * inv[None, :]
    cos, sin = jnp.cos(ang), jnp.sin(ang)
    one = jnp.ones((s, B_NOPE), F32)
    zpad = jnp.zeros((s, LANES - B_NOPE - B_ROPE), F32)
    z64 = jnp.zeros((s, B_NOPE), F32)
    zh = jnp.zeros((s, half), F32)
    ct = jnp.concatenate([one, cos, cos, zpad], axis=1)
    s1 = jnp.concatenate([z64, zh, sin, zpad], axis=1)
    s2 = jnp.concatenate([z64, -sin, zh, zpad], axis=1)
    return ct, s1, s2


def _mla_prep(braw, p, bsz, s, tm):
    t = braw.shape[0]
    tm = min(tm, s)
    pad = LANES - B_NOPE - B_ROPE
    wq = p["w_qb"].reshape(B_Q_LORA, B_HEADS, B_NOPE + B_ROPE)
    wq = jnp.pad(wq, ((0, 0), (0, 0), (0, pad))).reshape(B_Q_LORA, B_HEADS * LANES).astype(BF16)
    wkv = p["w_kvb"].reshape(B_KV_LORA, B_HEADS, B_NOPE + B_DV)
    wk = jnp.pad(wkv[:, :, :B_NOPE], ((0, 0), (0, 0), (0, LANES - B_NOPE))).reshape(B_KV_LORA, B_HEADS * LANES).astype(BF16)
    wv = wkv[:, :, B_NOPE:].reshape(B_KV_LORA, B_HEADS * B_DV).astype(BF16)
    mseg =_seg_matrix(LANES, [(0, B_NOPE), (B_NOPE, B_ROPE)])
    scale = (B_NOPE + B_ROPE) ** -0.5 * LOG2E
    zp = jnp.zeros((pad,), F32)
    qg = jnp.concatenate([p["b_qn"].astype(F32) * scale, zp]).reshape(1, LANES)
    kg = jnp.concatenate([p["b_kn"].astype(F32), zp]).reshape(1, LANES)
    ct, s1, s2 = _rope_tables(s)
    nst = s // tm
    full = lambda shape: _resident(shape, lambda b, i: tuple(0 for _ in shape))
    row = lambda wd: pl.BlockSpec((tm, wd), lambda b, i: (b * nst + i, 0))
    tab = pl.BlockSpec((tm, LANES), lambda b, i: (i, 0))
    return pl.pallas_call(
        _mla_prep_kernel,
        grid=(bsz, nst),
        in_specs=[row(512), full((B_Q_LORA, 512)), full((B_KV_LORA, 512)), full((B_KV_LORA, 512)),
                  full((LANES, LANES)), full((1, B_Q_LORA)), full((1, B_KV_LORA)),
                  full((1, LANES)), full((1, LANES)), tab, tab, tab],
        out_specs=[row(512), row(512), row(512)],
        out_shape=[jax.ShapeDtypeStruct((t, 512), BF16)] * 3,
        compiler_params=_cparams(("parallel", "parallel")),
        name="mla_prep",
    )(braw, wq, wk, wv, jnp.asarray(mseg, BF16),
      p["b_cqn"].astype(F32).reshape(1, -1), p["b_ckvn"].astype(F32).reshape(1, -1), qg, kg, ct, s1, s2)


def _flash_kernel(slopes_ref, lam_ref, hn_ref, q_ref, k_ref, v_ref, o_ref, m_sc, l_sc, acc_sc,
                  *, nmaps, alibi, tq, tk, seq, lam_init):
    h = pl.program_id(1)
    q0 = pl.program_id(2) * tq
    q = q_ref[...]
    if nmaps == 2:
        lane = lax.broadcasted_iota(jnp.int32, q.shape, 1)
        qs = [jnp.where(lane < A_DK, q, jnp.zeros_like(q)), jnp.where(lane >= A_DK, q, jnp.zeros_like(q))]
    else:
        qs = [q]
    m_sc[...] = jnp.full(m_sc.shape, NEG_INF, F32)
    l_sc[...] = jnp.zeros(l_sc.shape, F32)
    acc_sc[...] = jnp.zeros(acc_sc.shape, F32)
    if alibi:
        slope2 = slopes_ref[h] * LOG2E
        dmat = (lax.broadcasted_iota(jnp.int32, (tq, tk), 0)
                - lax.broadcasted_iota(jnp.int32, (tq, tk), 1)).astype(F32)

    def body(j, carry):
        k0 = pl.multiple_of(j * tk, tk)
        k = k_ref[pl.ds(k0, tk), :]
        v = v_ref[pl.ds(k0, tk), :]
        if alibi:
            bias = -slope2 * jnp.abs(dmat + (q0 - k0).astype(F32))
        for m in range(nmaps):
            s = lax.dot_general(qs[m], k, (((1,), (1,)), ((), ())), preferred_element_type=F32)
            if alibi:
                s = s + bias
            m_old = m_sc[m]
            m_new = jnp.maximum(m_old, jnp.max(s, axis=1, keepdims=True))
            alpha = jnp.exp2(m_old - m_new)
            pr = jnp.exp2(s - m_new)
            l_sc[m] = alpha * l_sc[m] + jnp.sum(pr, axis=1, keepdims=True)
            acc_sc[m] = alpha * acc_sc[m] + jnp.dot(pr.astype(BF16), v, preferred_element_type=F32)
            m_sc[m] = m_new
        return carry

    lax.fori_loop(0, seq // tk, body, 0)
    outs = [acc_sc[m] / l_sc[m] for m in range(nmaps)]
    o_ref[...] = _flash_epilogue(outs, lam_ref, hn_ref, lam_init).astype(BF16)


def _flash_epilogue(outs, lam_ref, hn_ref, lam_init):
    if len(outs) == 1:
        return outs[0]
    lf = lam_ref[...]
    e1 = jnp.exp(jnp.sum(lf[0:1] * lf[1:2], axis=1, keepdims=True))
    e2 = jnp.exp(jnp.sum(lf[2:3] * lf[3:4], axis=1, keepdims=True))
    lam = e1 - e2 + lam_init
    o = outs[0] - lam * outs[1]
    o = (o * lax.rsqrt(jnp.mean(o * o, axis=-1, keepdims=True) + EPS)) * hn_ref[...]
    return o * (1.0 - lam_init)


def _flash_fixed_kernel(shift_ref, slopes_ref, lam_ref, hn_ref, q_ref, k_ref, v_ref, o_ref, v1_sc, acc_sc,
                        *bias_sc, nmaps, alibi, tq, tk, seq, lam_init, slopes_static):
    h = pl.program_id(1)
    i = pl.program_id(2)
    q0 = i * tq

    @pl.when(i == 0)
    def _():
        v1_sc[:, :LANES] = v_ref[...]
        v1_sc[:, LANES:] = jnp.ones((seq, LANES), BF16)
        if alibi:
            bias_sc[0][...] = (lax.broadcasted_iota(jnp.int32, (tq, tk), 0)
                               - lax.broadcasted_iota(jnp.int32, (tq, tk), 1)).astype(F32) * (slopes_ref[h] * LOG2E)

    q = q_ref[...]
    if nmaps == 2:
        lane = lax.broadcasted_iota(jnp.int32, q.shape, 1)
        q_all = jnp.concatenate([jnp.where(lane < A_DK, q, jnp.zeros_like(q)),
                                 jnp.where(lane >= A_DK, q, jnp.zeros_like(q))], axis=0)
    else:
        q_all = q
    shift = shift_ref[0]
    if alibi:
        slope2 = slopes_ref[h] * LOG2E
        t0_sc = bias_sc[0]

    def step(j, first):
        k0 = j * tk if isinstance(j, int) else pl.multiple_of(j * tk, tk)
        k = k_ref[pl.ds(k0, tk), :]
        v1 = v1_sc[pl.ds(k0, tk), :]
        if alibi:
            sub = jnp.abs(t0_sc[...] + slope2 * (q0 - k0).astype(F32)) + shift
        s = lax.dot_general(q_all, k, (((1,), (1,)), ((), ())), preferred_element_type=F32)
        ps = [jnp.exp2(s[m * tq:(m + 1) * tq] - (sub if alibi else shift)).astype(BF16) for m in range(nmaps)]
        pr = ps[0] if nmaps == 1 else jnp.concatenate(ps, axis=0)
        pv = jnp.dot(pr, v1, preferred_element_type=F32)
        if first:
            acc_sc[...] = pv
        else:
            acc_sc[...] += pv

    nkv = seq // tk
    if alibi:
        assert tq == tk
        for hd, slope in enumerate(slopes_static):
            reach = int(math.floor((ZERO_EXP2 / (slope * LOG2E) - 1.0) / tk)) + 1
            cnt = min(nkv, 2 * reach + 1)

            @pl.when(h == hd)
            def _(reach=reach, cnt=cnt):
                start = jnp.clip(i - reach, 0, nkv - cnt) if cnt < nkv else 0
                for jj in range(cnt):
                    step(start + jj, jj == 0)
    else:
        for jj in range(nkv):
            step(jj, jj == 0)
    outs = [acc_sc[m * tq:(m + 1) * tq, :LANES] / acc_sc[m * tq:(m + 1) * tq, LANES:] for m in range(nmaps)]
    o_ref[...] = _flash_epilogue(outs, lam_ref, hn_ref, lam_init).astype(BF16)


def _flash(qa, ka, va, qcb, kcb, vcb, bsz, s, heads, nmaps, alibi, slopes, lam, hn, lam_init, bound2):
    t = qa.shape[0]
    name = "flash_diff" if nmaps == 2 else "flash_mla"
    shift = bound2.reshape(1).astype(F32)

    def call(kern, tq, tk, scratch, extra_in, extra_args, suffix):
        tq_, tk_ = min(tq, s), min(tk, s)
        nq = s // tq_
        return pl.pallas_call(
            functools.partial(kern, nmaps=nmaps, alibi=alibi, tq=tq_, tk=tk_, seq=s, lam_init=lam_init),
            grid=(bsz, heads, nq),
            in_specs=extra_in + [
                pl.BlockSpec(memory_space=pltpu.SMEM),
                _resident(lam.shape, lambda b, h, i: (0, 0)),
                _resident(hn.shape, lambda b, h, i: (0, 0)),
                pl.BlockSpec((tq_, LANES), lambda b, h, i: (b * nq + i, qcb + h)),
                pl.BlockSpec((s, LANES), lambda b, h, i: (b, kcb + h)),
                pl.BlockSpec((s, LANES), lambda b, h, i: (b, vcb + h)),
            ],
            out_specs=pl.BlockSpec((tq_, LANES), lambda b, h, i: (b * nq + i, h)),
            out_shape=jax.ShapeDtypeStruct((t, heads * LANES), BF16),
            scratch_shapes=scratch(tq_),
            compiler_params=_cparams(("parallel", "parallel", "arbitrary")),
            name=name + suffix,
        )(*extra_args, slopes, lam, hn, qa, ka, va)

    def fixed():
        return call(functools.partial(_flash_fixed_kernel, slopes_static=_alibi_slopes_static(heads)),
                    FLASH_TQ if alibi else FLASH_TQ_PLAIN, FLASH_TK,
                    lambda tq_: [pltpu.VMEM((s, 2 * LANES), BF16), pltpu.VMEM((nmaps * tq_, 2 * LANES), F32)]
                    + ([pltpu.VMEM((tq_, min(FLASH_TK, s)), F32)] if alibi else []),
                    [pl.BlockSpec(memory_space=pltpu.SMEM)], [shift], "_fixed")

    def online():
        return call(_flash_kernel, FLASH_ONLINE_TQ, FLASH_TK,
                    lambda tq_: [pltpu.VMEM((nmaps, tq_, 1), F32), pltpu.VMEM((nmaps, tq_, 1), F32),
                                 pltpu.VMEM((nmaps, tq_, LANES), F32)],
                    [], [], "_online")

    return lax.cond(bound2 <= MAX_FIXED_SHIFT, fixed, online)


def _band_kernel(slopes_ref, sink_ref, q_ref, k_ref, v_ref, *outs, window, qb, kw, ut, u_len, dil, use_sink, emit_lse):
    o_ref = outs[0]
    slab = pl.program_id(0)
    ubase = pl.program_id(3) * ut
    lane = lax.broadcasted_iota(jnp.int32, (kw, LANES), 1)
    qlane = lax.broadcasted_iota(jnp.int32, (qb, LANES), 1)
    dmat = (lax.broadcasted_iota(jnp.int32, (qb, kw), 0) - lax.broadcasted_iota(jnp.int32, (qb, kw), 1))
    for sb in range(ut // qb):
        u0 = ubase + sb * qb
        start = pl.multiple_of(jnp.clip(u0 - window, 0, u_len - kw), 16)
        q = q_ref[sb * qb:(sb + 1) * qb, :]
        k = k_ref[pl.ds(start, kw), :]
        v = v_ref[pl.ds(start, kw), :]
        rel = dmat + (u0 - start)
        valid = jnp.abs(rel) <= window
        dist = jnp.abs(rel).astype(F32) * float(dil)
        o_acc = jnp.zeros((qb, LANES), F32)
        lse_acc = jnp.zeros((qb, LANES), F32)
        for hh in range(2):
            head = slab * 2 + hh
            sel_q = (qlane < 64) if hh == 0 else (qlane >= 64)
            sel_v = (lane < 64) if hh == 0 else (lane >= 64)
            qh = jnp.where(sel_q, q, jnp.zeros_like(q))
            vh = jnp.where(sel_v, v, jnp.zeros_like(v))
            s = lax.dot_general(qh, k, (((1,), (1,)), ((), ())), preferred_element_type=F32)
            logits = jnp.where(valid, s - (slopes_ref[head] * LOG2E) * dist, NEG_INF)
            mx = jnp.max(logits, axis=1, keepdims=True)
            if use_sink:
                sk = sink_ref[head]
                mx = jnp.maximum(mx, sk)
            e = jnp.exp2(logits - mx)
            den = jnp.sum(e, axis=1, keepdims=True)
            if use_sink:
                den = den + jnp.exp2(sk - mx)
            pv = jnp.dot(e.astype(BF16), vh, preferred_element_type=F32)
            o_acc = o_acc + pv / den
            if emit_lse:
                lse_acc = lse_acc + jnp.where(sel_q, mx + jnp.log2(den), 0.0)
        o_ref[sb * qb:(sb + 1) * qb, :] = o_acc.astype(BF16)
        if emit_lse:
            outs[1][sb * qb:(sb + 1) * qb, :] = lse_acc


def _band_fixed_kernel(shift_ref, slopes_ref, sink_ref, q_ref, k_ref, v_ref, *outs,
                       window, qb, kw, ut, u_len, dil, use_sink, emit_lse, sps):
    o_ref, sub_sc = outs[0], outs[-1]
    slab0 = pl.program_id(0) * sps
    ubase = pl.program_id(3) * ut
    shift = shift_ref[0]
    qlane = lax.broadcasted_iota(jnp.int32, (2 * qb, LANES), 1)
    qrow = lax.broadcasted_iota(jnp.int32, (2 * qb, LANES), 0)
    keep = (qlane < 64) == (qrow < qb)
    first = lax.broadcasted_iota(jnp.int32, (qb, LANES), 1) < 64
    ones = jnp.ones((kw, LANES), BF16)

    @pl.when((pl.program_id(1) == 0) & (pl.program_id(2) == 0) & (pl.program_id(3) == 0))
    def _():
        dmat = (lax.broadcasted_iota(jnp.int32, (qb, kw), 0) - lax.broadcasted_iota(jnp.int32, (qb, kw), 1))
        for case in range(3):
            absrel = jnp.abs(dmat + case * window)
            for hh in range(2 * sps):
                slope2 = slopes_ref[slab0 * 2 + hh] * (LOG2E * dil)
                sub_sc[case, hh // 2, (hh % 2) * qb:(hh % 2 + 1) * qb] = jnp.where(
                    absrel <= window, absrel.astype(F32) * slope2 + shift, -NEG_INF)

    for sb in range(ut // qb):
        u0 = ubase + sb * qb
        start = pl.multiple_of(jnp.clip(u0 - window, 0, u_len - kw), 16)
        case = (u0 - start) // window
        for sl in range(sps):
            lanes = slice(sl * LANES, (sl + 1) * LANES)
            q = q_ref[sb * qb:(sb + 1) * qb, lanes]
            k = k_ref[pl.ds(start, kw), lanes]
            v1 = jnp.concatenate([v_ref[pl.ds(start, kw), lanes], ones], axis=1)
            q2 = jnp.concatenate([q, q], axis=0)
            q2 = jnp.where(keep, q2, jnp.zeros_like(q2))
            s2 = lax.dot_general(q2, k, (((1,), (1,)), ((), ())), preferred_element_type=F32)
            r = jnp.dot(jnp.exp2(s2 - sub_sc[case, sl]).astype(BF16), v1, preferred_element_type=F32)
            za, zb = r[:qb, LANES:], r[qb:, LANES:]
            if use_sink:
                head = (slab0 + sl) * 2
                za = za + jnp.exp2(jnp.full((1, 1), sink_ref[head] - shift, F32))
                zb = zb + jnp.exp2(jnp.full((1, 1), sink_ref[head + 1] - shift, F32))
            o_ref[sb * qb:(sb + 1) * qb, lanes] = jnp.where(first, r[:qb, :LANES] / za, r[qb:, :LANES] / zb).astype(BF16)
            if emit_lse:
                outs[1][sb * qb:(sb + 1) * qb, lanes] = jnp.where(first, jnp.log2(za), jnp.log2(zb)) + shift


def _band(srcv, qcb, kcb, vcb, bsz, s, dil, ncb_src, window, slopes, sink, use_sink, emit_lse, ut, bound2):
    u_len = s // dil
    assert srcv.shape[0] == bsz * u_len
    qb = min(LANES, u_len)
    kw = min(qb + 2 * window, u_len)
    nslab = 4
    out_shapes = [jax.ShapeDtypeStruct((bsz * u_len, dil * 512), BF16)]
    if emit_lse:
        out_shapes.append(jax.ShapeDtypeStruct((bsz * u_len, dil * 512), F32))
    smem = pl.BlockSpec(memory_space=pltpu.SMEM)

    def call(kern, extra_in, extra_args, suffix, sps, rows, scratch=()):
        wd = sps * LANES
        ut_ = min(rows, u_len)
        nu = u_len // ut_
        assert all(f(sl) == f(0) + sl for f in (qcb, kcb, vcb) for sl in range(sps)) and ncb_src % sps == 0
        out_spec = pl.BlockSpec((ut_, wd), lambda g, b, r, i: (b * nu + i, r * (nslab // sps) + g))
        return pl.pallas_call(
            functools.partial(kern, window=window, qb=qb, kw=kw, ut=ut_, u_len=u_len, dil=dil,
                              use_sink=use_sink, emit_lse=emit_lse),
            grid=(nslab // sps, bsz, dil, nu),
            scratch_shapes=list(scratch),
            in_specs=extra_in + [
                smem, smem,
                pl.BlockSpec((ut_, wd), lambda g, b, r, i: (b * nu + i, (r * ncb_src + qcb(g * sps)) // sps)),
                pl.BlockSpec((u_len, wd), lambda g, b, r, i: (b, (r * ncb_src + kcb(g * sps)) // sps)),
                pl.BlockSpec((u_len, wd), lambda g, b, r, i: (b, (r * ncb_src + vcb(g * sps)) // sps)),
            ],
            out_specs=[out_spec] * len(out_shapes),
            out_shape=out_shapes,
            compiler_params=_cparams(("arbitrary", "arbitrary", "arbitrary", "arbitrary")),
            name="band_w%d_d%d%s" % (window, dil, suffix),
        )(*extra_args, slopes, sink, srcv, srcv, srcv)

    sps = nslab if (dil > 1 and kcb(0) % nslab == 0 and vcb(0) % nslab == 0 and qcb(0) % nslab == 0) else 1
    res = lax.cond(bound2 <= MAX_FIXED_SHIFT,
                   lambda: call(functools.partial(_band_fixed_kernel, sps=sps), [smem],
                                [bound2.reshape(1).astype(F32)], "_fixed", sps, ut // sps,
                                [pltpu.VMEM((3, sps, 2 * qb, kw), F32)]),
                   lambda: call(_band_kernel, [], [], "_online", 1, ut))
    return list(res)


def _memattn_kernel(q_ref, k_ref, v_ref, o_ref):
    for h in range(M_HEADS):
        sl = slice(h * M_HD, (h + 1) * M_HD)
        s = lax.dot_general(q_ref[:, sl], k_ref[:, sl], (((1,), (1,)), ((), ())), preferred_element_type=F32)
        mx = jnp.max(s, axis=1, keepdims=True)
        e = jnp.exp(s - mx)
        den = jnp.sum(e, axis=1, keepdims=True)
        pv = jnp.dot(e.astype(BF16), v_ref[:, sl], preferred_element_type=F32)
        o_ref[:, sl] = (pv / den).astype(BF16)


def _memattn(qm, mk, mv, bsz, s, tq):
    t = qm.shape[0]
    tq = min(tq, s)
    nq = s // tq
    wd = M_HEADS * M_HD
    return pl.pallas_call(
        _memattn_kernel,
        grid=(bsz, nq),
        in_specs=[pl.BlockSpec((tq, wd), lambda b, i: (b * nq + i, 0)),
                  pl.BlockSpec((N_MEM, wd), lambda b, i: (b, 0)),
                  pl.BlockSpec((N_MEM, wd), lambda b, i: (b, 0))],
        out_specs=pl.BlockSpec((tq, wd), lambda b, i: (b * nq + i, 0)),
        out_shape=jax.ShapeDtypeStruct((t, wd), BF16),
        compiler_params=_cparams(("parallel", "parallel")),
        name="memattn",
    )(qm, mk, mv)


def _merge_kernel(x_ref, ng_ref, wz_ref, wg_ref, bg_ref, wbr_ref, wout_ref,
                  oa_ref, ob_ref, oc_ref, mq_ref, mk_ref, mv_ref, od0_ref, od1_ref, od2_ref, l0_ref, l1_ref, l2_ref,
                  y_ref, *stage, dils, sub_rows):
    tm = x_ref.shape[0]
    nsl = BRANCH_W // LANES

    def token_major(ref, dil, slabs):
        if dil == 1:
            return ref[...].astype(F32)
        rows = tm // dil
        for r in range(dil):
            for c in range(nsl):
                col = r * BRANCH_W + c * LANES
                slabs[c, pl.ds(r, rows, stride=dil), :] = ref[:, col:col + LANES].astype(F32)
        return jnp.concatenate([slabs[c] for c in range(nsl)], axis=1)

    def dilated_mixture():
        st = iter(stage)
        ls = [token_major(ref, dl, next(st) if dl > 1 else None) for ref, dl in zip((l0_ref, l1_ref, l2_ref), dils)]
        os_ = [token_major(ref, dl, next(st) if dl > 1 else None) for ref, dl in zip((od0_ref, od1_ref, od2_ref), dils)]
        mx = jnp.maximum(jnp.maximum(ls[0], ls[1]), ls[2])
        e0, e1, e2 = jnp.exp2(ls[0] - mx), jnp.exp2(ls[1] - mx), jnp.exp2(ls[2] - mx)
        den = e0 + e1 + e2
        return (e0 / den) * os_[0] + (e1 / den) * os_[1] + (e2 / den) * os_[2]

    def memory_attention(rs):
        outs = []
        for hh in range(M_HEADS):
            sl = slice(hh * M_HD, (hh + 1) * M_HD)
            sc = lax.dot_general(mq_ref[rs, sl], mk_ref[:, sl], (((1,), (1,)), ((), ())), preferred_element_type=F32)
            e = jnp.exp(sc - jnp.max(sc, axis=1, keepdims=True))
            pv = jnp.dot(e.astype(BF16), mv_ref[:, sl], preferred_element_type=F32)
            outs.append(pv / jnp.sum(e, axis=1, keepdims=True))
        return jnp.concatenate(outs, axis=1)

    d = x_ref.shape[1]
    od = dilated_mixture()
    for rg in range(tm // sub_rows):
        rs = slice(rg * sub_rows, (rg + 1) * sub_rows)
        x = x_ref[rs, :]
        ms = jnp.mean(x * x, axis=-1, keepdims=True)
        h = ((x * lax.rsqrt(ms + EPS)) * ng_ref[...]).astype(BF16)
        branch_in = (lambda: oa_ref[rs, :].astype(F32), lambda: ob_ref[rs, :].astype(F32),
                     lambda: oc_ref[rs, :].astype(F32), lambda: od[rs], lambda: memory_attention(rs))
        acc = jnp.zeros((sub_rows, d), F32)
        for br in range(N_BRANCH):
            z = jnp.dot(h, wz_ref[:, br * BRANCH_W:(br + 1) * BRANCH_W], preferred_element_type=F32)
            g = jnp.dot(h, wg_ref[:, br * d:(br + 1) * d], preferred_element_type=F32) + bg_ref[br:br + 1, :]
            u = (branch_in[br]() * (z / (1.0 + jnp.exp(-z)))).astype(BF16)
            tbr = jnp.dot(u, wbr_ref[br], preferred_element_type=F32)
            acc = acc + tbr / (1.0 + jnp.exp(-g))
        y_ref[rs, :] = x + jnp.dot(acc.astype(BF16), wout_ref[...], preferred_element_type=F32)


def _merge(x2d, p, wz, wg, oa, ob, oc, mq, mk, mv, seq, ods, lses, tm):
    t, d = x2d.shape
    tm = min(tm, t)
    rowf = lambda wd: pl.BlockSpec((tm, wd), lambda i: (i, 0))
    dils = tuple(dl for _, dl in DIL_PAIRS)
    assert all(tm % (16 * dl) == 0 for dl in dils)
    classf = [pl.BlockSpec((tm // dl, dl * BRANCH_W), lambda i: (i, 0)) for dl in dils]
    nstage = 2 * sum(1 for dl in dils if dl > 1)
    full = lambda shape: _resident(shape, lambda i: tuple(0 for _ in shape))
    return pl.pallas_call(
        functools.partial(_merge_kernel, dils=dils, sub_rows=min(MERGE_SUB_ROWS, tm)),
        grid=(t // tm,),
        in_specs=[rowf(d), full((1, d)), full((d, N_BRANCH * BRANCH_W)), full((d, N_BRANCH * d)),
                  full((N_BRANCH, d)), full((N_BRANCH, BRANCH_W, d)), full((d, d))]
        + [rowf(BRANCH_W)] * 4
        + [pl.BlockSpec((N_MEM, M_HEADS * M_HD), lambda i: (i // (seq // tm), 0))] * 2 + classf + classf,
        out_specs=rowf(d),
        out_shape=jax.ShapeDtypeStruct((t, d), F32),
        scratch_shapes=[pltpu.VMEM((BRANCH_W // LANES, tm, LANES), F32)] * nstage,
        compiler_params=_cparams(("parallel",)),
        name="merge",
    )(x2d, p["norm_g"].astype(F32).reshape(1, d), wz, wg, p["b_gate"].astype(F32),
      p["w_br_bf16"], p["w_out_bf16"], oa, ob, oc, mq, mk, mv, *ods, *lses)


def _alibi_slopes_static(n):
    return tuple(float(v) for v in 2.0 ** (-8.0 * np.arange(1, n + 1, dtype=np.float64) / n))


def _alibi_slopes(n):
    return jnp.asarray(_alibi_slopes_static(n), F32)


def _encoder_layer(x, mem, layer_idx, p):
    bsz, s, d = x.shape
    t = bsz * s
    x2d = x.reshape(t, d)

    pa, pb, pc, pd0, pd1, pd2, pm = _proj(x2d, p["norm_g"].astype(F32), p["w_main"], p["gain_main"],
                                          _MAIN_TILES, _MAIN_WIDTHS, tm=PROJ_ROWS,
                                          out_dils=[1, 1, 1] + [dl for _, dl in DIL_PAIRS] + [1])

    lam_init = 0.8 - 0.6 * math.exp(-0.3 * layer_idx)
    bound_a = (BOUND_MARGIN * A_DK * (A_DK ** -0.5 * LOG2E)
               * jnp.max(jnp.abs(p["a_qn"].astype(F32))) * jnp.max(jnp.abs(p["a_kn"].astype(F32))))
    oa = _flash(pa, pa, pa, 0, 4, 8, bsz, s, A_HEADS, 2, True, _alibi_slopes(A_HEADS),
                p["a_lam"].astype(F32), p["a_hn"].astype(F32).reshape(1, A_DV), lam_init, bound_a)

    qb, kb, vb = _mla_prep(pb, p, bsz, s, tm=MLA_PREP_ROWS)
    dummy_lam = jnp.zeros((4, A_DK), F32)
    dummy_hn = jnp.ones((1, A_DV), F32)
    dqk = B_NOPE + B_ROPE
    bound_b = (BOUND_MARGIN * dqk * (dqk ** -0.5 * LOG2E)
               * jnp.max(jnp.abs(p["b_qn"].astype(F32))) * jnp.max(jnp.abs(p["b_kn"].astype(F32))))
    ob = _flash(qb, kb, vb, 0, 0, 0, bsz, s, B_HEADS, 1, False, _alibi_slopes(B_HEADS),
                dummy_lam, dummy_hn, 0.0, bound_b)

    sink2 = p["c_sink"].astype(F32) * LOG2E
    bound_c = jnp.maximum(
        BOUND_MARGIN * C_HD * (C_HD ** -0.5 * LOG2E)
        * jnp.max(jnp.abs(p["c_qn"].astype(F32))) * jnp.max(jnp.abs(p["c_kn"].astype(F32))),
        jnp.max(sink2))
    oc, = _band(pc, lambda sl: sl, lambda sl: 4 + sl // 2, lambda sl: 6 + sl // 2, bsz, s, 1, 8,
                C_WINDOW, _alibi_slopes(C_QH), sink2, True, False, BAND_ROWS, bound_c)

    ods, lses = [], []
    zero_sink = jnp.zeros((D_HEADS,), F32)
    bound_d = (BOUND_MARGIN * D_HD * (D_HD ** -0.5 * LOG2E)
               * jnp.max(jnp.abs(p["d_qn"].astype(F32))) * jnp.max(jnp.abs(p["d_kn"].astype(F32))))
    for g, (win, dil) in enumerate(DIL_PAIRS):
        og, lg = _band((pd0, pd1, pd2)[g], lambda sl: sl, lambda sl: 4 + sl,
                       lambda sl: 8 + sl, bsz, s, dil, 12, win // (2 * dil),
                       _alibi_slopes(D_HEADS), zero_sink, False, True, BAND_ROWS, bound_d)
        ods.append(og)
        lses.append(lg)

    mtiles = [(0, SEG128, 0, 0), (256, SEG128, 0, 256), (512, SEG_NONE, 1, 0), (768, SEG_NONE, 1, 256)]
    mk, mv = _proj(mem.reshape(bsz * N_MEM, d), p["m_norm"].astype(F32), p["w_mem_kv_bf16"], p["gain_mem"],
                   mtiles, [512, 512], tm=MEM_PROJ_ROWS)
    y = _merge(x2d, p, p["wz"], p["wg"], oa, ob, oc, pm, mk, mv, s, ods, lses, tm=MERGE_ROWS)
    return y.reshape(bsz, s, d)


def _prepare_layer(p):
    w_in = p["w_in"]
    pieces, k, n = [], 0, len(_MAIN_COLS)
    while k < n:
        k2 = k + 1
        if _MAIN_COLS[k] == N_IN:
            while k2 < n and _MAIN_COLS[k2] == N_IN:
                k2 += 1
            pieces.append(jnp.zeros((w_in.shape[0], k2 - k), BF16))
        else:
            while k2 < n and _MAIN_COLS[k2] == _MAIN_COLS[k2 - 1] + 1:
                k2 += 1
            pieces.append(w_in[:, int(_MAIN_COLS[k]):int(_MAIN_COLS[k]) + (k2 - k)].astype(BF16))
        k = k2
    return {
        "w_main": jnp.concatenate(pieces, axis=1),
        "gain_main": _main_gain_vector(p),
        "wz": w_in[:, OFF_Z:OFF_G].astype(BF16),
        "wg": w_in[:, OFF_G:].astype(BF16),
        "w_mem_kv_bf16": p["w_mem_kv"].astype(BF16),
        "gain_mem": jnp.concatenate([jnp.tile(p["m_kn"].astype(F32), M_HEADS),
                                     jnp.ones((M_HEADS * M_HD,), F32)]).reshape(1, -1),
        "w_br_bf16": p["w_br"].astype(BF16),
        "w_out_bf16": p["w_out"].astype(BF16),
    }


def kernel(x_prompt, x_sample, mem_prompt, mem_sample, norm_g, w_in, a_qn, a_kn, a_lam, a_hn, b_cqn, b_ckvn, w_qb, w_kvb, b_qn, b_kn, c_qn, c_kn, c_sink, d_qn, d_kn, m_norm, w_mem_kv, m_qn, m_kn, b_gate, w_br, w_out):
    depth = norm_g.shape[0]
    y_prompt, y_sample = x_prompt, x_sample
    for l in range(depth):
        p = {
            "norm_g": norm_g[l], "w_in": w_in[l],
            "a_qn": a_qn[l], "a_kn": a_kn[l], "a_lam": a_lam[l], "a_hn": a_hn[l],
            "b_cqn": b_cqn[l], "b_ckvn": b_ckvn[l], "w_qb": w_qb[l], "w_kvb": w_kvb[l],
            "b_qn": b_qn[l], "b_kn": b_kn[l],
            "c_qn": c_qn[l], "c_kn": c_kn[l], "c_sink": c_sink[l],
            "d_qn": d_qn[l], "d_kn": d_kn[l],
            "m_norm": m_norm[l], "w_mem_kv": w_mem_kv[l], "m_qn": m_qn[l], "m_kn": m_kn[l],
            "b_gate": b_gate[l], "w_br": w_br[l], "w_out": w_out[l],
        }
        p.update(_prepare_layer(p))
        y_prompt = _encoder_layer(y_prompt, mem_prompt, l, p)
        y_sample = _encoder_layer(y_sample, mem_sample, l, p)
    return (y_prompt, y_sample)
```
